```python
import math, functools
import jax, jax.numpy as jnp
from jax import lax
import numpy as np

D_MODEL = 1024
BATCH = 2
SEQ = 8192
DEPTH = 1
DEC_BATCH = 128
DEC_SEQ = 4
PAST_LEN = 2048
PAGE_SIZE = 128

GLA_HEADS = 4
GLA_DK = D_MODEL // 2
GLA_DV = D_MODEL
GLA_HEAD_DK = GLA_DK // GLA_HEADS
GLA_HEAD_DV = GLA_DV // GLA_HEADS
GLA_GATE_RANK = 16
GLA_GATE_TAU = 16.0
GLA_CHUNK = 64
DSA_HEADS = 8
DSA_HEAD_DIM = 64
DSA_WIDTH = DSA_HEADS * DSA_HEAD_DIM
IDX_HEADS = 4
IDX_DIM = 64
TOPK_MAX = 256
QUERY_BLOCK = 128
REL_BUCKETS = 32
REL_MAX_DIST = 128
D_FF = ((8 * D_MODEL + 3 * 256 - 1) // (3 * 256)) * 256
RMS_EPS = 1e-6
LN_EPS = 1e-6
SPLIT_SIZES = (GLA_DK, GLA_DK, GLA_DV, GLA_DV, GLA_GATE_RANK,
               DSA_WIDTH, DSA_WIDTH, DSA_WIDTH, IDX_HEADS * IDX_DIM, IDX_DIM, IDX_HEADS,
               D_MODEL, D_MODEL)
D_IN_PROJ = sum(SPLIT_SIZES)

kernel_name = "gla_dsa_gated_hybrid_step"


def _rmsnorm(x, g):
    xf = x.astype(jnp.float32)
    y = xf * lax.rsqrt(jnp.mean(xf * xf, axis=-1, keepdims=True) + RMS_EPS)
    return (y * g.astype(jnp.float32)).astype(x.dtype)


def _layernorm(x, g, b):
    xf = x.astype(jnp.float32)
    mu = jnp.mean(xf, axis=-1, keepdims=True)
    var = jnp.mean(jnp.square(xf - mu), axis=-1, keepdims=True)
    y = (xf - mu) * lax.rsqrt(var + LN_EPS) * g.astype(jnp.float32) + b.astype(jnp.float32)
    return y.astype(x.dtype)


def _project(h, w_in, w_gate_up, b_gate, idx_k_g, idx_k_b):
    B, L, _ = h.shape
    z = h @ w_in
    points = [int(p) for p in np.cumsum(SPLIT_SIZES)[:-1]]
    (q_g, k_g, v_g, r_g, a_low, q_d, k_d, v_d, q_i, k_i, w_i, gate_g, gate_d) = jnp.split(z, points, axis=-1)
    log_a = jax.nn.log_sigmoid((a_low @ w_gate_up + b_gate).astype(jnp.float32)) / GLA_GATE_TAU
    heads = lambda a, n: a.reshape(B, L, n, -1)
    return (heads(q_g, GLA_HEADS), heads(k_g, GLA_HEADS), heads(v_g, GLA_HEADS), r_g,
            heads(log_a, GLA_HEADS),
            heads(q_d, DSA_HEADS), heads(k_d, DSA_HEADS), heads(v_d, DSA_HEADS),
            heads(q_i, IDX_HEADS), _layernorm(k_i, idx_k_g, idx_k_b), w_i, gate_g, gate_d)


def _gla_chunked(q, k, v, log_a, s0):
    B, L, H, _ = q.shape
    C = math.gcd(L, GLA_CHUNK)
    n = L // C

    def chunks(a):
        return jnp.moveaxis(a.astype(jnp.float32).reshape(B, n, C, *a.shape[2:]), 1, 0)

    causal = jnp.tril(jnp.ones((C, C), dtype=bool))[None, :, :, None, None]

    def step(S, inp):
        qc, kc, vc, gc = inp
        b = jnp.cumsum(gc, axis=1)
        diff = jnp.where(causal, b[:, :, None] - b[:, None, :], -jnp.inf)
        att = jnp.einsum('bthd,bshd,btshd->btsh', qc, kc, jnp.exp(diff))
        o = (jnp.einsum('btsh,bshv->bthv', att, vc)
             + jnp.einsum('bthd,bhdv->bthv', qc * jnp.exp(b), S))
        b_last = b[:, -1]
        S = (S * jnp.exp(b_last)[..., None]
             + jnp.einsum('bshd,bshv->bhdv', kc * jnp.exp(b_last[:, None] - b), vc))
        return S, o

    S, o = lax.scan(step, s0.astype(jnp.float32),
                    (chunks(q * (GLA_HEAD_DK ** -0.5)), chunks(k), chunks(v), chunks(log_a)))
    o = jnp.moveaxis(o, 0, 1).reshape(B, L, H, -1)
    return o, S


def _t5_bucket(rel):
    max_exact = REL_BUCKETS // 2
    n = jnp.maximum(rel, 0)
    large = max_exact + (jnp.log(jnp.maximum(n, 1).astype(jnp.float32) / max_exact)
                         / math.log(REL_MAX_DIST / max_exact)
                         * (REL_BUCKETS - max_exact)).astype(jnp.int32)
    large = jnp.minimum(large, REL_BUCKETS - 1)
    return jnp.where(n < max_exact, n, large)


def _indexer_topk(q_i, w_i, k_i, t_pos, topk):
    s = jnp.einsum('bqhd,bsd->bqhs', q_i.astype(jnp.float32), k_i.astype(jnp.float32)) * (IDX_DIM ** -0.5)
    score = jnp.einsum('bqhs,bqh->bqs', jax.nn.relu(s), w_i.astype(jnp.float32) * (IDX_HEADS ** -0.5))
    key_pos = jnp.arange(k_i.shape[1])
    score = jnp.where(key_pos[None, None, :] <= t_pos[None, :, None], score, -jnp.inf)
    _, idx = lax.top_k(score, topk)
    return idx


def _sparse_attend(q, k_sel, v_sel, idx, t_pos, rel_bias):
    rel = t_pos[None, :, None] - idx
    bias = jnp.swapaxes(rel_bias.astype(jnp.float32)[_t5_bucket(rel)], -1, -2)
    logits = jnp.einsum('bqhd,bqkhd->bqhk', q.astype(jnp.float32), k_sel.astype(jnp.float32)) * (DSA_HEAD_DIM ** -0.5) + bias
    logits = jnp.where(rel[:, :, None, :] >= 0, logits, -jnp.inf)
    p = jax.nn.softmax(logits, axis=-1)
    return jnp.einsum('bqhk,bqkhd->bqhd', p, v_sel.astype(jnp.float32)).astype(q.dtype)


def _gather_rows(rows, ix):
    return jax.vmap(lambda r, i: r[i])(rows, ix)


def _dsa_prompt(q, k, v, q_i, w_i, k_i, rel_bias):
    B, L = q.shape[:2]
    topk = min(TOPK_MAX, L // 4)
    nb = L // QUERY_BLOCK

    def blocks(a):
        return jnp.moveaxis(a.reshape(B, nb, QUERY_BLOCK, *a.shape[2:]), 1, 0)

    def one_block(inp):
        blk, qb, qib, wib = inp
        t_pos = blk * QUERY_BLOCK + jnp.arange(QUERY_BLOCK)
        idx = _indexer_topk(qib, wib, k_i, t_pos, topk)
        return _sparse_attend(qb, _gather_rows(k, idx), _gather_rows(v, idx), idx, t_pos, rel_bias)

    out = lax.map(one_block, (jnp.arange(nb), blocks(q), blocks(q_i), blocks(w_i)))
    return jnp.moveaxis(out, 0, 1).reshape(B, L, DSA_HEADS, DSA_HEAD_DIM)


def _dsa_sample(q, k_new, v_new, q_i, w_i, k_i_new, rel_bias, cache_k, cache_v, cache_kidx, page_table):
    DB, T = q.shape[:2]
    past = page_table.shape[1] * PAGE_SIZE
    topk = min(TOPK_MAX, (past + T) // 4)
    t_pos = past + jnp.arange(T)
    k_i_past = cache_kidx[page_table].reshape(DB, past, IDX_DIM)
    k_i_all = jnp.concatenate([k_i_past, k_i_new.astype(k_i_past.dtype)], axis=1)
    idx = _indexer_topk(q_i, w_i, k_i_all, t_pos, topk)
    in_past = (idx < past)[..., None, None]
    pidx = jnp.minimum(idx, past - 1)
    phys = _gather_rows(page_table, pidx // PAGE_SIZE)
    slot = pidx % PAGE_SIZE
    nidx = jnp.clip(idx - past, 0, T - 1)
    k_sel = jnp.where(in_past, cache_k[phys, slot], _gather_rows(k_new, nidx).astype(cache_k.dtype))
    v_sel = jnp.where(in_past, cache_v[phys, slot], _gather_rows(v_new, nidx).astype(cache_v.dtype))
    return _sparse_attend(q, k_sel, v_sel, idx, t_pos, rel_bias)


def _trunk_layer(x, gla_s0, dsa_fn, rel_bias, lw):
    (g_mix, w_in, w_gate_up, b_gate, gla_norm_g, w_gla_branch, idx_k_g, idx_k_b,
     w_dsa_branch, w_o, g_ffn, w_ffn_gate, w_ffn_up, w_ffn_down) = lw
    B, L, _ = x.shape
    h = _rmsnorm(x, g_mix)
    (q_g, k_g, v_g, r_g, log_a, q_d, k_d, v_d, q_i, k_i, w_i, gate_g, gate_d) = _project(
        h, w_in, w_gate_up, b_gate, idx_k_g, idx_k_b)
    o_g, s_new = _gla_chunked(q_g, k_g, v_g, log_a, gla_s0)
    o_g = _rmsnorm(o_g.astype(x.dtype), gla_norm_g).reshape(B, L, GLA_DV) * jax.nn.silu(r_g)
    y_g = o_g @ w_gla_branch
    o_d = dsa_fn(q_d, k_d, v_d, q_i, w_i, k_i, rel_bias)
    y_d = o_d.reshape(B, L, DSA_WIDTH) @ w_dsa_branch
    x = x + (jax.nn.sigmoid(gate_g) * y_g + jax.nn.sigmoid(gate_d) * y_d) @ w_o
    hf = _rmsnorm(x, g_ffn)
    x = x + (jax.nn.silu(hf @ w_ffn_gate) * (hf @ w_ffn_up)) @ w_ffn_down
    return x, (k_d, v_d, k_i, s_new.astype(gla_s0.dtype))


def setup_inputs(seed: int = 0) -> dict:
    key = jax.random.key(seed)
    ks = jax.random.split(key, 32)
    n_pages = PAST_LEN // PAGE_SIZE
    n_used = DEC_BATCH * n_pages
    n_pool = n_used + n_used // 4

    def nrm(k, shape, scale):
        return jax.random.normal(k, shape, jnp.float32) * scale

    def gain(k, shape):
        return 1.0 + nrm(k, shape, 0.05)

    page_table = jax.random.permutation(ks[7], n_pool)[:n_used].reshape(DEC_BATCH, n_pages).astype(jnp.int32)
    return {
        "x_prompt": nrm(ks[0], (BATCH, SEQ, D_MODEL), 1.0),
        "x_sample": nrm(ks[1], (DEC_BATCH, DEC_SEQ, D_MODEL), 1.0),
        "cache_k": nrm(ks[2], (DEPTH, n_pool, PAGE_SIZE, DSA_HEADS, DSA_HEAD_DIM), 1.0),
        "cache_v": nrm(ks[3], (DEPTH, n_pool, PAGE_SIZE, DSA_HEADS, DSA_HEAD_DIM), 1.0),
        "cache_kidx": nrm(ks[4], (DEPTH, n_pool, PAGE_SIZE, IDX_DIM), 1.0),
        "state_gla": nrm(ks[5], (DEPTH, DEC_BATCH, GLA_HEADS, GLA_HEAD_DK, GLA_HEAD_DV), 0.3),
        "page_table": page_table,
        "g_mix": gain(ks[8], (DEPTH, D_MODEL)),
        "w_in": nrm(ks[9], (DEPTH, D_MODEL, D_IN_PROJ), D_MODEL ** -0.5),
        "w_gate_up": nrm(ks[10], (DEPTH, GLA_GATE_RANK, GLA_DK), GLA_GATE_RANK ** -0.5),
        "b_gate": nrm(ks[11], (DEPTH, GLA_DK), 0.1),
        "gla_norm_g": gain(ks[12], (DEPTH, GLA_HEAD_DV)),
        "w_gla_branch": nrm(ks[13], (DEPTH, GLA_DV, D_MODEL), GLA_DV ** -0.5),
        "idx_k_g": gain(ks[14], (DEPTH, IDX_DIM)),
        "idx_k_b": nrm(ks[15], (DEPTH, IDX_DIM), 0.02),
        "w_dsa_branch": nrm(ks[16], (DEPTH, DSA_WIDTH, D_MODEL), DSA_WIDTH ** -0.5),
        "w_o": nrm(ks[17], (DEPTH, D_MODEL, D_MODEL), D_MODEL ** -0.5),
        "g_ffn": gain(ks[18], (DEPTH, D_MODEL)),
        "w_ffn_gate": nrm(ks[19], (DEPTH, D_MODEL, D_FF), D_MODEL ** -0.5),
        "w_ffn_up": nrm(ks[20], (DEPTH, D_MODEL, D_FF), D_MODEL ** -0.5),
        "w_ffn_down": nrm(ks[21], (DEPTH, D_FF, D_MODEL), D_FF ** -0.5),
        "rel_bias": nrm(ks[22], (REL_BUCKETS, DSA_HEADS), 0.5),
        "g_final": gain(ks[23], (D_MODEL,)),
    }


def reference(x_prompt, x_sample, cache_k, cache_v, cache_kidx, state_gla, page_table,
              g_mix, w_in, w_gate_up, b_gate, gla_norm_g, w_gla_branch, idx_k_g, idx_k_b,
              w_dsa_branch, w_o, g_ffn, w_ffn_gate, w_ffn_up, w_ffn_down, rel_bias, g_final):
    xp, xs = x_prompt, x_sample
    new_p = ([], [], [], [])
    new_s = ([], [], [], [])
    for layer in range(DEPTH):
        lw = (g_mix[layer], w_in[layer], w_gate_up[layer], b_gate[layer], gla_norm_g[layer],
              w_gla_branch[layer], idx_k_g[layer], idx_k_b[layer], w_dsa_branch[layer], w_o[layer],
              g_ffn[layer], w_ffn_gate[layer], w_ffn_up[layer], w_ffn_down[layer])
        s0 = jnp.zeros((xp.shape[0], GLA_HEADS, GLA_HEAD_DK, GLA_HEAD_DV), state_gla.dtype)
        xp, st_p = _trunk_layer(xp, s0, _dsa_prompt, rel_bias, lw)
        dsa_s = functools.partial(_dsa_sample, cache_k=cache_k[layer], cache_v=cache_v[layer],
                                  cache_kidx=cache_kidx[layer], page_table=page_table)
        xs, st_s = _trunk_layer(xs, state_gla[layer], dsa_s, rel_bias, lw)
        for lst, a in zip(new_p, st_p):
            lst.append(a)
        for lst, a in zip(new_s, st_s):
            lst.append(a)
    y_prompt = _rmsnorm(xp, g_final)
    y_sample = _rmsnorm(xs, g_final)
    k_p, v_p, ki_p, s_p = [jnp.stack(a) for a in new_p]
    k_s, v_s, ki_s, s_s = [jnp.stack(a) for a in new_s]
    return (y_prompt, y_sample, k_p, v_p, ki_p, s_p, k_s, v_s, ki_s, s_s)
```

```python
import functools
import math

import numpy as np
import jax
import jax.numpy as jnp
from jax import lax
from jax.experimental import pallas as pl
from jax.experimental.pallas import tpu as pltpu

F32, BF16, I32 = jnp.float32, jnp.bfloat16, jnp.int32

D_MODEL = 1024
GLA_HEADS = 4
GLA_HEAD_DK = 128
GLA_HEAD_DV = 256
GLA_DK = GLA_HEADS * GLA_HEAD_DK
GLA_DV = GLA_HEADS * GLA_HEAD_DV
GLA_GATE_RANK = 16
GLA_GATE_TAU = 16.0
DSA_HEADS = 8
DSA_HEAD_DIM = 64
DSA_WIDTH = DSA_HEADS * DSA_HEAD_DIM
IDX_HEADS = 4
IDX_DIM = 64
TOPK_MAX = 256
QUERY_BLOCK = 128
PAGE_SIZE = 128
REL_BUCKETS = 32
REL_MAX_DIST = 128
RMS_EPS = 1e-6
LN_EPS = 1e-6
SPLIT_SIZES = (GLA_DK, GLA_DK, GLA_DV, GLA_DV, GLA_GATE_RANK, DSA_WIDTH, DSA_WIDTH, DSA_WIDTH,
               IDX_HEADS * IDX_DIM, IDX_DIM, IDX_HEADS, D_MODEL, D_MODEL)

LANES = 128
SUBLANES = 8
VMEM_LIMIT = 56 * 1024 * 1024

PROJ_TILE = 512
ZA_TILES = 11
ZA_WIDTH = ZA_TILES * PROJ_TILE
COL_QG, COL_KG = 0, 1
COL_VG, COL_RG, COL_GG, COL_GD = 1, 2, 3, 4
ZA_QD = 10 * PROJ_TILE
MISC_ALOW, MISC_WI, MISC_QI = 64, 80, 128

GLA_CHUNK = 64
GLA_TILE = 256
SAMPLE_CHUNK = 16
DSA_TK = 512
BIAS_TAB_ROWS = 1408
BIAS_TAB_OFF = 896
NEG = -1e30
BISECT_STEPS = 26


def _sigmoid(x):
    return 1.0 / (1.0 + jnp.exp(-x))


def _rms(x, g):
    return x * lax.rsqrt(jnp.mean(x * x, axis=-1, keepdims=True) + RMS_EPS) * g


def _t5_bucket_np(n):
    n = np.maximum(np.asarray(n, np.int64), 0)
    max_exact = REL_BUCKETS // 2
    large = max_exact + (np.log(np.maximum(n, 1).astype(np.float32) / np.float32(max_exact))
                         / np.float32(math.log(REL_MAX_DIST / max_exact))
                         * np.float32(REL_BUCKETS - max_exact)).astype(np.int32)
    large = np.minimum(large, REL_BUCKETS - 1)
    return np.where(n < max_exact, n, large).astype(np.int32)


def _inproj_body(x_ref, g_ref, w_ref, ikg_ref, ikb_ref, za_ref, kd_ref, vd_ref, misc_ref, kin_ref, h_scr):
    j = pl.program_id(1)

    @pl.when(j == 0)
    def _():
        h_scr[...] = _rms(x_ref[...], g_ref[...]).astype(BF16)

    res = jnp.dot(h_scr[...], w_ref[...], preferred_element_type=F32)

    @pl.when(j < ZA_TILES)
    def _():
        za_ref[...] = res

    @pl.when(j == ZA_TILES)
    def _():
        kd_ref[...] = res

    @pl.when(j == ZA_TILES + 1)
    def _():
        vd_ref[...] = res

    @pl.when(j == ZA_TILES + 2)
    def _():
        misc_ref[...] = res
        ki = res[:, 0:IDX_DIM]
        mu = jnp.mean(ki, axis=-1, keepdims=True)
        var = jnp.mean(jnp.square(ki - mu), axis=-1, keepdims=True)
        kin_ref[...] = (ki - mu) * lax.rsqrt(var + LN_EPS) * ikg_ref[...] + ikb_ref[...]


def _pack_w_in(w_in):
    pts = np.cumsum((0,) + SPLIT_SIZES)
    seg = [w_in[:, int(pts[i]):int(pts[i + 1])] for i in range(len(SPLIT_SIZES))]
    q_g, k_g, v_g, r_g, a_low, q_d, k_d, v_d, q_i, k_i, w_i, gate_g, gate_d = seg
    z = lambda n: jnp.zeros((w_in.shape[0], n), w_in.dtype)
    misc = jnp.concatenate([k_i, a_low, w_i, z(LANES - MISC_WI - IDX_HEADS), q_i,
                            z(PROJ_TILE - MISC_QI - IDX_HEADS * IDX_DIM)], axis=1)
    return jnp.concatenate([q_g, k_g, v_g, r_g, gate_g, gate_d, q_d, k_d, v_d, misc], axis=1).astype(BF16)


def _in_proj(x2d, g_mix, w_cat, idx_k_g, idx_k_b):
    n = x2d.shape[0]
    tm = min(n, 1024)
    nj = w_cat.shape[1] // PROJ_TILE
    row = lambda i, j: (i, 0)
    return pl.pallas_call(
        _inproj_body,
        grid=(n // tm, nj),
        in_specs=[pl.BlockSpec((tm, D_MODEL), row),
                  pl.BlockSpec((1, D_MODEL), lambda i, j: (0, 0)),
                  pl.BlockSpec((D_MODEL, PROJ_TILE), lambda i, j: (0, j)),
                  pl.BlockSpec((1, IDX_DIM), lambda i, j: (0, 0)),
                  pl.BlockSpec((1, IDX_DIM), lambda i, j: (0, 0))],
        out_specs=[pl.BlockSpec((tm, PROJ_TILE), lambda i, j: (i, jnp.minimum(j, ZA_TILES - 1))),
                   pl.BlockSpec((tm, PROJ_TILE), row),
                   pl.BlockSpec((tm, PROJ_TILE), row),
                   pl.BlockSpec((tm, PROJ_TILE), row),
                   pl.BlockSpec((tm, IDX_DIM), row)],
        out_shape=[jax.ShapeDtypeStruct((n, ZA_WIDTH), F32),
                   jax.ShapeDtypeStruct((n, PROJ_TILE), F32),
                   jax.ShapeDtypeStruct((n, PROJ_TILE), F32),
                   jax.ShapeDtypeStruct((n, PROJ_TILE), F32),
                   jax.ShapeDtypeStruct((n, IDX_DIM), F32)],
        scratch_shapes=[pltpu.VMEM((tm, D_MODEL), BF16)],
        compiler_params=pltpu.CompilerParams(dimension_semantics=("arbitrary", "arbitrary"),
                                             vmem_limit_bytes=VMEM_LIMIT),
        name="in_proj",
    )(x2d, g_mix.reshape(1, -1), w_cat, idx_k_g.reshape(1, -1), idx_k_b.reshape(1, -1))


def _gla_consts(c):
    nlev = int(math.log2(c))
    t = np.arange(c)[:, None]
    s = np.arange(c)[None, :]
    mats = [(s <= t), np.ones((c, c), bool)]
    masks = []
    for l in range(nlev):
        mid = ((t >> (l + 1)) << (l + 1)) + (1 << l) - 1
        mats.append(s <= mid)
        masks.append(((t >> (l + 1)) == (s >> (l + 1))) & (((t >> l) & 1) == 1) & (((s >> l) & 1) == 0))
    masks.append(t == s)
    return (jnp.asarray(np.concatenate(mats, 0).astype(np.float32), BF16),
            jnp.asarray(np.stack(masks).astype(np.float32), F32), nlev)


_NT = (((1,), (1,)), ((), ()))
_TN = (((0,), (0,)), ((), ()))


def _gla_body(q_ref, k_ref, v_ref, misc_ref, wgu_ref, bg_ref, gn_ref, mst_ref, lmask_ref, s0_ref,
              o_ref, sout_ref, s_scr, la_scr, *, c, nc, nlev, valid_rows):
    step = pl.program_id(1)

    @pl.when(step == 0)
    def _():
        for h in range(GLA_HEADS):
            s_scr[h] = s0_ref[h].T

    x = jnp.dot(misc_ref[...].astype(BF16), wgu_ref[...], preferred_element_type=F32) + bg_ref[...]
    log_a = (jnp.minimum(x, 0.0) - jnp.log1p(jnp.exp(-jnp.abs(x)))) * (1.0 / GLA_GATE_TAU)
    if valid_rows < c * nc:
        log_a = jnp.where(lax.broadcasted_iota(I32, log_a.shape, 0) < valid_rows, log_a, 0.0)
    la_scr[...] = log_a
    scale = GLA_HEAD_DK ** -0.5

    def chunk(ci, carry):
        r0 = pl.multiple_of(ci * c, c)
        rows = pl.ds(r0, c)
        g_all = la_scr[rows, :]
        g_hi = g_all.astype(BF16)
        g_lo = (g_all - g_hi.astype(F32)).astype(BF16)
        cs = jnp.dot(mst_ref[...], jnp.concatenate([g_hi, g_lo], axis=1), preferred_element_type=F32)
        cs = cs[:, :GLA_DK] + cs[:, GLA_DK:]
        for h in range(GLA_HEADS):
            ksl = slice(h * GLA_HEAD_DK, (h + 1) * GLA_HEAD_DK)
            vsl = slice(h * GLA_HEAD_DV, (h + 1) * GLA_HEAD_DV)
            q = q_ref[rows, ksl] * scale
            k = k_ref[rows, ksl]
            v = v_ref[rows, vsl].astype(BF16)
            b = cs[0:c, ksl]
            e_last = cs[c:2 * c, ksl]
            st = s_scr[h]
            o = lax.dot_general((q * jnp.exp(b)).astype(BF16), st.astype(BF16), _NT,
                                preferred_element_type=F32)
            att = lmask_ref[nlev] * lax.dot_general(q.astype(BF16), k.astype(BF16), _NT,
                                                    preferred_element_type=F32)
            for l in range(nlev):
                e = cs[(2 + l) * c:(3 + l) * c, ksl]
                ql = (q * jnp.exp(jnp.minimum(b - e, 0.0))).astype(BF16)
                kl = (k * jnp.exp(jnp.minimum(e - b, 0.0))).astype(BF16)
                att = att + lmask_ref[l] * lax.dot_general(ql, kl, _NT, preferred_element_type=F32)
            o = o + jnp.dot(att.astype(BF16), v, preferred_element_type=F32)
            k_st = (k * jnp.exp(e_last - b)).astype(BF16)
            s_scr[h] = st * jnp.exp(e_last[0:1, :]) + lax.dot_general(v, k_st, _TN,
                                                                      preferred_element_type=F32)
            o_ref[rows, vsl] = _rms(o, gn_ref[...])
        return carry

    lax.fori_loop(0, nc, chunk, 0)

    @pl.when(step == pl.num_programs(1) - 1)
    def _():
        for h in range(GLA_HEADS):
            sout_ref[h] = s_scr[h].T


def _gla(q_arr, k_arr, v_arr, misc_arr, cols, wgu_pad, b_gate, gla_norm_g, s0, c, tile, valid_rows):
    nb, l = q_arr.shape[0], q_arr.shape[1]
    mst, lmask, nlev = _gla_consts(c)
    const2 = lambda b, s: (0, 0)
    body = functools.partial(_gla_body, c=c, nc=tile // c, nlev=nlev, valid_rows=valid_rows)
    return pl.pallas_call(
        body,
        grid=(nb, l // tile),
        in_specs=[pl.BlockSpec((None, tile, GLA_DK), lambda b, s: (b, s, cols[0])),
                  pl.BlockSpec((None, tile, GLA_DK), lambda b, s: (b, s, cols[1])),
                  pl.BlockSpec((None, tile, GLA_DV), lambda b, s: (b, s, cols[2])),
                  pl.BlockSpec((None, tile, LANES), lambda b, s: (b, s, 0)),
                  pl.BlockSpec((LANES, GLA_DK), const2),
                  pl.BlockSpec((1, GLA_DK), const2),
                  pl.BlockSpec((1, GLA_HEAD_DV), const2),
                  pl.BlockSpec(mst.shape, const2),
                  pl.BlockSpec(lmask.shape, lambda b, s: (0, 0, 0)),
                  pl.BlockSpec((None, GLA_HEADS, GLA_HEAD_DK, GLA_HEAD_DV), lambda b, s: (b, 0, 0, 0))],
        out_specs=[pl.BlockSpec((None, tile, GLA_DV), lambda b, s: (b, s, 0)),
                   pl.BlockSpec((None, GLA_HEADS, GLA_HEAD_DK, GLA_HEAD_DV), lambda b, s: (b, 0, 0, 0))],
        out_shape=[jax.ShapeDtypeStruct((nb, l, GLA_DV), F32),
                   jax.ShapeDtypeStruct((nb, GLA_HEADS, GLA_HEAD_DK, GLA_HEAD_DV), F32)],
        scratch_shapes=[pltpu.VMEM((GLA_HEADS, GLA_HEAD_DV, GLA_HEAD_DK), F32),
                        pltpu.VMEM((tile, GLA_DK), F32)],
        compiler_params=pltpu.CompilerParams(dimension_semantics=("arbitrary", "arbitrary"),
                                             vmem_limit_bytes=VMEM_LIMIT),
        name="gla",
    )(q_arr, k_arr, v_arr, misc_arr, wgu_pad, b_gate.reshape(1, -1), gla_norm_g.reshape(1, -1),
      mst, lmask, s0)


def _select_threshold(sc_ref, nt, tr, topk, small, lane_ok, pos_bits):
    w = sc_ref.shape[1]
    inf = jnp.float32(jnp.inf)

    def over_tiles(fn, init):
        def body(i, carry):
            r0 = pl.multiple_of(i * tr, tr)
            return fn(carry, sc_ref[pl.ds(r0, tr), :], r0)
        return lax.fori_loop(0, nt, body, init)

    fold = lambda x: x.reshape(tr // SUBLANES, SUBLANES, w)
    zeros8 = jnp.zeros((SUBLANES, w), I32)
    pinf8 = jnp.full((SUBLANES, w), inf, F32)

    def count(pred):
        acc = over_tiles(lambda a, blk, r0: a + jnp.sum(fold(pred(blk, r0).astype(I32)), axis=0), zeros8)
        return jnp.sum(acc, axis=0, keepdims=True)

    def min_where(pred):
        acc = over_tiles(lambda a, blk, r0: jnp.minimum(
            a, jnp.min(fold(jnp.where(pred(blk, r0), blk, inf)), axis=0)), pinf8)
        return jnp.min(acc, axis=0, keepdims=True)

    def minmax(carry, blk, r0):
        mx, mn = carry
        b3 = fold(blk)
        return (jnp.maximum(mx, jnp.max(b3, axis=0)),
                jnp.minimum(mn, jnp.min(jnp.where(b3 == -inf, inf, b3), axis=0)))

    mx8, mn8 = over_tiles(minmax, (-pinf8, pinf8))
    hi = jnp.max(mx8, axis=0, keepdims=True)
    lo = jnp.min(mn8, axis=0, keepdims=True)

    def bisect(_, carry):
        lo, hi = carry
        mid = 0.5 * lo + 0.5 * hi
        ge = count(lambda blk, r0: blk >= mid) >= topk
        return jnp.where(ge, mid, lo), jnp.where(ge, hi, mid)

    lo, hi = lax.fori_loop(0, BISECT_STEPS, bisect, (lo, hi))
    v0 = min_where(lambda blk, r0: blk >= lo)

    def gt_next(v):
        def f(carry, blk, r0):
            cg, nx = carry
            gt = blk > v
            return (cg + jnp.sum(fold(gt.astype(I32)), axis=0),
                    jnp.minimum(nx, jnp.min(fold(jnp.where(gt, blk, inf)), axis=0)))
        cg8, nx8 = over_tiles(f, (zeros8, pinf8))
        return jnp.sum(cg8, axis=0, keepdims=True), jnp.min(nx8, axis=0, keepdims=True)

    live = lane_ok & jnp.logical_not(small)

    def peel(state):
        v, _, _ = state
        cg, nx = gt_next(v)
        move = (cg >= topk) & live
        return jnp.where(move, nx, v), cg, jnp.max(move.astype(I32))

    v, cnt_gt, _ = lax.while_loop(lambda s: s[2] > 0, peel,
                                  (v0, jnp.zeros((1, w), I32), jnp.int32(1)))
    cnt_ge = count(lambda blk, r0: blk >= v)
    excess = (cnt_ge > topk) & live

    @pl.when(jnp.max(excess.astype(I32)) > 0)
    def _():
        need = topk - cnt_gt

        def tie_count(m):
            def pred(blk, r0):
                pos = r0 + lax.broadcasted_iota(I32, (tr, w), 0)
                return (blk == v) & (pos <= m)
            return count(pred)

        def bit_step(it, p):
            cand = p + jnp.left_shift(jnp.int32(1), pos_bits - 1 - it)
            return jnp.where(tie_count(cand) <= need, cand, p)

        p = lax.fori_loop(0, pos_bits, bit_step, jnp.zeros((1, w), I32))
        bound = jnp.where(excess, p, jnp.int32(2 ** 30))

        def rewrite(i, carry):
            r0 = pl.multiple_of(i * tr, tr)
            blk = sc_ref[pl.ds(r0, tr), :]
            pos = r0 + lax.broadcasted_iota(I32, (tr, w), 0)
            sc_ref[pl.ds(r0, tr), :] = jnp.where((blk == v) & (pos > bound), -inf, blk)
            return carry

        lax.fori_loop(0, nt, rewrite, 0)

    return jnp.where(small, -inf, v)


def _dsa_prompt_body(relb_ref, btab_ref, ki_ref, qi_ref, wi_ref, kd_ref, vt_ref, qd_ref, o_ref,
                     sc_scr, tbl_scr, q2_scr, acc_scr, m_scr, l_scr, *, topk, pos_bits, far_bucket):
    b = pl.program_id(0)
    j = pl.program_id(1)
    tk = DSA_TK
    hd = DSA_HEAD_DIM

    @pl.when((b == 0) & (j == 0))
    def _():
        q2_scr[...] = jnp.zeros(q2_scr.shape, BF16)

        def build(ci, carry):
            r0 = pl.multiple_of(ci * LANES, LANES)
            bt = btab_ref[pl.ds(r0, LANES), :]
            for h in range(DSA_HEADS):
                t = jnp.zeros(bt.shape, F32)
                for bk in range(REL_BUCKETS):
                    t = jnp.where(bt == bk, relb_ref[bk, h], t)
                tbl_scr[h, pl.ds(r0, LANES), :] = t
            return carry

        lax.fori_loop(0, BIAS_TAB_ROWS // LANES, build, 0)

    nt = j // (tk // QUERY_BLOCK) + 1
    qpos = j * QUERY_BLOCK + lax.broadcasted_iota(I32, (1, LANES), 1)
    for p in range(DSA_HEADS // 2):
        q2_scr[p, 0:hd, 0:LANES] = qd_ref[2 * p * hd:(2 * p + 1) * hd, :]
        q2_scr[p, hd:2 * hd, LANES:2 * LANES] = qd_ref[(2 * p + 1) * hd:(2 * p + 2) * hd, :]

    wi = wi_ref[...]

    def score_tile(i, carry):
        r0 = pl.multiple_of(i * tk, tk)
        s4 = jnp.dot(ki_ref[pl.ds(r0, tk), :], qi_ref[...], preferred_element_type=F32)
        sc = jnp.zeros((tk, LANES), F32)
        for h in range(IDX_HEADS):
            sc = sc + jnp.maximum(s4[:, h * LANES:(h + 1) * LANES], 0.0) * wi[h:h + 1, :]
        kpos = r0 + lax.broadcasted_iota(I32, (tk, LANES), 0)
        sc_scr[pl.ds(r0, tk), :] = jnp.where(kpos <= qpos, sc, -jnp.inf)
        return carry

    lax.fori_loop(0, nt, score_tile, 0)

    small = (qpos + 1) < topk
    kstar = _select_threshold(sc_scr, nt, tk, topk, small, jnp.full((1, LANES), True), pos_bits)

    m_scr[...] = jnp.full(m_scr.shape, NEG, F32)
    l_scr[...] = jnp.zeros(l_scr.shape, F32)
    acc_scr[...] = jnp.zeros(acc_scr.shape, F32)

    def attend(i, near):
        r0 = pl.multiple_of(i * tk, tk)
        blk = sc_scr[pl.ds(r0, tk), :]
        if near:
            kpos = r0 + lax.broadcasted_iota(I32, (tk, LANES), 0)
            addm = jnp.where(blk >= kstar, jnp.where(kpos <= qpos, 0.0, NEG), NEG)
            off = pl.multiple_of(i * tk - j * QUERY_BLOCK + BIAS_TAB_OFF, LANES)
        else:
            addm = jnp.where(blk >= kstar, 0.0, NEG)
        for p in range(DSA_HEADS // 2):
            lg2 = jnp.dot(kd_ref[pl.ds(r0, tk), p * LANES:(p + 1) * LANES], q2_scr[p],
                          preferred_element_type=F32)
            for hh in range(2):
                h = 2 * p + hh
                lg = lg2[:, hh * LANES:(hh + 1) * LANES]
                if near:
                    lg = lg + tbl_scr[h, pl.ds(off, tk), :]
                else:
                    lg = lg + relb_ref[far_bucket, h]
                lg = lg + addm
                m_old = m_scr[h:h + 1, :]
                m_new = jnp.maximum(m_old, jnp.max(lg, axis=0, keepdims=True))
                pr = jnp.exp(lg - m_new)
                alpha = jnp.exp(m_old - m_new)
                l_scr[h:h + 1, :] = alpha * l_scr[h:h + 1, :] + jnp.sum(pr, axis=0, keepdims=True)
                m_scr[h:h + 1, :] = m_new
                pv = jnp.dot(vt_ref[h * hd:(h + 1) * hd, pl.ds(r0, tk)], pr.astype(BF16),
                             preferred_element_type=F32)
                acc_scr[h * hd:(h + 1) * hd, :] = alpha * acc_scr[h * hd:(h + 1) * hd, :] + pv

    n_far = jnp.maximum(nt - 2, 0)

    def far_step(i, carry):
        attend(i, False)
        return carry

    def near_step(i, carry):
        attend(i, True)
        return carry

    lax.fori_loop(0, n_far, far_step, 0)
    lax.fori_loop(n_far, nt, near_step, 0)

    for h in range(DSA_HEADS):
        acc_scr[h * hd:(h + 1) * hd, :] = acc_scr[h * hd:(h + 1) * hd, :] / l_scr[h:h + 1, :]
    o_ref[...] = acc_scr[...].T.astype(BF16)


def _bias_bucket_table():
    u = np.arange(BIAS_TAB_ROWS)[:, None]
    r = np.arange(LANES)[None, :]
    return jnp.asarray(_t5_bucket_np(r + BIAS_TAB_OFF - u), I32)


def _dsa_prompt(zA, kd, vd, misc, kin, rel_bias, nb, l):
    nq = l // QUERY_BLOCK
    topk = min(TOPK_MAX, l // 4)
    assert l % DSA_TK == 0
    far = _t5_bucket_np(np.arange(REL_MAX_DIST, max(l, REL_MAX_DIST + 1)))
    assert (far == far[0]).all()
    ki = kin.reshape(nb, l, IDX_DIM).astype(BF16)
    qi = misc[:, MISC_QI:MISC_QI + IDX_HEADS * IDX_DIM].reshape(nb, nq, QUERY_BLOCK, IDX_HEADS, IDX_DIM)
    qi = qi.transpose(0, 4, 1, 3, 2).reshape(nb, IDX_DIM, nq * IDX_HEADS * QUERY_BLOCK).astype(BF16)
    wi = misc[:, MISC_WI:MISC_WI + IDX_HEADS] * ((IDX_DIM ** -0.5) * (IDX_HEADS ** -0.5))
    wi = wi.reshape(nb, nq, QUERY_BLOCK, IDX_HEADS).transpose(0, 1, 3, 2)
    wi = jnp.pad(wi, ((0, 0), (0, 0), (0, SUBLANES - IDX_HEADS), (0, 0))).reshape(nb, nq * SUBLANES, QUERY_BLOCK)
    kd_bf = kd.reshape(nb, l, DSA_WIDTH).astype(BF16)
    vt = vd.reshape(nb, l, DSA_WIDTH).transpose(0, 2, 1).astype(BF16)
    qd = (zA[:, ZA_QD:ZA_QD + DSA_WIDTH] * (DSA_HEAD_DIM ** -0.5)).reshape(nb, l, DSA_WIDTH)
    qd = qd.transpose(0, 2, 1).astype(BF16)
    body = functools.partial(_dsa_prompt_body, topk=topk, pos_bits=max(1, int(math.ceil(math.log2(l)))),
                             far_bucket=int(far[0]))
    whole = lambda b, j: (b, 0, 0)
    return pl.pallas_call(
        body,
        grid=(nb, nq),
        in_specs=[pl.BlockSpec(memory_space=pltpu.SMEM),
                  pl.BlockSpec((BIAS_TAB_ROWS, LANES), lambda b, j: (0, 0), pipeline_mode=pl.Buffered(1)),
                  pl.BlockSpec((None, l, IDX_DIM), whole, pipeline_mode=pl.Buffered(1)),
                  pl.BlockSpec((None, IDX_DIM, IDX_HEADS * QUERY_BLOCK), lambda b, j: (b, 0, j)),
                  pl.BlockSpec((None, SUBLANES, QUERY_BLOCK), lambda b, j: (b, j, 0)),
                  pl.BlockSpec((None, l, DSA_WIDTH), whole, pipeline_mode=pl.Buffered(1)),
                  pl.BlockSpec((None, DSA_WIDTH, l), whole, pipeline_mode=pl.Buffered(1)),
                  pl.BlockSpec((None, DSA_WIDTH, QUERY_BLOCK), lambda b, j: (b, 0, j))],
        out_specs=pl.BlockSpec((None, QUERY_BLOCK, DSA_WIDTH), lambda b, j: (b, j, 0)),
        out_shape=jax.ShapeDtypeStruct((nb, l, DSA_WIDTH), BF16),
        scratch_shapes=[pltpu.VMEM((l, LANES), F32),
                        pltpu.VMEM((DSA_HEADS, BIAS_TAB_ROWS, LANES), F32),
                        pltpu.VMEM((DSA_HEADS // 2, LANES, 2 * LANES), BF16),
                        pltpu.VMEM((DSA_WIDTH, LANES), F32),
                        pltpu.VMEM((DSA_HEADS, LANES), F32),
                        pltpu.VMEM((DSA_HEADS, LANES), F32)],
        compiler_params=pltpu.CompilerParams(dimension_semantics=("arbitrary", "arbitrary"),
                                             vmem_limit_bytes=VMEM_LIMIT),
        name="dsa_prompt",
    )(rel_bias, _bias_bucket_table(), ki, qi, wi, kd_bf, vt, qd)


SEQ_GROUP = LANES // SUBLANES


def _dsa_sample_select_body(pt_ref, *refs, n_pages, t_new, topk, pos_bits):
    page_refs = refs[:n_pages]
    knew_ref, qi_ref, wi_ref, mask_ref, sc_scr = refs[n_pages:]
    g = pl.program_id(1)
    past = n_pages * PAGE_SIZE
    lane = lax.broadcasted_iota(I32, (1, LANES), 1)
    in_group = (lane // SUBLANES) == g

    def scores(k_rows):
        kb = k_rows.astype(BF16)
        sc = jnp.zeros((k_rows.shape[0], LANES), F32)
        for h in range(IDX_HEADS):
            rhs = jnp.where(in_group, qi_ref[h], jnp.zeros((), BF16))
            s = jnp.dot(kb, rhs, preferred_element_type=F32)
            sc = sc + jnp.maximum(s, 0.0) * wi_ref[h:h + 1, :]
        return jnp.where(in_group, sc, 0.0)

    for p in range(n_pages + 1):
        rows = slice(p * PAGE_SIZE, (p + 1) * PAGE_SIZE)
        sc = scores(page_refs[p][...] if p < n_pages else knew_ref[...])

        @pl.when(g == 0)
        def _():
            sc_scr[rows, :] = sc

        @pl.when(g > 0)
        def _():
            sc_scr[rows, :] = sc_scr[rows, :] + sc

    @pl.when(g == SEQ_GROUP - 1)
    def _():
        q_of_lane = lane % SUBLANES
        rows = slice(past, past + PAGE_SIZE)
        cpos = lax.broadcasted_iota(I32, (PAGE_SIZE, LANES), 0)
        sc_scr[rows, :] = jnp.where(cpos <= q_of_lane, sc_scr[rows, :], -jnp.inf)
        lane_ok = q_of_lane < t_new
        small = (past + q_of_lane + 1) < topk
        kstar = _select_threshold(sc_scr, n_pages + 1, PAGE_SIZE, topk, small, lane_ok, pos_bits)
        for p in range(n_pages + 1):
            rows = slice(p * PAGE_SIZE, (p + 1) * PAGE_SIZE)
            mask_ref[rows, :] = jnp.where(sc_scr[rows, :] >= kstar, 1.0, 0.0).astype(BF16)


def _dsa_sample_select(page_table, cache_kidx, kin_new, qi_t, wi_t, t_new, topk):
    db, n_pages = page_table.shape
    ng = db // SEQ_GROUP
    rows = (n_pages + 1) * PAGE_SIZE
    seq = lambda gi, g, pt: (gi * SEQ_GROUP + g, 0, 0)
    page_specs = [pl.BlockSpec((None, PAGE_SIZE, IDX_DIM),
                               functools.partial(lambda gi, g, pt, p: (pt[gi * SEQ_GROUP + g, p], 0, 0), p=p))
                  for p in range(n_pages)]
    body = functools.partial(_dsa_sample_select_body, n_pages=n_pages, t_new=t_new, topk=topk,
                             pos_bits=int(math.ceil(math.log2(rows))))
    grid_spec = pltpu.PrefetchScalarGridSpec(
        num_scalar_prefetch=1,
        grid=(ng, SEQ_GROUP),
        in_specs=page_specs + [pl.BlockSpec((None, PAGE_SIZE, IDX_DIM), seq),
                               pl.BlockSpec((None, IDX_HEADS, IDX_DIM, LANES), lambda gi, g, pt: (gi * SEQ_GROUP + g, 0, 0, 0)),
                               pl.BlockSpec((None, SUBLANES, LANES), seq)],
        out_specs=pl.BlockSpec((None, rows, LANES), lambda gi, g, pt: (gi, 0, 0)),
        scratch_shapes=[pltpu.VMEM((rows, LANES), F32)],
    )
    return pl.pallas_call(
        body, grid_spec=grid_spec,
        out_shape=jax.ShapeDtypeStruct((ng, rows, LANES), BF16),
        compiler_params=pltpu.CompilerParams(dimension_semantics=("arbitrary", "arbitrary"),
                                             vmem_limit_bytes=VMEM_LIMIT),
        name="dsa_sample_select",
    )(page_table, *([cache_kidx] * n_pages), kin_new, qi_t, wi_t)


def _dsa_sample_attend_body(pt_ref, *refs, n_pages):
    k_refs = refs[:n_pages]
    v_refs = refs[n_pages:2 * n_pages]
    (knew_ref, vnew_ref, mask_ref, qbig_ref, btab_ref, rb_ref, o_ref, lg_scr, tbl_scr) = refs[2 * n_pages:]
    b = pl.program_id(0)
    hq = DSA_HEADS * SUBLANES

    @pl.when(b == 0)
    def _():
        bt = btab_ref[...]
        t = jnp.zeros(bt.shape, F32)
        for bk in range(REL_BUCKETS):
            t = jnp.where(bt == bk, rb_ref[bk:bk + 1, :], t)
        tbl_scr[...] = t

    g = b % SEQ_GROUP
    r_i = lax.broadcasted_iota(I32, (LANES, hq), 0)
    c_i = lax.broadcasted_iota(I32, (LANES, hq), 1)
    place = jnp.where(((r_i // SUBLANES) == g) & ((r_i % SUBLANES) == (c_i % SUBLANES)), 1.0, 0.0).astype(BF16)
    far_bias = rb_ref[REL_BUCKETS - 1:REL_BUCKETS, :]

    m = jnp.full((1, hq), NEG, F32)
    for p in range(n_pages + 1):
        rows = slice(p * PAGE_SIZE, (p + 1) * PAGE_SIZE)
        k_rows = k_refs[p][...] if p < n_pages else knew_ref[...]
        lg = jnp.dot(k_rows.astype(BF16), qbig_ref[...], preferred_element_type=F32)
        if p >= n_pages - 1:
            lg = lg + tbl_scr[(p - n_pages + 1) * PAGE_SIZE:(p - n_pages + 2) * PAGE_SIZE, :]
        else:
            lg = lg + far_bias
        sel = jnp.dot(mask_ref[rows, :], place, preferred_element_type=F32)
        lg = lg + jnp.where(sel > 0.5, 0.0, NEG)
        lg_scr[rows, :] = lg
        m = jnp.maximum(m, jnp.max(lg, axis=0, keepdims=True))

    acc = jnp.zeros((hq, DSA_WIDTH), F32)
    lsum = jnp.zeros((hq, LANES), F32)
    ones = jnp.ones((PAGE_SIZE, LANES), BF16)
    for p in range(n_pages + 1):
        rows = slice(p * PAGE_SIZE, (p + 1) * PAGE_SIZE)
        v_rows = v_refs[p][...] if p < n_pages else vnew_ref[...]
        pr = jnp.exp(lg_scr[rows, :] - m).astype(BF16)
        acc = acc + lax.dot_general(pr, v_rows.astype(BF16), _TN, preferred_element_type=F32)
        lsum = lsum + lax.dot_general(pr, ones, _TN, preferred_element_type=F32)
    acc = acc / lsum[:, 0:1]
    head_of_lane = lax.broadcasted_iota(I32, (SUBLANES, DSA_WIDTH), 1) // DSA_HEAD_DIM
    out = jnp.zeros((SUBLANES, DSA_WIDTH), F32)
    for h in range(DSA_HEADS):
        out = jnp.where(head_of_lane == h, acc[h * SUBLANES:(h + 1) * SUBLANES, :], out)
    o_ref[...] = out


def _dsa_sample_attend(page_table, cache_k, cache_v, k_new, v_new, mask, qbig, rel_bias, t_new):
    db, n_pages = page_table.shape
    hq = DSA_HEADS * SUBLANES
    rows = (n_pages + 1) * PAGE_SIZE
    seq = lambda b, pt: (b, 0, 0)
    page = lambda p: functools.partial(lambda b, pt, p: (pt[b, p], 0, 0), p=p)
    kv_spec = lambda p: pl.BlockSpec((None, PAGE_SIZE, DSA_WIDTH), page(p))
    u = np.arange(2 * PAGE_SIZE)[:, None]
    q = (np.arange(hq) % SUBLANES)[None, :]
    btab = jnp.asarray(_t5_bucket_np(PAGE_SIZE + q - u), I32)
    rb = jnp.repeat(rel_bias, SUBLANES, axis=1)
    grid_spec = pltpu.PrefetchScalarGridSpec(
        num_scalar_prefetch=1,
        grid=(db,),
        in_specs=[kv_spec(p) for p in range(n_pages)] + [kv_spec(p) for p in range(n_pages)] + [
            pl.BlockSpec((None, PAGE_SIZE, DSA_WIDTH), seq),
            pl.BlockSpec((None, PAGE_SIZE, DSA_WIDTH), seq),
            pl.BlockSpec((None, rows, LANES), lambda b, pt: (b // SEQ_GROUP, 0, 0)),
            pl.BlockSpec((None, DSA_WIDTH, hq), seq),
            pl.BlockSpec(btab.shape, lambda b, pt: (0, 0)),
            pl.BlockSpec(rb.shape, lambda b, pt: (0, 0))],
        out_specs=pl.BlockSpec((None, SUBLANES, DSA_WIDTH), seq),
        scratch_shapes=[pltpu.VMEM((rows, hq), F32), pltpu.VMEM((2 * PAGE_SIZE, hq), F32)],
    )
    return pl.pallas_call(
        functools.partial(_dsa_sample_attend_body, n_pages=n_pages), grid_spec=grid_spec,
        out_shape=jax.ShapeDtypeStruct((db, SUBLANES, DSA_WIDTH), F32),
        compiler_params=pltpu.CompilerParams(dimension_semantics=("arbitrary",),
                                             vmem_limit_bytes=VMEM_LIMIT),
        name="dsa_sample_attend",
    )(page_table, *([cache_k] * n_pages), *([cache_v] * n_pages), k_new, v_new, mask, qbig, btab, rb)


def _dsa_sample(zA, kd, vd, misc, kin, rel_bias, cache_k, cache_v, cache_kidx, page_table, db, t_new):
    n_pages = page_table.shape[1]
    past = n_pages * PAGE_SIZE
    topk = min(TOPK_MAX, (past + t_new) // 4)
    assert db % SEQ_GROUP == 0 and t_new <= SUBLANES
    pad_rows = lambda a: jnp.pad(a, ((0, 0), (0, PAGE_SIZE - t_new), (0, 0)))
    kin_new = pad_rows(kin.reshape(db, t_new, IDX_DIM))
    k_new = pad_rows(kd.reshape(db, t_new, DSA_WIDTH))
    v_new = pad_rows(vd.reshape(db, t_new, DSA_WIDTH))
    qi = misc[:, MISC_QI:MISC_QI + IDX_HEADS * IDX_DIM].reshape(db, t_new, IDX_HEADS, IDX_DIM)
    qi = jnp.pad(qi.transpose(0, 2, 3, 1), ((0, 0), (0, 0), (0, 0), (0, SUBLANES - t_new)))
    qi_t = jnp.tile(qi, (1, 1, 1, SEQ_GROUP)).astype(BF16)
    wi = misc[:, MISC_WI:MISC_WI + IDX_HEADS] * ((IDX_DIM ** -0.5) * (IDX_HEADS ** -0.5))
    wi = jnp.pad(wi.reshape(db, t_new, IDX_HEADS).transpose(0, 2, 1),
                 ((0, 0), (0, SUBLANES - IDX_HEADS), (0, SUBLANES - t_new)))
    wi_t = jnp.tile(wi, (1, 1, SEQ_GROUP))
    mask = _dsa_sample_select(page_table, cache_kidx, kin_new, qi_t, wi_t, t_new, topk)
    qd = (zA[:, ZA_QD:ZA_QD + DSA_WIDTH] * (DSA_HEAD_DIM ** -0.5)).reshape(db, t_new, DSA_HEADS, DSA_HEAD_DIM)
    qd = jnp.pad(qd.transpose(0, 2, 3, 1), ((0, 0), (0, 0), (0, 0), (0, SUBLANES - t_new)))
    eye = jnp.eye(DSA_HEADS, dtype=qd.dtype)
    qbig = (qd[:, :, :, None, :] * eye[None, :, None, :, None]).reshape(db, DSA_WIDTH, DSA_HEADS * SUBLANES)
    o = _dsa_sample_attend(page_table, cache_k.reshape(-1, PAGE_SIZE, DSA_WIDTH),
                           cache_v.reshape(-1, PAGE_SIZE, DSA_WIDTH), k_new, v_new, mask,
                           qbig.astype(BF16), rel_bias, t_new)
    return o[:, :t_new, :].reshape(db * t_new, DSA_WIDTH).astype(BF16)


def _post_body(x_ref, og_ref, rg_ref, gg_ref, gd_ref, od_ref, wg_ref, wd_ref, wo_ref, gf_ref,
               x1_ref, hf_ref):
    rg = rg_ref[...]
    a = (og_ref[...] * (rg * _sigmoid(rg))).astype(BF16)
    y_g = jnp.dot(a, wg_ref[...], preferred_element_type=F32)
    y_d = jnp.dot(od_ref[...], wd_ref[...], preferred_element_type=F32)
    mix = (_sigmoid(gg_ref[...]) * y_g + _sigmoid(gd_ref[...]) * y_d).astype(BF16)
    x1 = x_ref[...] + jnp.dot(mix, wo_ref[...], preferred_element_type=F32)
    x1_ref[...] = x1
    hf_ref[...] = _rms(x1, gf_ref[...]).astype(BF16)


def _post(x2d, og, zA, od, w_gla, w_dsa, w_o, g_ffn):
    n = x2d.shape[0]
    tm = min(n, 512)
    row = lambda i: (i, 0)
    const = lambda i: (0, 0)
    return pl.pallas_call(
        _post_body,
        grid=(n // tm,),
        in_specs=[pl.BlockSpec((tm, D_MODEL), row),
                  pl.BlockSpec((tm, GLA_DV), row),
                  pl.BlockSpec((tm, GLA_DV), lambda i: (i, COL_RG)),
                  pl.BlockSpec((tm, D_MODEL), lambda i: (i, COL_GG)),
                  pl.BlockSpec((tm, D_MODEL), lambda i: (i, COL_GD)),
                  pl.BlockSpec((tm, DSA_WIDTH), row),
                  pl.BlockSpec((GLA_DV, D_MODEL), const),
                  pl.BlockSpec((DSA_WIDTH, D_MODEL), const),
                  pl.BlockSpec((D_MODEL, D_MODEL), const),
                  pl.BlockSpec((1, D_MODEL), const)],
        out_specs=[pl.BlockSpec((tm, D_MODEL), row), pl.BlockSpec((tm, D_MODEL), row)],
        out_shape=[jax.ShapeDtypeStruct((n, D_MODEL), F32), jax.ShapeDtypeStruct((n, D_MODEL), BF16)],
        compiler_params=pltpu.CompilerParams(dimension_semantics=("arbitrary",),
                                             vmem_limit_bytes=VMEM_LIMIT),
        name="post_mix",
    )(x2d, og, zA, zA, zA, od, w_gla, w_dsa, w_o, g_ffn.reshape(1, -1))


FFN_TILE = 256


def _ffn_body(hf_ref, x1_ref, wg_ref, wu_ref, wd_ref, gfin_ref, y_ref, acc_scr):
    k = pl.program_id(1)
    hf = hf_ref[...]
    gate = jnp.dot(hf, wg_ref[...], preferred_element_type=F32)
    up = jnp.dot(hf, wu_ref[...], preferred_element_type=F32)
    part = jnp.dot((gate * _sigmoid(gate) * up).astype(BF16), wd_ref[...], preferred_element_type=F32)

    @pl.when(k == 0)
    def _():
        acc_scr[...] = x1_ref[...] + part

    @pl.when(k > 0)
    def _():
        acc_scr[...] = acc_scr[...] + part

    @pl.when(k == pl.num_programs(1) - 1)
    def _():
        y_ref[...] = _rms(acc_scr[...], gfin_ref[...])


def _ffn(hf, x1, w_gate, w_up, w_down, g_final):
    n = hf.shape[0]
    d_ff = w_gate.shape[1]
    tm = min(n, 1024)
    row = lambda i, k: (i, 0)
    return pl.pallas_call(
        _ffn_body,
        grid=(n // tm, d_ff // FFN_TILE),
        in_specs=[pl.BlockSpec((tm, D_MODEL), row),
                  pl.BlockSpec((tm, D_MODEL), row),
                  pl.BlockSpec((D_MODEL, FFN_TILE), lambda i, k: (0, k)),
                  pl.BlockSpec((D_MODEL, FFN_TILE), lambda i, k: (0, k)),
                  pl.BlockSpec((FFN_TILE, D_MODEL), lambda i, k: (k, 0)),
                  pl.BlockSpec((1, D_MODEL), lambda i, k: (0, 0))],
        out_specs=pl.BlockSpec((tm, D_MODEL), row),
        out_shape=jax.ShapeDtypeStruct((n, D_MODEL), F32),
        scratch_shapes=[pltpu.VMEM((tm, D_MODEL), F32)],
        compiler_params=pltpu.CompilerParams(dimension_semantics=("arbitrary", "arbitrary"),
                                             vmem_limit_bytes=VMEM_LIMIT),
        name="ffn",
    )(hf, x1, w_gate, w_up, w_down, g_final.reshape(1, -1))


def _gate_weight_pad(w_gate_up):
    pad = jnp.zeros((LANES, GLA_DK), w_gate_up.dtype)
    return pad.at[MISC_ALOW:MISC_ALOW + GLA_GATE_RANK].set(w_gate_up).astype(BF16)


def kernel(x_prompt, x_sample, cache_k, cache_v, cache_kidx, state_gla, page_table, g_mix, w_in,
           w_gate_up, b_gate, gla_norm_g, w_gla_branch, idx_k_g, idx_k_b, w_dsa_branch, w_o, g_ffn,
           w_ffn_gate, w_ffn_up, w_ffn_down, rel_bias, g_final):
    depth = w_in.shape[0]
    assert depth == 1, "the final RMSNorm is fused into the FFN kernel of the single layer"
    nb, l, _ = x_prompt.shape
    db, t_new, _ = x_sample.shape
    layer = 0
    w_cat = _pack_w_in(w_in[layer])
    wgu_pad = _gate_weight_pad(w_gate_up[layer])
    w_gla = w_gla_branch[layer].astype(BF16)
    w_dsa = w_dsa_branch[layer].astype(BF16)
    w_out = w_o[layer].astype(BF16)
    w_fg, w_fu, w_fd = (w.astype(BF16) for w in (w_ffn_gate[layer], w_ffn_up[layer], w_ffn_down[layer]))

    xp = x_prompt.reshape(nb * l, D_MODEL)
    zA, kd, vd, misc, kin = _in_proj(xp, g_mix[layer], w_cat, idx_k_g[layer], idx_k_b[layer])
    zA3 = zA.reshape(nb, l, ZA_WIDTH)
    s0 = jnp.zeros((nb, GLA_HEADS, GLA_HEAD_DK, GLA_HEAD_DV), state_gla.dtype)
    og, s_p = _gla(zA3, zA3, zA3, misc.reshape(nb, l, PROJ_TILE), (COL_QG, COL_KG, COL_VG), wgu_pad,
                   b_gate[layer], gla_norm_g[layer], s0, GLA_CHUNK, GLA_TILE, GLA_TILE)
    od = _dsa_prompt(zA, kd, vd, misc, kin, rel_bias, nb, l)
    x1, hf = _post(xp, og.reshape(nb * l, GLA_DV), zA, od.reshape(nb * l, DSA_WIDTH), w_gla, w_dsa,
                   w_out, g_ffn[layer])
    y_p = _ffn(hf, x1, w_fg, w_fu, w_fd, g_final).reshape(nb, l, D_MODEL)

    xs = x_sample.reshape(db * t_new, D_MODEL)
    zA_s, kd_s, vd_s, misc_s, kin_s = _in_proj(xs, g_mix[layer], w_cat, idx_k_g[layer], idx_k_b[layer])
    pad_t = lambda a: jnp.pad(a.reshape(db, t_new, -1), ((0, 0), (0, SAMPLE_CHUNK - t_new), (0, 0)))
    og_s, s_s = _gla(pad_t(zA_s[:, 0:GLA_DK]), pad_t(zA_s[:, GLA_DK:2 * GLA_DK]),
                     pad_t(zA_s[:, 2 * GLA_DK:2 * GLA_DK + GLA_DV]), pad_t(misc_s[:, 0:LANES]), (0, 0, 0),
                     wgu_pad, b_gate[layer], gla_norm_g[layer], state_gla[layer], SAMPLE_CHUNK, SAMPLE_CHUNK, t_new)
    og_s = og_s[:, :t_new, :].reshape(db * t_new, GLA_DV)
    od_s = _dsa_sample(zA_s, kd_s, vd_s, misc_s, kin_s, rel_bias, cache_k[layer], cache_v[layer],
                       cache_kidx[layer], page_table, db, t_new)
    x1_s, hf_s = _post(xs, og_s, zA_s, od_s, w_gla, w_dsa, w_out, g_ffn[layer])
    y_s = _ffn(hf_s, x1_s, w_fg, w_fu, w_fd, g_final).reshape(db, t_new, D_MODEL)

    heads = lambda a, n, t: a.reshape(1, n, t, DSA_HEADS, DSA_HEAD_DIM)
    return (y_p, y_s,
            heads(kd, nb, l), heads(vd, nb, l), kin.reshape(1, nb, l, IDX_DIM), s_p[None],
            heads(kd_s, db, t_new), heads(vd_s, db, t_new), kin_s.reshape(1, db, t_new, IDX_DIM), s_s[None])
```

```python
import functools
import math

import numpy as np
import jax
import jax.numpy as jnp
from jax import lax
from jax.experimental import pallas as pl
from jax.experimental.pallas import tpu as pltpu

F32, BF16, I32 = jnp.float32, jnp.bfloat16, jnp.int32

D_MODEL = 1024
GLA_HEADS = 4
GLA_HEAD_DK = 128
GLA_HEAD_DV = 256
GLA_DK = GLA_HEADS * GLA_HEAD_DK
GLA_DV = GLA_HEADS * GLA_HEAD_DV
GLA_GATE_RANK = 16
GLA_GATE_TAU = 16.0
DSA_HEADS = 8
DSA_HEAD_DIM = 64
DSA_WIDTH = DSA_HEADS * DSA_HEAD_DIM
IDX_HEADS = 4
IDX_DIM = 64
TOPK_MAX = 256
QUERY_BLOCK = 128
PAGE_SIZE = 128
REL_BUCKETS = 32
REL_MAX_DIST = 128
RMS_EPS = 1e-6
LN_EPS = 1e-6
SPLIT_SIZES = (GLA_DK, GLA_DK, GLA_DV, GLA_DV, GLA_GATE_RANK, DSA_WIDTH, DSA_WIDTH, DSA_WIDTH,
               IDX_HEADS * IDX_DIM, IDX_DIM, IDX_HEADS, D_MODEL, D_MODEL)

LANES = 128
SUBLANES = 8
VMEM_LIMIT = 56 * 1024 * 1024

PROJ_TILE = 512
ZA_TILES = 11
ZA_WIDTH = ZA_TILES * PROJ_TILE
COL_QG, COL_KG = 0, 1
COL_VG, COL_RG, COL_GG, COL_GD = 1, 2, 3, 4
ZA_QD = 10 * PROJ_TILE
MISC_ALOW, MISC_WI, MISC_QI = 64, 80, 128

GLA_CHUNK = 64
GLA_TILE = 256
SAMPLE_CHUNK = 16
DSA_TK = 512
BIAS_TAB_ROWS = 1408
BIAS_TAB_OFF = 896
V_ROWS = 80
NEG = -1e30
LOG2E = 1.4426950408889634
BISECT_STEPS = 40


def _sigmoid(x):
    return 1.0 / (1.0 + jnp.exp(-x))


def _rms(x, g):
    return x * lax.rsqrt(jnp.mean(x * x, axis=-1, keepdims=True) + RMS_EPS) * g


def _t5_bucket_np(n):
    n = np.maximum(np.asarray(n, np.int64), 0)
    max_exact = REL_BUCKETS // 2
    large = max_exact + (np.log(np.maximum(n, 1).astype(np.float32) / np.float32(max_exact))
                         / np.float32(math.log(REL_MAX_DIST / max_exact))
                         * np.float32(REL_BUCKETS - max_exact)).astype(np.int32)
    large = np.minimum(large, REL_BUCKETS - 1)
    return np.where(n < max_exact, n, large).astype(np.int32)


def _inproj_body(x_ref, g_ref, w_ref, ikg_ref, ikb_ref, za_ref, kd_ref, vd_ref, misc_ref, kin_ref, h_scr):
    j = pl.program_id(1)

    @pl.when(j == 0)
    def _():
        h_scr[...] = _rms(x_ref[...], g_ref[...]).astype(BF16)

    res = jnp.dot(h_scr[...], w_ref[...], preferred_element_type=F32)

    @pl.when(j < ZA_TILES)
    def _():
        za_ref[...] = res

    @pl.when(j == ZA_TILES)
    def _():
        kd_ref[...] = res

    @pl.when(j == ZA_TILES + 1)
    def _():
        vd_ref[...] = res

    @pl.when(j == ZA_TILES + 2)
    def _():
        misc_ref[...] = res
        ki = res[:, 0:IDX_DIM]
        mu = jnp.mean(ki, axis=-1, keepdims=True)
        var = jnp.mean(jnp.square(ki - mu), axis=-1, keepdims=True)
        kin_ref[...] = (ki - mu) * lax.rsqrt(var + LN_EPS) * ikg_ref[...] + ikb_ref[...]


def _pack_w_in(w_in):
    pts = np.cumsum((0,) + SPLIT_SIZES)
    seg = [w_in[:, int(pts[i]):int(pts[i + 1])] for i in range(len(SPLIT_SIZES))]
    q_g, k_g, v_g, r_g, a_low, q_d, k_d, v_d, q_i, k_i, w_i, gate_g, gate_d = seg
    z = lambda n: jnp.zeros((w_in.shape[0], n), w_in.dtype)
    misc = jnp.concatenate([k_i, a_low, w_i, z(LANES - MISC_WI - IDX_HEADS), q_i,
                            z(PROJ_TILE - MISC_QI - IDX_HEADS * IDX_DIM)], axis=1)
    return jnp.concatenate([q_g, k_g, v_g, r_g, gate_g, gate_d, q_d, k_d, v_d, misc], axis=1).astype(BF16)


def _in_proj(x2d, g_mix, w_cat, idx_k_g, idx_k_b):
    n = x2d.shape[0]
    tm = min(n, 1024)
    nj = w_cat.shape[1] // PROJ_TILE
    row = lambda i, j: (i, 0)
    return pl.pallas_call(
        _inproj_body,
        grid=(n // tm, nj),
        in_specs=[pl.BlockSpec((tm, D_MODEL), row),
                  pl.BlockSpec((1, D_MODEL), lambda i, j: (0, 0)),
                  pl.BlockSpec((D_MODEL, PROJ_TILE), lambda i, j: (0, j)),
                  pl.BlockSpec((1, IDX_DIM), lambda i, j: (0, 0)),
                  pl.BlockSpec((1, IDX_DIM), lambda i, j: (0, 0))],
        out_specs=[pl.BlockSpec((tm, PROJ_TILE), lambda i, j: (i, jnp.minimum(j, ZA_TILES - 1))),
                   pl.BlockSpec((tm, PROJ_TILE), row),
                   pl.BlockSpec((tm, PROJ_TILE), row),
                   pl.BlockSpec((tm, PROJ_TILE), row),
                   pl.BlockSpec((tm, IDX_DIM), row)],
        out_shape=[jax.ShapeDtypeStruct((n, ZA_WIDTH), F32),
                   jax.ShapeDtypeStruct((n, PROJ_TILE), F32),
                   jax.ShapeDtypeStruct((n, PROJ_TILE), F32),
                   jax.ShapeDtypeStruct((n, PROJ_TILE), F32),
                   jax.ShapeDtypeStruct((n, IDX_DIM), F32)],
        scratch_shapes=[pltpu.VMEM((tm, D_MODEL), BF16)],
        compiler_params=pltpu.CompilerParams(dimension_semantics=("arbitrary", "arbitrary"),
                                             vmem_limit_bytes=VMEM_LIMIT),
        name="in_proj",
    )(x2d, g_mix.reshape(1, -1), w_cat, idx_k_g.reshape(1, -1), idx_k_b.reshape(1, -1))


def _gla_consts(c):
    nlev = int(math.log2(c))
    t = np.arange(c)[:, None]
    s = np.arange(c)[None, :]
    mats = [(s <= t), np.ones((c, c), bool)]
    masks = []
    for l in range(nlev):
        mid = ((t >> (l + 1)) << (l + 1)) + (1 << l) - 1
        mats.append(s <= mid)
        masks.append(((t >> (l + 1)) == (s >> (l + 1))) & (((t >> l) & 1) == 1) & (((s >> l) & 1) == 0))
    masks.append(t == s)
    return (jnp.asarray(np.concatenate(mats, 0).astype(np.float32), BF16),
            jnp.asarray(np.stack(masks).astype(np.float32), F32), nlev)


_NT = (((1,), (1,)), ((), ()))
_TN = (((0,), (0,)), ((), ()))


def _gla_body(q_ref, k_ref, v_ref, misc_ref, wgu_ref, bg_ref, gn_ref, mst_ref, lmask_ref, s0_ref,
              o_ref, sout_ref, s_scr, la_scr, *, c, nc, nlev, valid_rows):
    step = pl.program_id(1)

    @pl.when(step == 0)
    def _():
        for h in range(GLA_HEADS):
            s_scr[h] = s0_ref[h].T

    x = jnp.dot(misc_ref[...].astype(BF16), wgu_ref[...], preferred_element_type=F32) + bg_ref[...]
    log_a = (jnp.minimum(x, 0.0) - jnp.log1p(jnp.exp(-jnp.abs(x)))) * (1.0 / GLA_GATE_TAU)
    if valid_rows < c * nc:
        log_a = jnp.where(lax.broadcasted_iota(I32, log_a.shape, 0) < valid_rows, log_a, 0.0)
    la_scr[...] = log_a
    scale = GLA_HEAD_DK ** -0.5

    def chunk(ci, carry):
        r0 = pl.multiple_of(ci * c, c)
        rows = pl.ds(r0, c)
        g_all = la_scr[rows, :]
        g_hi = g_all.astype(BF16)
        g_lo = (g_all - g_hi.astype(F32)).astype(BF16)
        cs = jnp.dot(mst_ref[...], jnp.concatenate([g_hi, g_lo], axis=1), preferred_element_type=F32)
        cs = cs[:, :GLA_DK] + cs[:, GLA_DK:]
        for h in range(GLA_HEADS):
            ksl = slice(h * GLA_HEAD_DK, (h + 1) * GLA_HEAD_DK)
            vsl = slice(h * GLA_HEAD_DV, (h + 1) * GLA_HEAD_DV)
            q = q_ref[rows, ksl] * scale
            k = k_ref[rows, ksl]
            v = v_ref[rows, vsl].astype(BF16)
            b = cs[0:c, ksl]
            e_last = cs[c:2 * c, ksl]
            st = s_scr[h]
            o = lax.dot_general((q * jnp.exp(b)).astype(BF16), st.astype(BF16), _NT,
                                preferred_element_type=F32)
            att = lmask_ref[nlev] * lax.dot_general(q.astype(BF16), k.astype(BF16), _NT,
                                                    preferred_element_type=F32)
            for l in range(nlev):
                e = cs[(2 + l) * c:(3 + l) * c, ksl]
                ql = (q * jnp.exp(jnp.minimum(b - e, 0.0))).astype(BF16)
                kl = (k * jnp.exp(jnp.minimum(e - b, 0.0))).astype(BF16)
                att = att + lmask_ref[l] * lax.dot_general(ql, kl, _NT, preferred_element_type=F32)
            o = o + jnp.dot(att.astype(BF16), v, preferred_element_type=F32)
            k_st = (k * jnp.exp(e_last - b)).astype(BF16)
            s_scr[h] = st * jnp.exp(e_last[0:1, :]) + lax.dot_general(v, k_st, _TN,
                                                                      preferred_element_type=F32)
            o_ref[rows, vsl] = _rms(o, gn_ref[...])
        return carry

    lax.fori_loop(0, nc, chunk, 0)

    @pl.when(step == pl.num_programs(1) - 1)
    def _():
        for h in range(GLA_HEADS):
            sout_ref[h] = s_scr[h].T


def _gla(q_arr, k_arr, v_arr, misc_arr, cols, wgu_pad, b_gate, gla_norm_g, s0, c, tile, valid_rows):
    nb, l = q_arr.shape[0], q_arr.shape[1]
    mst, lmask, nlev = _gla_consts(c)
    const2 = lambda b, s: (0, 0)
    body = functools.partial(_gla_body, c=c, nc=tile // c, nlev=nlev, valid_rows=valid_rows)
    return pl.pallas_call(
        body,
        grid=(nb, l // tile),
        in_specs=[pl.BlockSpec((None, tile, GLA_DK), lambda b, s: (b, s, cols[0])),
                  pl.BlockSpec((None, tile, GLA_DK), lambda b, s: (b, s, cols[1])),
                  pl.BlockSpec((None, tile, GLA_DV), lambda b, s: (b, s, cols[2])),
                  pl.BlockSpec((None, tile, LANES), lambda b, s: (b, s, 0)),
                  pl.BlockSpec((LANES, GLA_DK), const2),
                  pl.BlockSpec((1, GLA_DK), const2),
                  pl.BlockSpec((1, GLA_HEAD_DV), const2),
                  pl.BlockSpec(mst.shape, const2),
                  pl.BlockSpec(lmask.shape, lambda b, s: (0, 0, 0)),
                  pl.BlockSpec((None, GLA_HEADS, GLA_HEAD_DK, GLA_HEAD_DV), lambda b, s: (b, 0, 0, 0))],
        out_specs=[pl.BlockSpec((None, tile, GLA_DV), lambda b, s: (b, s, 0)),
                   pl.BlockSpec((None, GLA_HEADS, GLA_HEAD_DK, GLA_HEAD_DV), lambda b, s: (b, 0, 0, 0))],
        out_shape=[jax.ShapeDtypeStruct((nb, l, GLA_DV), F32),
                   jax.ShapeDtypeStruct((nb, GLA_HEADS, GLA_HEAD_DK, GLA_HEAD_DV), F32)],
        scratch_shapes=[pltpu.VMEM((GLA_HEADS, GLA_HEAD_DV, GLA_HEAD_DK), F32),
                        pltpu.VMEM((tile, GLA_DK), F32)],
        compiler_params=pltpu.CompilerParams(dimension_semantics=("arbitrary", "arbitrary"),
                                             vmem_limit_bytes=VMEM_LIMIT),
        name="gla",
    )(q_arr, k_arr, v_arr, misc_arr, wgu_pad, b_gate.reshape(1, -1), gla_norm_g.reshape(1, -1),
      mst, lmask, s0)


def _select_threshold(sc_ref, nt, tr, topk, small, lane_ok, pos_bits):
    w = sc_ref.shape[1]
    inf = jnp.float32(jnp.inf)

    def over_tiles(fn, init):
        def body(i, carry):
            r0 = pl.multiple_of(i * tr, tr)
            return fn(carry, sc_ref[pl.ds(r0, tr), :], r0)
        return lax.fori_loop(0, nt, body, init)

    fold = lambda x: x.reshape(tr // SUBLANES, SUBLANES, w)
    zeros8 = jnp.zeros((SUBLANES, w), I32)
    pinf8 = jnp.full((SUBLANES, w), inf, F32)

    def count(pred):
        acc = over_tiles(lambda a, blk, r0: a + jnp.sum(fold(pred(blk, r0).astype(I32)), axis=0), zeros8)
        return jnp.sum(acc, axis=0, keepdims=True)

    def min_where(pred):
        acc = over_tiles(lambda a, blk, r0: jnp.minimum(
            a, jnp.min(fold(jnp.where(pred(blk, r0), blk, inf)), axis=0)), pinf8)
        return jnp.min(acc, axis=0, keepdims=True)

    def minmax(carry, blk, r0):
        mx, mn = carry
        b3 = fold(blk)
        return (jnp.maximum(mx, jnp.max(b3, axis=0)),
                jnp.minimum(mn, jnp.min(jnp.where(b3 == -inf, inf, b3), axis=0)))

    mx8, mn8 = over_tiles(minmax, (-pinf8, pinf8))
    hi = jnp.max(mx8, axis=0, keepdims=True)
    lo = jnp.min(mn8, axis=0, keepdims=True)

    live = lane_ok & jnp.logical_not(small)

    def bisect(state):
        lo, hi, c_lo, it, _ = state
        mid = 0.5 * lo + 0.5 * hi
        c_mid = count(lambda blk, r0: blk >= mid)
        ge = c_mid >= topk
        c_lo = jnp.where(ge, c_mid, c_lo)
        go = jnp.max(((c_lo > topk) & live).astype(I32))
        return jnp.where(ge, mid, lo), jnp.where(ge, hi, mid), c_lo, it + 1, go

    lo, hi, _, _, _ = lax.while_loop(
        lambda s: (s[4] > 0) & (s[3] < BISECT_STEPS), bisect,
        (lo, hi, jnp.full((1, w), 2 ** 30, I32), jnp.int32(0), jnp.int32(1)))
    v0 = min_where(lambda blk, r0: blk >= lo)

    def gt_next(v):
        def f(carry, blk, r0):
            cg, nx = carry
            gt = blk > v
            return (cg + jnp.sum(fold(gt.astype(I32)), axis=0),
                    jnp.minimum(nx, jnp.min(fold(jnp.where(gt, blk, inf)), axis=0)))
        cg8, nx8 = over_tiles(f, (zeros8, pinf8))
        return jnp.sum(cg8, axis=0, keepdims=True), jnp.min(nx8, axis=0, keepdims=True)

    def peel(state):
        v, _, _ = state
        cg, nx = gt_next(v)
        move = (cg >= topk) & live
        return jnp.where(move, nx, v), cg, jnp.max(move.astype(I32))

    v, cnt_gt, _ = lax.while_loop(lambda s: s[2] > 0, peel,
                                  (v0, jnp.zeros((1, w), I32), jnp.int32(1)))
    cnt_ge = count(lambda blk, r0: blk >= v)
    excess = (cnt_ge > topk) & live

    @pl.when(jnp.max(excess.astype(I32)) > 0)
    def _():
        need = topk - cnt_gt

        def tie_count(m):
            def pred(blk, r0):
                pos = r0 + lax.broadcasted_iota(I32, (tr, w), 0)
                return (blk == v) & (pos <= m)
            return count(pred)

        def bit_step(it, p):
            cand = p + jnp.left_shift(jnp.int32(1), pos_bits - 1 - it)
            return jnp.where(tie_count(cand) <= need, cand, p)

        p = lax.fori_loop(0, pos_bits, bit_step, jnp.zeros((1, w), I32))
        bound = jnp.where(excess, p, jnp.int32(2 ** 30))

        def rewrite(i, carry):
            r0 = pl.multiple_of(i * tr, tr)
            blk = sc_ref[pl.ds(r0, tr), :]
            pos = r0 + lax.broadcasted_iota(I32, (tr, w), 0)
            sc_ref[pl.ds(r0, tr), :] = jnp.where((blk == v) & (pos > bound), -inf, blk)
            return carry

        lax.fori_loop(0, nt, rewrite, 0)

    return jnp.where(small, -inf, v)


def _dsa_prompt_body(relb_ref, btab_ref, ki_ref, qi_ref, wi_ref, kd_ref, vt_ref, qd_ref, o_ref,
                     sc_scr, tbl_scr, q2_scr, acc_scr, m_scr, o_scr, lg_scr, pr_scr, *,
                     topk, pos_bits, far_bucket):
    b = pl.program_id(0)
    j = pl.program_id(1)
    tk = DSA_TK
    hd = DSA_HEAD_DIM

    @pl.when((b == 0) & (j == 0))
    def _():
        q2_scr[...] = jnp.zeros(q2_scr.shape, BF16)

        def build(ci, carry):
            r0 = pl.multiple_of(ci * LANES, LANES)
            bt = btab_ref[pl.ds(r0, LANES), :]
            for h in range(DSA_HEADS):
                t = jnp.zeros(bt.shape, F32)
                for bk in range(REL_BUCKETS):
                    t = jnp.where(bt == bk, (relb_ref[bk, h] - relb_ref[far_bucket, h]) * LOG2E, t)
                tbl_scr[h, pl.ds(r0, LANES), :] = t
            return carry

        lax.fori_loop(0, BIAS_TAB_ROWS // LANES, build, 0)

    nt = j // (tk // QUERY_BLOCK) + 1
    qpos = j * QUERY_BLOCK + lax.broadcasted_iota(I32, (1, LANES), 1)
    for p in range(DSA_HEADS // 2):
        q2_scr[p, 0:hd, 0:LANES] = qd_ref[2 * p * hd:(2 * p + 1) * hd, :]
        q2_scr[p, hd:2 * hd, LANES:2 * LANES] = qd_ref[(2 * p + 1) * hd:(2 * p + 2) * hd, :]

    wi = wi_ref[...]

    def score_tile(i, carry):
        r0 = pl.multiple_of(i * tk, tk)
        s4 = jnp.dot(ki_ref[pl.ds(r0, tk), :], qi_ref[...], preferred_element_type=F32)
        sc = jnp.zeros((tk, LANES), F32)
        for h in range(IDX_HEADS):
            sc = sc + jnp.maximum(s4[:, h * LANES:(h + 1) * LANES], 0.0) * wi[h:h + 1, :]
        kpos = r0 + lax.broadcasted_iota(I32, (tk, LANES), 0)
        sc_scr[pl.ds(r0, tk), :] = jnp.where(kpos <= qpos, sc, -jnp.inf)
        return carry

    lax.fori_loop(0, nt, score_tile, 0)

    small = (qpos + 1) < topk
    kstar = _select_threshold(sc_scr, nt, tk, topk, small, jnp.full((1, LANES), True), pos_bits)

    m_scr[...] = jnp.full(m_scr.shape, NEG, F32)
    acc_scr[...] = jnp.zeros(acc_scr.shape, F32)
    vrows = acc_scr.shape[0] // DSA_HEADS

    def attend(i, near):
        r0 = pl.multiple_of(i * tk, tk)
        blk = sc_scr[pl.ds(r0, tk), :]
        if near:
            kpos = r0 + lax.broadcasted_iota(I32, (tk, LANES), 0)
            addm = jnp.where(blk >= kstar, jnp.where(kpos <= qpos, 0.0, NEG), NEG)
            off = pl.multiple_of(i * tk - j * QUERY_BLOCK + BIAS_TAB_OFF, LANES)
        else:
            addm = jnp.where(blk >= kstar, 0.0, NEG)
        tile_max = []
        for p in range(DSA_HEADS // 2):
            lg2 = jnp.dot(kd_ref[pl.ds(r0, tk), p * LANES:(p + 1) * LANES], q2_scr[p],
                          preferred_element_type=F32)
            for hh in range(2):
                h = 2 * p + hh
                lg = lg2[:, hh * LANES:(hh + 1) * LANES] + addm
                if near:
                    lg = lg + tbl_scr[h, pl.ds(off, tk), :]
                lg_scr[h] = lg
                tile_max.append(jnp.max(lg, axis=0, keepdims=True))
        alpha = []
        for h in range(DSA_HEADS):
            m_old = m_scr[h:h + 1, :]
            m_new = jnp.maximum(m_old, tile_max[h])
            m_scr[h:h + 1, :] = m_new
            alpha.append(jnp.exp2(m_old - m_new))
            pr_scr[h] = jnp.exp2((lg_scr[h] - m_new).astype(BF16))
        for h in range(DSA_HEADS):
            rows = slice(h * vrows, (h + 1) * vrows)
            acc_scr[rows, :] = alpha[h] * acc_scr[rows, :] + jnp.dot(
                vt_ref[rows, pl.ds(r0, tk)], pr_scr[h], preferred_element_type=F32)

    def step(near):
        def body(i, carry):
            attend(i, near)
            return carry
        return body

    n_far = jnp.maximum(nt - 2, 0)
    lax.fori_loop(0, n_far, step(False), 0)
    lax.fori_loop(n_far, nt, step(True), 0)

    for h in range(DSA_HEADS):
        o_scr[h * hd:(h + 1) * hd, :] = (acc_scr[h * vrows:h * vrows + hd, :]
                                         / acc_scr[h * vrows + hd:h * vrows + hd + 1, :])
    o_ref[...] = o_scr[...].T.astype(BF16)


def _bias_bucket_table():
    u = np.arange(BIAS_TAB_ROWS)[:, None]
    r = np.arange(LANES)[None, :]
    return jnp.asarray(_t5_bucket_np(r + BIAS_TAB_OFF - u), I32)


def _dsa_prompt(zA, kd, vd, misc, kin, rel_bias, nb, l):
    nq = l // QUERY_BLOCK
    topk = min(TOPK_MAX, l // 4)
    assert l % DSA_TK == 0
    far = _t5_bucket_np(np.arange(REL_MAX_DIST, max(l, REL_MAX_DIST + 1)))
    assert (far == far[0]).all()
    ki = kin.reshape(nb, l, IDX_DIM).astype(BF16)
    qi = misc[:, MISC_QI:MISC_QI + IDX_HEADS * IDX_DIM].reshape(nb, nq, QUERY_BLOCK, IDX_HEADS, IDX_DIM)
    qi = qi.transpose(0, 4, 1, 3, 2).reshape(nb, IDX_DIM, nq * IDX_HEADS * QUERY_BLOCK).astype(BF16)
    wi = misc[:, MISC_WI:MISC_WI + IDX_HEADS] * ((IDX_DIM ** -0.5) * (IDX_HEADS ** -0.5))
    wi = wi.reshape(nb, nq, QUERY_BLOCK, IDX_HEADS).transpose(0, 1, 3, 2)
    wi = jnp.pad(wi, ((0, 0), (0, 0), (0, SUBLANES - IDX_HEADS), (0, 0))).reshape(nb, nq * SUBLANES, QUERY_BLOCK)
    kd_bf = kd.reshape(nb, l, DSA_WIDTH).astype(BF16)
    vt = vd.reshape(nb, l, DSA_HEADS, DSA_HEAD_DIM).transpose(0, 2, 3, 1)
    vt = jnp.concatenate([vt, jnp.ones((nb, DSA_HEADS, 1, l), vt.dtype),
                          jnp.zeros((nb, DSA_HEADS, V_ROWS - DSA_HEAD_DIM - 1, l), vt.dtype)], axis=2)
    vt = vt.reshape(nb, DSA_HEADS * V_ROWS, l).astype(BF16)
    qd = (zA[:, ZA_QD:ZA_QD + DSA_WIDTH] * ((DSA_HEAD_DIM ** -0.5) * LOG2E)).reshape(nb, l, DSA_WIDTH)
    qd = qd.transpose(0, 2, 1).astype(BF16)
    body = functools.partial(_dsa_prompt_body, topk=topk, pos_bits=max(1, int(math.ceil(math.log2(l)))),
                             far_bucket=int(far[0]))
    whole = lambda b, j: (b, 0, 0)
    return pl.pallas_call(
        body,
        grid=(nb, nq),
        in_specs=[pl.BlockSpec(memory_space=pltpu.SMEM),
                  pl.BlockSpec((BIAS_TAB_ROWS, LANES), lambda b, j: (0, 0), pipeline_mode=pl.Buffered(1)),
                  pl.BlockSpec((None, l, IDX_DIM), whole, pipeline_mode=pl.Buffered(1)),
                  pl.BlockSpec((None, IDX_DIM, IDX_HEADS * QUERY_BLOCK), lambda b, j: (b, 0, j)),
                  pl.BlockSpec((None, SUBLANES, QUERY_BLOCK), lambda b, j: (b, j, 0)),
                  pl.BlockSpec((None, l, DSA_WIDTH), whole, pipeline_mode=pl.Buffered(1)),
                  pl.BlockSpec((None, DSA_HEADS * V_ROWS, l), whole, pipeline_mode=pl.Buffered(1)),
                  pl.BlockSpec((None, DSA_WIDTH, QUERY_BLOCK), lambda b, j: (b, 0, j))],
        out_specs=pl.BlockSpec((None, QUERY_BLOCK, DSA_WIDTH), lambda b, j: (b, j, 0)),
        out_shape=jax.ShapeDtypeStruct((nb, l, DSA_WIDTH), BF16),
        scratch_shapes=[pltpu.VMEM((l, LANES), F32),
                        pltpu.VMEM((DSA_HEADS, BIAS_TAB_ROWS, LANES), F32),
                        pltpu.VMEM((DSA_HEADS // 2, LANES, 2 * LANES), BF16),
                        pltpu.VMEM((DSA_HEADS * V_ROWS, LANES), F32),
                        pltpu.VMEM((DSA_HEADS, LANES), F32),
                        pltpu.VMEM((DSA_WIDTH, LANES), F32),
                        pltpu.VMEM((DSA_HEADS, DSA_TK, LANES), F32),
                        pltpu.VMEM((DSA_HEADS, DSA_TK, LANES), BF16)],
        compiler_params=pltpu.CompilerParams(dimension_semantics=("arbitrary", "arbitrary"),
                                             vmem_limit_bytes=VMEM_LIMIT),
        name="dsa_prompt",
    )(rel_bias, _bias_bucket_table(), ki, qi, wi, kd_bf, vt, qd)


SEQ_GROUP = LANES // SUBLANES


def _dsa_sample_select_body(pt_ref, *refs, n_pages, t_new, topk, pos_bits):
    page_refs = refs[:n_pages]
    knew_ref, qi_ref, wi_ref, mask_ref, sc_scr = refs[n_pages:]
    g = pl.program_id(1)
    past = n_pages * PAGE_SIZE
    lane = lax.broadcasted_iota(I32, (1, LANES), 1)
    in_group = (lane // SUBLANES) == g

    def scores(k_rows):
        kb = k_rows.astype(BF16)
        sc = jnp.zeros((k_rows.shape[0], LANES), F32)
        for h in range(IDX_HEADS):
            rhs = jnp.where(in_group, qi_ref[h], jnp.zeros((), BF16))
            s = jnp.dot(kb, rhs, preferred_element_type=F32)
            sc = sc + jnp.maximum(s, 0.0) * wi_ref[h:h + 1, :]
        return jnp.where(in_group, sc, 0.0)

    for p in range(n_pages + 1):
        rows = slice(p * PAGE_SIZE, (p + 1) * PAGE_SIZE)
        sc = scores(page_refs[p][...] if p < n_pages else knew_ref[...])

        @pl.when(g == 0)
        def _():
            sc_scr[rows, :] = sc

        @pl.when(g > 0)
        def _():
            sc_scr[rows, :] = sc_scr[rows, :] + sc

    @pl.when(g == SEQ_GROUP - 1)
    def _():
        q_of_lane = lane % SUBLANES
        rows = slice(past, past + PAGE_SIZE)
        cpos = lax.broadcasted_iota(I32, (PAGE_SIZE, LANES), 0)
        sc_scr[rows, :] = jnp.where(cpos <= q_of_lane, sc_scr[rows, :], -jnp.inf)
        lane_ok = q_of_lane < t_new
        small = (past + q_of_lane + 1) < topk
        kstar = _select_threshold(sc_scr, n_pages + 1, PAGE_SIZE, topk, small, lane_ok, pos_bits)
        for p in range(n_pages + 1):
            rows = slice(p * PAGE_SIZE, (p + 1) * PAGE_SIZE)
            mask_ref[rows, :] = jnp.where(sc_scr[rows, :] >= kstar, 1.0, 0.0).astype(BF16)


def _dsa_sample_select(page_table, cache_kidx, kin_new, qi_t, wi_t, t_new, topk):
    db, n_pages = page_table.shape
    ng = db // SEQ_GROUP
    rows = (n_pages + 1) * PAGE_SIZE
    seq = lambda gi, g, pt: (gi * SEQ_GROUP + g, 0, 0)
    page_specs = [pl.BlockSpec((None, PAGE_SIZE, IDX_DIM),
                               functools.partial(lambda gi, g, pt, p: (pt[gi * SEQ_GROUP + g, p], 0, 0), p=p))
                  for p in range(n_pages)]
    body = functools.partial(_dsa_sample_select_body, n_pages=n_pages, t_new=t_new, topk=topk,
                             pos_bits=int(math.ceil(math.log2(rows))))
    grid_spec = pltpu.PrefetchScalarGridSpec(
        num_scalar_prefetch=1,
        grid=(ng, SEQ_GROUP),
        in_specs=page_specs + [pl.BlockSpec((None, PAGE_SIZE, IDX_DIM), seq),
                               pl.BlockSpec((None, IDX_HEADS, IDX_DIM, LANES), lambda gi, g, pt: (gi * SEQ_GROUP + g, 0, 0, 0)),
                               pl.BlockSpec((None, SUBLANES, LANES), seq)],
        out_specs=pl.BlockSpec((None, rows, LANES), lambda gi, g, pt: (gi, 0, 0)),
        scratch_shapes=[pltpu.VMEM((rows, LANES), F32)],
    )
    return pl.pallas_call(
        body, grid_spec=grid_spec,
        out_shape=jax.ShapeDtypeStruct((ng, rows, LANES), BF16),
        compiler_params=pltpu.CompilerParams(dimension_semantics=("arbitrary", "arbitrary"),
                                             vmem_limit_bytes=VMEM_LIMIT),
        name="dsa_sample_select",
    )(page_table, *([cache_kidx] * n_pages), kin_new, qi_t, wi_t)


def _dsa_sample_attend_body(pt_ref, *refs, n_pages):
    k_refs = refs[:n_pages]
    v_refs = refs[n_pages:2 * n_pages]
    (knew_ref, vnew_ref, mask_ref, qbig_ref, btab_ref, rb_ref, o_ref, lg_scr, tbl_scr) = refs[2 * n_pages:]
    b = pl.program_id(0)
    hq = DSA_HEADS * SUBLANES

    @pl.when(b == 0)
    def _():
        bt = btab_ref[...]
        t = jnp.zeros(bt.shape, F32)
        for bk in range(REL_BUCKETS):
            t = jnp.where(bt == bk, rb_ref[bk:bk + 1, :], t)
        tbl_scr[...] = t

    g = b % SEQ_GROUP
    r_i = lax.broadcasted_iota(I32, (LANES, hq), 0)
    c_i = lax.broadcasted_iota(I32, (LANES, hq), 1)
    place = jnp.where(((r_i // SUBLANES) == g) & ((r_i % SUBLANES) == (c_i % SUBLANES)), 1.0, 0.0).astype(BF16)
    far_bias = rb_ref[REL_BUCKETS - 1:REL_BUCKETS, :]

    m = jnp.full((1, hq), NEG, F32)
    for p in range(n_pages + 1):
        rows = slice(p * PAGE_SIZE, (p + 1) * PAGE_SIZE)
        k_rows = k_refs[p][...] if p < n_pages else knew_ref[...]
        lg = jnp.dot(k_rows.astype(BF16), qbig_ref[...], preferred_element_type=F32)
        if p >= n_pages - 1:
            lg = lg + tbl_scr[(p - n_pages + 1) * PAGE_SIZE:(p - n_pages + 2) * PAGE_SIZE, :]
        else:
            lg = lg + far_bias
        sel = jnp.dot(mask_ref[rows, :], place, preferred_element_type=F32)
        lg = lg + jnp.where(sel > 0.5, 0.0, NEG)
        lg_scr[rows, :] = lg
        m = jnp.maximum(m, jnp.max(lg, axis=0, keepdims=True))

    acc = jnp.zeros((hq, DSA_WIDTH), F32)
    lsum = jnp.zeros((hq, LANES), F32)
    ones = jnp.ones((PAGE_SIZE, LANES), BF16)
    for p in range(n_pages + 1):
        rows = slice(p * PAGE_SIZE, (p + 1) * PAGE_SIZE)
        v_rows = v_refs[p][...] if p < n_pages else vnew_ref[...]
        pr = jnp.exp(lg_scr[rows, :] - m).astype(BF16)
        acc = acc + lax.dot_general(pr, v_rows.astype(BF16), _TN, preferred_element_type=F32)
        lsum = lsum + lax.dot_general(pr, ones, _TN, preferred_element_type=F32)
    acc = acc / lsum[:, 0:1]
    head_of_lane = lax.broadcasted_iota(I32, (SUBLANES, DSA_WIDTH), 1) // DSA_HEAD_DIM
    out = jnp.zeros((SUBLANES, DSA_WIDTH), F32)
    for h in range(DSA_HEADS):
        out = jnp.where(head_of_lane == h, acc[h * SUBLANES:(h + 1) * SUBLANES, :], out)
    o_ref[...] = out


def _dsa_sample_attend(page_table, cache_k, cache_v, k_new, v_new, mask, qbig, rel_bias, t_new):
    db, n_pages = page_table.shape
    hq = DSA_HEADS * SUBLANES
    rows = (n_pages + 1) * PAGE_SIZE
    seq = lambda b, pt: (b, 0, 0)
    page = lambda p: functools.partial(lambda b, pt, p: (pt[b, p], 0, 0), p=p)
    kv_spec = lambda p: pl.BlockSpec((None, PAGE_SIZE, DSA_WIDTH), page(p))
    u = np.arange(2 * PAGE_SIZE)[:, None]
    q = (np.arange(hq) % SUBLANES)[None, :]
    btab = jnp.asarray(_t5_bucket_np(PAGE_SIZE + q - u), I32)
    rb = jnp.repeat(rel_bias, SUBLANES, axis=1)
    grid_spec = pltpu.PrefetchScalarGridSpec(
        num_scalar_prefetch=1,
        grid=(db,),
        in_specs=[kv_spec(p) for p in range(n_pages)] + [kv_spec(p) for p in range(n_pages)] + [
            pl.BlockSpec((None, PAGE_SIZE, DSA_WIDTH), seq),
            pl.BlockSpec((None, PAGE_SIZE, DSA_WIDTH), seq),
            pl.BlockSpec((None, rows, LANES), lambda b, pt: (b // SEQ_GROUP, 0, 0)),
            pl.BlockSpec((None, DSA_WIDTH, hq), seq),
            pl.BlockSpec(btab.shape, lambda b, pt: (0, 0)),
            pl.BlockSpec(rb.shape, lambda b, pt: (0, 0))],
        out_specs=pl.BlockSpec((None, SUBLANES, DSA_WIDTH), seq),
        scratch_shapes=[pltpu.VMEM((rows, hq), F32), pltpu.VMEM((2 * PAGE_SIZE, hq), F32)],
    )
    return pl.pallas_call(
        functools.partial(_dsa_sample_attend_body, n_pages=n_pages), grid_spec=grid_spec,
        out_shape=jax.ShapeDtypeStruct((db, SUBLANES, DSA_WIDTH), F32),
        compiler_params=pltpu.CompilerParams(dimension_semantics=("arbitrary",),
                                             vmem_limit_bytes=VMEM_LIMIT),
        name="dsa_sample_attend",
    )(page_table, *([cache_k] * n_pages), *([cache_v] * n_pages), k_new, v_new, mask, qbig, btab, rb)


def _dsa_sample(zA, kd, vd, misc, kin, rel_bias, cache_k, cache_v, cache_kidx, page_table, db, t_new):
    n_pages = page_table.shape[1]
    past = n_pages * PAGE_SIZE
    topk = min(TOPK_MAX, (past + t_new) // 4)
    assert db % SEQ_GROUP == 0 and t_new <= SUBLANES
    pad_rows = lambda a: jnp.pad(a, ((0, 0), (0, PAGE_SIZE - t_new), (0, 0)))
    kin_new = pad_rows(kin.reshape(db, t_new, IDX_DIM))
    k_new = pad_rows(kd.reshape(db, t_new, DSA_WIDTH))
    v_new = pad_rows(vd.reshape(db, t_new, DSA_WIDTH))
    qi = misc[:, MISC_QI:MISC_QI + IDX_HEADS * IDX_DIM].reshape(db, t_new, IDX_HEADS, IDX_DIM)
    qi = jnp.pad(qi.transpose(0, 2, 3, 1), ((0, 0), (0, 0), (0, 0), (0, SUBLANES - t_new)))
    qi_t = jnp.tile(qi, (1, 1, 1, SEQ_GROUP)).astype(BF16)
    wi = misc[:, MISC_WI:MISC_WI + IDX_HEADS] * ((IDX_DIM ** -0.5) * (IDX_HEADS ** -0.5))
    wi = jnp.pad(wi.reshape(db, t_new, IDX_HEADS).transpose(0, 2, 1),
                 ((0, 0), (0, SUBLANES - IDX_HEADS), (0, SUBLANES - t_new)))
    wi_t = jnp.tile(wi, (1, 1, SEQ_GROUP))
    mask = _dsa_sample_select(page_table, cache_kidx, kin_new, qi_t, wi_t, t_new, topk)
    qd = (zA[:, ZA_QD:ZA_QD + DSA_WIDTH] * (DSA_HEAD_DIM ** -0.5)).reshape(db, t_new, DSA_HEADS, DSA_HEAD_DIM)
    qd = jnp.pad(qd.transpose(0, 2, 3, 1), ((0, 0), (0, 0), (0, 0), (0, SUBLANES - t_new)))
    eye = jnp.eye(DSA_HEADS, dtype=qd.dtype)
    qbig = (qd[:, :, :, None, :] * eye[None, :, None, :, None]).reshape(db, DSA_WIDTH, DSA_HEADS * SUBLANES)
    o = _dsa_sample_attend(page_table, cache_k.reshape(-1, PAGE_SIZE, DSA_WIDTH),
                           cache_v.reshape(-1, PAGE_SIZE, DSA_WIDTH), k_new, v_new, mask,
                           qbig.astype(BF16), rel_bias, t_new)
    return o[:, :t_new, :].reshape(db * t_new, DSA_WIDTH).astype(BF16)


def _post_body(x_ref, og_ref, rg_ref, gg_ref, gd_ref, od_ref, wg_ref, wd_ref, wo_ref, gf_ref,
               x1_ref, hf_ref):
    rg = rg_ref[...]
    a = (og_ref[...] * (rg * _sigmoid(rg))).astype(BF16)
    y_g = jnp.dot(a, wg_ref[...], preferred_element_type=F32)
    y_d = jnp.dot(od_ref[...], wd_ref[...], preferred_element_type=F32)
    mix = (_sigmoid(gg_ref[...]) * y_g + _sigmoid(gd_ref[...]) * y_d).astype(BF16)
    x1 = x_ref[...] + jnp.dot(mix, wo_ref[...], preferred_element_type=F32)
    x1_ref[...] = x1
    hf_ref[...] = _rms(x1, gf_ref[...]).astype(BF16)


def _post(x2d, og, zA, od, w_gla, w_dsa, w_o, g_ffn):
    n = x2d.shape[0]
    tm = min(n, 512)
    row = lambda i: (i, 0)
    const = lambda i: (0, 0)
    return pl.pallas_call(
        _post_body,
        grid=(n // tm,),
        in_specs=[pl.BlockSpec((tm, D_MODEL), row),
                  pl.BlockSpec((tm, GLA_DV), row),
                  pl.BlockSpec((tm, GLA_DV), lambda i: (i, COL_RG)),
                  pl.BlockSpec((tm, D_MODEL), lambda i: (i, COL_GG)),
                  pl.BlockSpec((tm, D_MODEL), lambda i: (i, COL_GD)),
                  pl.BlockSpec((tm, DSA_WIDTH), row),
                  pl.BlockSpec((GLA_DV, D_MODEL), const),
                  pl.BlockSpec((DSA_WIDTH, D_MODEL), const),
                  pl.BlockSpec((D_MODEL, D_MODEL), const),
                  pl.BlockSpec((1, D_MODEL), const)],
        out_specs=[pl.BlockSpec((tm, D_MODEL), row), pl.BlockSpec((tm, D_MODEL), row)],
        out_shape=[jax.ShapeDtypeStruct((n, D_MODEL), F32), jax.ShapeDtypeStruct((n, D_MODEL), BF16)],
        compiler_params=pltpu.CompilerParams(dimension_semantics=("arbitrary",),
                                             vmem_limit_bytes=VMEM_LIMIT),
        name="post_mix",
    )(x2d, og, zA, zA, zA, od, w_gla, w_dsa, w_o, g_ffn.reshape(1, -1))


FFN_TILE = 256


def _ffn_body(hf_ref, x1_ref, wg_ref, wu_ref, wd_ref, gfin_ref, y_ref, acc_scr):
    k = pl.program_id(1)
    hf = hf_ref[...]
    gate = jnp.dot(hf, wg_ref[...], preferred_element_type=F32)
    up = jnp.dot(hf, wu_ref[...], preferred_element_type=F32)
    part = jnp.dot((gate * _sigmoid(gate) * up).astype(BF16), wd_ref[...], preferred_element_type=F32)

    @pl.when(k == 0)
    def _():
        acc_scr[...] = x1_ref[...] + part

    @pl.when(k > 0)
    def _():
        acc_scr[...] = acc_scr[...] + part

    @pl.when(k == pl.num_programs(1) - 1)
    def _():
        y_ref[...] = _rms(acc_scr[...], gfin_ref[...])


def _ffn(hf, x1, w_gate, w_up, w_down, g_final):
    n = hf.shape[0]
    d_ff = w_gate.shape[1]
    tm = min(n, 1024)
    row = lambda i, k: (i, 0)
    return pl.pallas_call(
        _ffn_body,
        grid=(n // tm, d_ff // FFN_TILE),
        in_specs=[pl.BlockSpec((tm, D_MODEL), row),
                  pl.BlockSpec((tm, D_MODEL), row),
                  pl.BlockSpec((D_MODEL, FFN_TILE), lambda i, k: (0, k)),
                  pl.BlockSpec((D_MODEL, FFN_TILE), lambda i, k: (0, k)),
                  pl.BlockSpec((FFN_TILE, D_MODEL), lambda i, k: (k, 0)),
                  pl.BlockSpec((1, D_MODEL), lambda i, k: (0, 0))],
        out_specs=pl.BlockSpec((tm, D_MODEL), row),
        out_shape=jax.ShapeDtypeStruct((n, D_MODEL), F32),
        scratch_shapes=[pltpu.VMEM((tm, D_MODEL), F32)],
        compiler_params=pltpu.CompilerParams(dimension_semantics=("arbitrary", "arbitrary"),
                                             vmem_limit_bytes=VMEM_LIMIT),
        name="ffn",
    )(hf, x1, w_gate, w_up, w_down, g_final.reshape(1, -1))


def _gate_weight_pad(w_gate_up):
    pad = jnp.zeros((LANES, GLA_DK), w_gate_up.dtype)
    return pad.at[MISC_ALOW:MISC_ALOW + GLA_GATE_RANK].set(w_gate_up).astype(BF16)


def kernel(x_prompt, x_sample, cache_k, cache_v, cache_kidx, state_gla, page_table, g_mix, w_in,
           w_gate_up, b_gate, gla_norm_g, w_gla_branch, idx_k_g, idx_k_b, w_dsa_branch, w_o, g_ffn,
           w_ffn_gate, w_ffn_up, w_ffn_down, rel_bias, g_final):
    depth = w_in.shape[0]
    assert depth == 1, "the final RMSNorm is fused into the FFN kernel of the single layer"
    nb, l, _ = x_prompt.shape
    db, t_new, _ = x_sample.shape
    layer = 0
    w_cat = _pack_w_in(w_in[layer])
    wgu_pad = _gate_weight_pad(w_gate_up[layer])
    w_gla = w_gla_branch[layer].astype(BF16)
    w_dsa = w_dsa_branch[layer].astype(BF16)
    w_out = w_o[layer].astype(BF16)
    w_fg, w_fu, w_fd = (w.astype(BF16) for w in (w_ffn_gate[layer], w_ffn_up[layer], w_ffn_down[layer]))

    xp = x_prompt.reshape(nb * l, D_MODEL)
    zA, kd, vd, misc, kin = _in_proj(xp, g_mix[layer], w_cat, idx_k_g[layer], idx_k_b[layer])
    zA3 = zA.reshape(nb, l, ZA_WIDTH)
    s0 = jnp.zeros((nb, GLA_HEADS, GLA_HEAD_DK, GLA_HEAD_DV), state_gla.dtype)
    og, s_p = _gla(zA3, zA3, zA3, misc.reshape(nb, l, PROJ_TILE), (COL_QG, COL_KG, COL_VG), wgu_pad,
                   b_gate[layer], gla_norm_g[layer], s0, GLA_CHUNK, GLA_TILE, GLA_TILE)
    od = _dsa_prompt(zA, kd, vd, misc, kin, rel_bias, nb, l)
    x1, hf = _post(xp, og.reshape(nb * l, GLA_DV), zA, od.reshape(nb * l, DSA_WIDTH), w_gla, w_dsa,
                   w_out, g_ffn[layer])
    y_p = _ffn(hf, x1, w_fg, w_fu, w_fd, g_final).reshape(nb, l, D_MODEL)

    xs = x_sample.reshape(db * t_new, D_MODEL)
    zA_s, kd_s, vd_s, misc_s, kin_s = _in_proj(xs, g_mix[layer], w_cat, idx_k_g[layer], idx_k_b[layer])
    pad_t = lambda a: jnp.pad(a.reshape(db, t_new, -1), ((0, 0), (0, SAMPLE_CHUNK - t_new), (0, 0)))
    og_s, s_s = _gla(pad_t(zA_s[:, 0:GLA_DK]), pad_t(zA_s[:, GLA_DK:2 * GLA_DK]),
                     pad_t(zA_s[:, 2 * GLA_DK:2 * GLA_DK + GLA_DV]), pad_t(misc_s[:, 0:LANES]), (0, 0, 0),
                     wgu_pad, b_gate[layer], gla_norm_g[layer], state_gla[layer], SAMPLE_CHUNK, SAMPLE_CHUNK, t_new)
    og_s = og_s[:, :t_new, :].reshape(db * t_new, GLA_DV)
    od_s = _dsa_sample(zA_s, kd_s, vd_s, misc_s, kin_s, rel_bias, cache_k[layer], cache_v[layer],
                       cache_kidx[layer], page_table, db, t_new)
    x1_s, hf_s = _post(xs, og_s, zA_s, od_s, w_gla, w_dsa, w_out, g_ffn[layer])
    y_s = _ffn(hf_s, x1_s, w_fg, w_fu, w_fd, g_final).reshape(db, t_new, D_MODEL)

    heads = lambda a, n, t: a.reshape(1, n, t, DSA_HEADS, DSA_HEAD_DIM)
    return (y_p, y_s,
            heads(kd, nb, l), heads(vd, nb, l), kin.reshape(1, nb, l, IDX_DIM), s_p[None],
            heads(kd_s, db, t_new), heads(vd_s, db, t_new), kin_s.reshape(1, db, t_new, IDX_DIM), s_s[None])
```

```python
import functools
import math

import numpy as np
import jax
import jax.numpy as jnp
from jax import lax
from jax.experimental import pallas as pl
from jax.experimental.pallas import tpu as pltpu

F32, BF16, I32 = jnp.float32, jnp.bfloat16, jnp.int32

D_MODEL = 1024
GLA_HEADS = 4
GLA_HEAD_DK = 128
GLA_HEAD_DV = 256
GLA_DK = GLA_HEADS * GLA_HEAD_DK
GLA_DV = GLA_HEADS * GLA_HEAD_DV
GLA_GATE_RANK = 16
GLA_GATE_TAU = 16.0
DSA_HEADS = 8
DSA_HEAD_DIM = 64
DSA_WIDTH = DSA_HEADS * DSA_HEAD_DIM
IDX_HEADS = 4
IDX_DIM = 64
TOPK_MAX = 256
QUERY_BLOCK = 128
PAGE_SIZE = 128
REL_BUCKETS = 32
REL_MAX_DIST = 128
RMS_EPS = 1e-6
LN_EPS = 1e-6
SPLIT_SIZES = (GLA_DK, GLA_DK, GLA_DV, GLA_DV, GLA_GATE_RANK, DSA_WIDTH, DSA_WIDTH, DSA_WIDTH,
               IDX_HEADS * IDX_DIM, IDX_DIM, IDX_HEADS, D_MODEL, D_MODEL)

LANES = 128
SUBLANES = 8
VMEM_LIMIT = 56 * 1024 * 1024

PROJ_TILE = 512
ZA_TILES = 11
ZA_WIDTH = ZA_TILES * PROJ_TILE
COL_QG, COL_KG = 0, 1
COL_VG, COL_RG, COL_GG, COL_GD = 1, 2, 3, 4
ZA_QD = 10 * PROJ_TILE
MISC_ALOW, MISC_WI, MISC_QI = 64, 80, 128

GLA_CHUNK = 64
GLA_TILE = 256
SAMPLE_CHUNK = 16
DSA_TK = 512
BIAS_TAB_ROWS = 1408
BIAS_TAB_OFF = 896
V_ROWS = 80
NEG = -1e30
LOG2E = 1.4426950408889634
BISECT_STEPS = 15


def _sigmoid(x):
    return 1.0 / (1.0 + jnp.exp(-x))


def _rms(x, g):
    return x * lax.rsqrt(jnp.mean(x * x, axis=-1, keepdims=True) + RMS_EPS) * g


def _t5_bucket_np(n):
    n = np.maximum(np.asarray(n, np.int64), 0)
    max_exact = REL_BUCKETS // 2
    large = max_exact + (np.log(np.maximum(n, 1).astype(np.float32) / np.float32(max_exact))
                         / np.float32(math.log(REL_MAX_DIST / max_exact))
                         * np.float32(REL_BUCKETS - max_exact)).astype(np.int32)
    large = np.minimum(large, REL_BUCKETS - 1)
    return np.where(n < max_exact, n, large).astype(np.int32)


def _inproj_body(x_ref, g_ref, w_ref, ikg_ref, ikb_ref, za_ref, kd_ref, vd_ref, misc_ref, kin_ref, h_scr):
    j = pl.program_id(1)

    @pl.when(j == 0)
    def _():
        h_scr[...] = _rms(x_ref[...], g_ref[...]).astype(BF16)

    res = jnp.dot(h_scr[...], w_ref[...], preferred_element_type=F32)

    @pl.when(j < ZA_TILES)
    def _():
        za_ref[...] = res

    @pl.when(j == ZA_TILES)
    def _():
        kd_ref[...] = res

    @pl.when(j == ZA_TILES + 1)
    def _():
        vd_ref[...] = res

    @pl.when(j == ZA_TILES + 2)
    def _():
        misc_ref[...] = res
        ki = res[:, 0:IDX_DIM]
        mu = jnp.mean(ki, axis=-1, keepdims=True)
        var = jnp.mean(jnp.square(ki - mu), axis=-1, keepdims=True)
        kin_ref[...] = (ki - mu) * lax.rsqrt(var + LN_EPS) * ikg_ref[...] + ikb_ref[...]


def _pack_w_in(w_in):
    pts = np.cumsum((0,) + SPLIT_SIZES)
    seg = [w_in[:, int(pts[i]):int(pts[i + 1])] for i in range(len(SPLIT_SIZES))]
    q_g, k_g, v_g, r_g, a_low, q_d, k_d, v_d, q_i, k_i, w_i, gate_g, gate_d = seg
    z = lambda n: jnp.zeros((w_in.shape[0], n), w_in.dtype)
    misc = jnp.concatenate([k_i, a_low, w_i, z(LANES - MISC_WI - IDX_HEADS), q_i,
                            z(PROJ_TILE - MISC_QI - IDX_HEADS * IDX_DIM)], axis=1)
    return jnp.concatenate([q_g, k_g, v_g, r_g, gate_g, gate_d, q_d, k_d, v_d, misc], axis=1).astype(BF16)


def _in_proj(x2d, g_mix, w_cat, idx_k_g, idx_k_b):
    n = x2d.shape[0]
    tm = min(n, 1024)
    nj = w_cat.shape[1] // PROJ_TILE
    row = lambda i, j: (i, 0)
    return pl.pallas_call(
        _inproj_body,
        grid=(n // tm, nj),
        in_specs=[pl.BlockSpec((tm, D_MODEL), row),
                  pl.BlockSpec((1, D_MODEL), lambda i, j: (0, 0)),
                  pl.BlockSpec((D_MODEL, PROJ_TILE), lambda i, j: (0, j)),
                  pl.BlockSpec((1, IDX_DIM), lambda i, j: (0, 0)),
                  pl.BlockSpec((1, IDX_DIM), lambda i, j: (0, 0))],
        out_specs=[pl.BlockSpec((tm, PROJ_TILE), lambda i, j: (i, jnp.minimum(j, ZA_TILES - 1))),
                   pl.BlockSpec((tm, PROJ_TILE), row),
                   pl.BlockSpec((tm, PROJ_TILE), row),
                   pl.BlockSpec((tm, PROJ_TILE), row),
                   pl.BlockSpec((tm, IDX_DIM), row)],
        out_shape=[jax.ShapeDtypeStruct((n, ZA_WIDTH), F32),
                   jax.ShapeDtypeStruct((n, PROJ_TILE), F32),
                   jax.ShapeDtypeStruct((n, PROJ_TILE), F32),
                   jax.ShapeDtypeStruct((n, PROJ_TILE), F32),
                   jax.ShapeDtypeStruct((n, IDX_DIM), F32)],
        scratch_shapes=[pltpu.VMEM((tm, D_MODEL), BF16)],
        compiler_params=pltpu.CompilerParams(dimension_semantics=("arbitrary", "arbitrary"),
                                             vmem_limit_bytes=VMEM_LIMIT),
        name="in_proj",
    )(x2d, g_mix.reshape(1, -1), w_cat, idx_k_g.reshape(1, -1), idx_k_b.reshape(1, -1))


def _gla_consts(c):
    nlev = int(math.log2(c))
    t = np.arange(c)[:, None]
    s = np.arange(c)[None, :]
    mats = [(s <= t), np.ones((c, c), bool)]
    masks = []
    for l in range(nlev):
        mid = ((t >> (l + 1)) << (l + 1)) + (1 << l) - 1
        mats.append(s <= mid)
        masks.append(((t >> (l + 1)) == (s >> (l + 1))) & (((t >> l) & 1) == 1) & (((s >> l) & 1) == 0))
    masks.append(t == s)
    return (jnp.asarray(np.concatenate(mats, 0).astype(np.float32), BF16),
            jnp.asarray(np.stack(masks).astype(np.float32), F32), nlev)


_NT = (((1,), (1,)), ((), ()))
_TN = (((0,), (0,)), ((), ()))


def _gla_body(q_ref, k_ref, v_ref, misc_ref, wgu_ref, bg_ref, gn_ref, mst_ref, lmask_ref, s0_ref,
              o_ref, sout_ref, s_scr, la_scr, *, c, nc, nlev, valid_rows):
    step = pl.program_id(1)

    @pl.when(step == 0)
    def _():
        for h in range(GLA_HEADS):
            s_scr[h] = s0_ref[h].T

    x = jnp.dot(misc_ref[...].astype(BF16), wgu_ref[...], preferred_element_type=F32) + bg_ref[...]
    log_a = (jnp.minimum(x, 0.0) - jnp.log1p(jnp.exp(-jnp.abs(x)))) * (1.0 / GLA_GATE_TAU)
    if valid_rows < c * nc:
        log_a = jnp.where(lax.broadcasted_iota(I32, log_a.shape, 0) < valid_rows, log_a, 0.0)
    la_scr[...] = log_a
    scale = GLA_HEAD_DK ** -0.5

    def chunk(ci, carry):
        r0 = pl.multiple_of(ci * c, c)
        rows = pl.ds(r0, c)
        g_all = la_scr[rows, :]
        g_hi = g_all.astype(BF16)
        g_lo = (g_all - g_hi.astype(F32)).astype(BF16)
        cs = jnp.dot(mst_ref[...], jnp.concatenate([g_hi, g_lo], axis=1), preferred_element_type=F32)
        cs = cs[:, :GLA_DK] + cs[:, GLA_DK:]
        for h in range(GLA_HEADS):
            ksl = slice(h * GLA_HEAD_DK, (h + 1) * GLA_HEAD_DK)
            vsl = slice(h * GLA_HEAD_DV, (h + 1) * GLA_HEAD_DV)
            q = q_ref[rows, ksl] * scale
            k = k_ref[rows, ksl]
            v = v_ref[rows, vsl].astype(BF16)
            b = cs[0:c, ksl]
            e_last = cs[c:2 * c, ksl]
            st = s_scr[h]
            o = lax.dot_general((q * jnp.exp(b)).astype(BF16), st.astype(BF16), _NT,
                                preferred_element_type=F32)
            att = lmask_ref[nlev] * lax.dot_general(q.astype(BF16), k.astype(BF16), _NT,
                                                    preferred_element_type=F32)
            for l in range(nlev):
                e = cs[(2 + l) * c:(3 + l) * c, ksl]
                ql = (q * jnp.exp(jnp.minimum(b - e, 0.0))).astype(BF16)
                kl = (k * jnp.exp(jnp.minimum(e - b, 0.0))).astype(BF16)
                att = att + lmask_ref[l] * lax.dot_general(ql, kl, _NT, preferred_element_type=F32)
            o = o + jnp.dot(att.astype(BF16), v, preferred_element_type=F32)
            k_st = (k * jnp.exp(e_last - b)).astype(BF16)
            s_scr[h] = st * jnp.exp(e_last[0:1, :]) + lax.dot_general(v, k_st, _TN,
                                                                      preferred_element_type=F32)
            o_ref[rows, vsl] = _rms(o, gn_ref[...])
        return carry

    lax.fori_loop(0, nc, chunk, 0)

    @pl.when(step == pl.num_programs(1) - 1)
    def _():
        for h in range(GLA_HEADS):
            sout_ref[h] = s_scr[h].T


def _gla(q_arr, k_arr, v_arr, misc_arr, cols, wgu_pad, b_gate, gla_norm_g, s0, c, tile, valid_rows):
    nb, l = q_arr.shape[0], q_arr.shape[1]
    mst, lmask, nlev = _gla_consts(c)
    const2 = lambda b, s: (0, 0)
    body = functools.partial(_gla_body, c=c, nc=tile // c, nlev=nlev, valid_rows=valid_rows)
    return pl.pallas_call(
        body,
        grid=(nb, l // tile),
        in_specs=[pl.BlockSpec((None, tile, GLA_DK), lambda b, s: (b, s, cols[0])),
                  pl.BlockSpec((None, tile, GLA_DK), lambda b, s: (b, s, cols[1])),
                  pl.BlockSpec((None, tile, GLA_DV), lambda b, s: (b, s, cols[2])),
                  pl.BlockSpec((None, tile, LANES), lambda b, s: (b, s, 0)),
                  pl.BlockSpec((LANES, GLA_DK), const2),
                  pl.BlockSpec((1, GLA_DK), const2),
                  pl.BlockSpec((1, GLA_HEAD_DV), const2),
                  pl.BlockSpec(mst.shape, const2),
                  pl.BlockSpec(lmask.shape, lambda b, s: (0, 0, 0)),
                  pl.BlockSpec((None, GLA_HEADS, GLA_HEAD_DK, GLA_HEAD_DV), lambda b, s: (b, 0, 0, 0))],
        out_specs=[pl.BlockSpec((None, tile, GLA_DV), lambda b, s: (b, s, 0)),
                   pl.BlockSpec((None, GLA_HEADS, GLA_HEAD_DK, GLA_HEAD_DV), lambda b, s: (b, 0, 0, 0))],
        out_shape=[jax.ShapeDtypeStruct((nb, l, GLA_DV), F32),
                   jax.ShapeDtypeStruct((nb, GLA_HEADS, GLA_HEAD_DK, GLA_HEAD_DV), F32)],
        scratch_shapes=[pltpu.VMEM((GLA_HEADS, GLA_HEAD_DV, GLA_HEAD_DK), F32),
                        pltpu.VMEM((tile, GLA_DK), F32)],
        compiler_params=pltpu.CompilerParams(dimension_semantics=("arbitrary", "arbitrary"),
                                             vmem_limit_bytes=VMEM_LIMIT),
        name="gla",
    )(q_arr, k_arr, v_arr, misc_arr, wgu_pad, b_gate.reshape(1, -1), gla_norm_g.reshape(1, -1),
      mst, lmask, s0)


def _score_stats_init(w):
    inf = jnp.full((SUBLANES, w), jnp.inf, F32)
    zero = jnp.zeros((SUBLANES, w), I32)
    return -inf, inf, zero, zero


def _score_stats_update(carry, blk):
    mx, mn, c_pos, c_nn = carry
    b3 = blk.reshape(blk.shape[0] // SUBLANES, SUBLANES, blk.shape[1])
    return (jnp.maximum(mx, jnp.max(b3, axis=0)),
            jnp.minimum(mn, jnp.min(jnp.where(b3 == -jnp.inf, jnp.inf, b3), axis=0)),
            c_pos + jnp.sum((b3 > 0.0).astype(I32), axis=0),
            c_nn + jnp.sum((b3 >= 0.0).astype(I32), axis=0))


def _select_threshold(sc_ref, tri_ref, nt, tr, topk, small, lane_ok, stats=None):
    w = sc_ref.shape[1]
    inf = jnp.float32(jnp.inf)

    def over_tiles(fn, init):
        def body(i, carry):
            r0 = pl.multiple_of(i * tr, tr)
            return fn(carry, sc_ref[pl.ds(r0, tr), :], r0)
        return lax.fori_loop(0, nt, body, init)

    fold = lambda x: x.reshape(tr // SUBLANES, SUBLANES, w)
    zeros8 = jnp.zeros((SUBLANES, w), I32)
    pinf8 = jnp.full((SUBLANES, w), inf, F32)

    def count(pred):
        acc = over_tiles(lambda a, blk, r0: a + jnp.sum(fold(pred(blk, r0).astype(I32)), axis=0), zeros8)
        return jnp.sum(acc, axis=0, keepdims=True)

    def min_where(pred):
        acc = over_tiles(lambda a, blk, r0: jnp.minimum(
            a, jnp.min(fold(jnp.where(pred(blk, r0), blk, inf)), axis=0)), pinf8)
        return jnp.min(acc, axis=0, keepdims=True)

    if stats is None:
        stats = over_tiles(lambda carry, blk, r0: _score_stats_update(carry, blk), _score_stats_init(w))
    mx8, mn8, cp8, cn8 = stats
    c_pos = jnp.sum(cp8, axis=0, keepdims=True)
    c_nn = jnp.sum(cn8, axis=0, keepdims=True)
    hi = jnp.where(c_pos >= topk, jnp.max(mx8, axis=0, keepdims=True), 0.0)
    lo = jnp.where(c_nn >= topk, 0.0, jnp.min(mn8, axis=0, keepdims=True))

    live = lane_ok & jnp.logical_not(small)

    def bisect(_, carry):
        lo, hi = carry
        mid = 0.5 * lo + 0.5 * hi
        ge = count(lambda blk, r0: blk >= mid) >= topk
        return jnp.where(ge, mid, lo), jnp.where(ge, hi, mid)

    lo, hi = lax.fori_loop(0, BISECT_STEPS, bisect, (lo, hi))
    v0 = min_where(lambda blk, r0: blk >= lo)

    def gt_next(v):
        def f(carry, blk, r0):
            cg, nx = carry
            gt = blk > v
            return (cg + jnp.sum(fold(gt.astype(I32)), axis=0),
                    jnp.minimum(nx, jnp.min(fold(jnp.where(gt, blk, inf)), axis=0)))
        cg8, nx8 = over_tiles(f, (zeros8, pinf8))
        return jnp.sum(cg8, axis=0, keepdims=True), jnp.min(nx8, axis=0, keepdims=True)

    def peel(state):
        v, _, _ = state
        cg, nx = gt_next(v)
        move = (cg >= topk) & live
        return jnp.where(move, nx, v), cg, jnp.max(move.astype(I32))

    v, cnt_gt, _ = lax.while_loop(lambda s: s[2] > 0, peel,
                                  (v0, jnp.zeros((1, w), I32), jnp.int32(1)))
    cnt_ge = count(lambda blk, r0: blk >= v)
    excess = (cnt_ge > topk) & live

    @pl.when(jnp.max(excess.astype(I32)) > 0)
    def _():
        need = (topk - cnt_gt).astype(F32)

        tb = tri_ref.shape[0]

        def drop_surplus(i, seen):
            r0 = pl.multiple_of(i * tr, tr)
            blk = sc_ref[pl.ds(r0, tr), :]
            eq = blk == v
            ones = jnp.where(eq, 1.0, 0.0).astype(BF16)
            ranks = [jnp.dot(tri_ref[...], ones[s * tb:(s + 1) * tb, :], preferred_element_type=F32)
                     for s in range(tr // tb)]
            for s in range(tr // tb):
                rows = slice(s * tb, (s + 1) * tb)
                rank = ranks[s] + seen
                sc_ref[pl.ds(r0 + s * tb, tb), :] = jnp.where(eq[rows] & (rank > need) & excess, -inf, blk[rows])
                seen = rank[tb - 1:tb, :]
            return seen

        lax.fori_loop(0, nt, drop_surplus, jnp.zeros((1, w), F32))

    return jnp.where(small, -inf, v)


def _dsa_prompt_body(relb_ref, btab_ref, tri_ref, ki_ref, qi_ref, wi_ref, kd_ref, vt_ref, qd_ref, o_ref,
                     sc_scr, tbl_scr, q2_scr, acc_scr, m_scr, o_scr, lg_scr, pr_scr, *,
                     topk, far_bucket):
    b = pl.program_id(0)
    j = pl.program_id(1)
    tk = DSA_TK
    hd = DSA_HEAD_DIM

    @pl.when((b == 0) & (j == 0))
    def _():
        q2_scr[...] = jnp.zeros(q2_scr.shape, BF16)

        def build(ci, carry):
            r0 = pl.multiple_of(ci * LANES, LANES)
            bt = btab_ref[pl.ds(r0, LANES), :]
            for h in range(DSA_HEADS):
                t = jnp.zeros(bt.shape, F32)
                for bk in range(REL_BUCKETS):
                    t = jnp.where(bt == bk, (relb_ref[bk, h] - relb_ref[far_bucket, h]) * LOG2E, t)
                tbl_scr[h, pl.ds(r0, LANES), :] = t
            return carry

        lax.fori_loop(0, BIAS_TAB_ROWS // LANES, build, 0)

    nt = j // (tk // QUERY_BLOCK) + 1
    qpos = j * QUERY_BLOCK + lax.broadcasted_iota(I32, (1, LANES), 1)
    for p in range(DSA_HEADS // 2):
        q2_scr[p, 0:hd, 0:LANES] = qd_ref[2 * p * hd:(2 * p + 1) * hd, :]
        q2_scr[p, hd:2 * hd, LANES:2 * LANES] = qd_ref[(2 * p + 1) * hd:(2 * p + 2) * hd, :]

    wi = wi_ref[...]

    def score_tile(i, carry):
        r0 = pl.multiple_of(i * tk, tk)
        s4 = jnp.dot(ki_ref[pl.ds(r0, tk), :], qi_ref[...], preferred_element_type=F32)
        sc = jnp.zeros((tk, LANES), F32)
        for h in range(IDX_HEADS):
            sc = sc + jnp.maximum(s4[:, h * LANES:(h + 1) * LANES], 0.0) * wi[h:h + 1, :]
        kpos = r0 + lax.broadcasted_iota(I32, (tk, LANES), 0)
        sc = jnp.where(kpos <= qpos, sc, -jnp.inf)
        sc_scr[pl.ds(r0, tk), :] = sc
        return _score_stats_update(carry, sc)

    stats = lax.fori_loop(0, nt, score_tile, _score_stats_init(LANES))

    small = (qpos + 1) < topk
    kstar = _select_threshold(sc_scr, tri_ref, nt, tk, topk, small, jnp.full((1, LANES), True), stats)

    m_scr[...] = jnp.full(m_scr.shape, NEG, F32)
    acc_scr[...] = jnp.zeros(acc_scr.shape, F32)
    vrows = acc_scr.shape[0] // DSA_HEADS

    def attend(i, near):
        r0 = pl.multiple_of(i * tk, tk)
        blk = sc_scr[pl.ds(r0, tk), :]
        if near:
            kpos = r0 + lax.broadcasted_iota(I32, (tk, LANES), 0)
            addm = jnp.where(blk >= kstar, jnp.where(kpos <= qpos, 0.0, NEG), NEG)
            off = pl.multiple_of(i * tk - j * QUERY_BLOCK + BIAS_TAB_OFF, LANES)
        else:
            addm = jnp.where(blk >= kstar, 0.0, NEG)
        tile_max = []
        for p in range(DSA_HEADS // 2):
            lg2 = jnp.dot(kd_ref[pl.ds(r0, tk), p * LANES:(p + 1) * LANES], q2_scr[p],
                          preferred_element_type=F32)
            for hh in range(2):
                h = 2 * p + hh
                lg = lg2[:, hh * LANES:(hh + 1) * LANES] + addm
                if near:
                    lg = lg + tbl_scr[h, pl.ds(off, tk), :]
                lg_scr[h] = lg
                tile_max.append(jnp.max(lg, axis=0, keepdims=True))
        alpha = []
        for h in range(DSA_HEADS):
            m_old = m_scr[h:h + 1, :]
            m_new = jnp.maximum(m_old, tile_max[h])
            m_scr[h:h + 1, :] = m_new
            alpha.append(jnp.exp2(m_old - m_new))
            pr_scr[h] = jnp.exp2((lg_scr[h] - m_new).astype(BF16))
        for h in range(DSA_HEADS):
            rows = slice(h * vrows, (h + 1) * vrows)
            acc_scr[rows, :] = alpha[h] * acc_scr[rows, :] + jnp.dot(
                vt_ref[rows, pl.ds(r0, tk)], pr_scr[h], preferred_element_type=F32)

    def step(near):
        def body(i, carry):
            attend(i, near)
            return carry
        return body

    n_far = jnp.maximum(nt - 2, 0)
    lax.fori_loop(0, n_far, step(False), 0)
    lax.fori_loop(n_far, nt, step(True), 0)

    for h in range(DSA_HEADS):
        o_scr[h * hd:(h + 1) * hd, :] = (acc_scr[h * vrows:h * vrows + hd, :]
                                         / acc_scr[h * vrows + hd:h * vrows + hd + 1, :])
    o_ref[...] = o_scr[...].T.astype(BF16)


def _tri_ones(n):
    return jnp.asarray(np.tril(np.ones((n, n), np.float32)), BF16)


def _bias_bucket_table():
    u = np.arange(BIAS_TAB_ROWS)[:, None]
    r = np.arange(LANES)[None, :]
    return jnp.asarray(_t5_bucket_np(r + BIAS_TAB_OFF - u), I32)


def _dsa_prompt(zA, kd, vd, misc, kin, rel_bias, nb, l):
    nq = l // QUERY_BLOCK
    topk = min(TOPK_MAX, l // 4)
    assert l % DSA_TK == 0
    far = _t5_bucket_np(np.arange(REL_MAX_DIST, max(l, REL_MAX_DIST + 1)))
    assert (far == far[0]).all()
    ki = kin.reshape(nb, l, IDX_DIM).astype(BF16)
    qi = misc[:, MISC_QI:MISC_QI + IDX_HEADS * IDX_DIM].reshape(nb, nq, QUERY_BLOCK, IDX_HEADS, IDX_DIM)
    qi = qi.transpose(0, 4, 1, 3, 2).reshape(nb, IDX_DIM, nq * IDX_HEADS * QUERY_BLOCK).astype(BF16)
    wi = misc[:, MISC_WI:MISC_WI + IDX_HEADS] * ((IDX_DIM ** -0.5) * (IDX_HEADS ** -0.5))
    wi = wi.reshape(nb, nq, QUERY_BLOCK, IDX_HEADS).transpose(0, 1, 3, 2)
    wi = jnp.pad(wi, ((0, 0), (0, 0), (0, SUBLANES - IDX_HEADS), (0, 0))).reshape(nb, nq * SUBLANES, QUERY_BLOCK)
    kd_bf = kd.reshape(nb, l, DSA_WIDTH).astype(BF16)
    vt = vd.reshape(nb, l, DSA_HEADS, DSA_HEAD_DIM).transpose(0, 2, 3, 1)
    vt = jnp.concatenate([vt, jnp.ones((nb, DSA_HEADS, 1, l), vt.dtype),
                          jnp.zeros((nb, DSA_HEADS, V_ROWS - DSA_HEAD_DIM - 1, l), vt.dtype)], axis=2)
    vt = vt.reshape(nb, DSA_HEADS * V_ROWS, l).astype(BF16)
    qd = (zA[:, ZA_QD:ZA_QD + DSA_WIDTH] * ((DSA_HEAD_DIM ** -0.5) * LOG2E)).reshape(nb, l, DSA_WIDTH)
    qd = qd.transpose(0, 2, 1).astype(BF16)
    body = functools.partial(_dsa_prompt_body, topk=topk, far_bucket=int(far[0]))
    whole = lambda b, j: (b, 0, 0)
    return pl.pallas_call(
        body,
        grid=(nb, nq),
        in_specs=[pl.BlockSpec(memory_space=pltpu.SMEM),
                  pl.BlockSpec((BIAS_TAB_ROWS, LANES), lambda b, j: (0, 0), pipeline_mode=pl.Buffered(1)),
                  pl.BlockSpec((LANES, LANES), lambda b, j: (0, 0), pipeline_mode=pl.Buffered(1)),
                  pl.BlockSpec((None, l, IDX_DIM), whole, pipeline_mode=pl.Buffered(1)),
                  pl.BlockSpec((None, IDX_DIM, IDX_HEADS * QUERY_BLOCK), lambda b, j: (b, 0, j)),
                  pl.BlockSpec((None, SUBLANES, QUERY_BLOCK), lambda b, j: (b, j, 0)),
                  pl.BlockSpec((None, l, DSA_WIDTH), whole, pipeline_mode=pl.Buffered(1)),
                  pl.BlockSpec((None, DSA_HEADS * V_ROWS, l), whole, pipeline_mode=pl.Buffered(1)),
                  pl.BlockSpec((None, DSA_WIDTH, QUERY_BLOCK), lambda b, j: (b, 0, j))],
        out_specs=pl.BlockSpec((None, QUERY_BLOCK, DSA_WIDTH), lambda b, j: (b, j, 0)),
        out_shape=jax.ShapeDtypeStruct((nb, l, DSA_WIDTH), BF16),
        scratch_shapes=[pltpu.VMEM((l, LANES), F32),
                        pltpu.VMEM((DSA_HEADS, BIAS_TAB_ROWS, LANES), F32),
                        pltpu.VMEM((DSA_HEADS // 2, LANES, 2 * LANES), BF16),
                        pltpu.VMEM((DSA_HEADS * V_ROWS, LANES), F32),
                        pltpu.VMEM((DSA_HEADS, LANES), F32),
                        pltpu.VMEM((DSA_WIDTH, LANES), F32),
                        pltpu.VMEM((DSA_HEADS, DSA_TK, LANES), F32),
                        pltpu.VMEM((DSA_HEADS, DSA_TK, LANES), BF16)],
        compiler_params=pltpu.CompilerParams(dimension_semantics=("arbitrary", "arbitrary"),
                                             vmem_limit_bytes=VMEM_LIMIT),
        name="dsa_prompt",
    )(rel_bias, _bias_bucket_table(), _tri_ones(LANES), ki, qi, wi, kd_bf, vt, qd)


SEQ_GROUP = LANES // SUBLANES


def _dsa_sample_select_body(pt_ref, *refs, n_pages, t_new, topk):
    page_refs = refs[:n_pages]
    knew_ref, qi_ref, wi_ref, tri_ref, mask_ref, sc_scr = refs[n_pages:]
    g = pl.program_id(1)
    past = n_pages * PAGE_SIZE
    lane = lax.broadcasted_iota(I32, (1, LANES), 1)
    in_group = (lane // SUBLANES) == g

    def scores(k_rows):
        kb = k_rows.astype(BF16)
        sc = jnp.zeros((k_rows.shape[0], LANES), F32)
        for h in range(IDX_HEADS):
            rhs = jnp.where(in_group, qi_ref[h], jnp.zeros((), BF16))
            s = jnp.dot(kb, rhs, preferred_element_type=F32)
            sc = sc + jnp.maximum(s, 0.0) * wi_ref[h:h + 1, :]
        return jnp.where(in_group, sc, 0.0)

    for p in range(n_pages + 1):
        rows = slice(p * PAGE_SIZE, (p + 1) * PAGE_SIZE)
        sc = scores(page_refs[p][...] if p < n_pages else knew_ref[...])

        @pl.when(g == 0)
        def _():
            sc_scr[rows, :] = sc

        @pl.when(g > 0)
        def _():
            sc_scr[rows, :] = sc_scr[rows, :] + sc

    @pl.when(g == SEQ_GROUP - 1)
    def _():
        q_of_lane = lane % SUBLANES
        rows = slice(past, past + PAGE_SIZE)
        cpos = lax.broadcasted_iota(I32, (PAGE_SIZE, LANES), 0)
        sc_scr[rows, :] = jnp.where(cpos <= q_of_lane, sc_scr[rows, :], -jnp.inf)
        lane_ok = q_of_lane < t_new
        small = (past + q_of_lane + 1) < topk
        kstar = _select_threshold(sc_scr, tri_ref, n_pages + 1, PAGE_SIZE, topk, small, lane_ok)
        for p in range(n_pages + 1):
            rows = slice(p * PAGE_SIZE, (p + 1) * PAGE_SIZE)
            mask_ref[rows, :] = jnp.where(sc_scr[rows, :] >= kstar, 1.0, 0.0).astype(BF16)


def _dsa_sample_select(page_table, cache_kidx, kin_new, qi_t, wi_t, t_new, topk):
    db, n_pages = page_table.shape
    ng = db // SEQ_GROUP
    rows = (n_pages + 1) * PAGE_SIZE
    seq = lambda gi, g, pt: (gi * SEQ_GROUP + g, 0, 0)
    page_specs = [pl.BlockSpec((None, PAGE_SIZE, IDX_DIM),
                               functools.partial(lambda gi, g, pt, p: (pt[gi * SEQ_GROUP + g, p], 0, 0), p=p))
                  for p in range(n_pages)]
    body = functools.partial(_dsa_sample_select_body, n_pages=n_pages, t_new=t_new, topk=topk)
    grid_spec = pltpu.PrefetchScalarGridSpec(
        num_scalar_prefetch=1,
        grid=(ng, SEQ_GROUP),
        in_specs=page_specs + [pl.BlockSpec((None, PAGE_SIZE, IDX_DIM), seq),
                               pl.BlockSpec((None, IDX_HEADS, IDX_DIM, LANES), lambda gi, g, pt: (gi * SEQ_GROUP + g, 0, 0, 0)),
                               pl.BlockSpec((None, SUBLANES, LANES), seq),
                               pl.BlockSpec((PAGE_SIZE, PAGE_SIZE), lambda gi, g, pt: (0, 0))],
        out_specs=pl.BlockSpec((None, rows, LANES), lambda gi, g, pt: (gi, 0, 0)),
        scratch_shapes=[pltpu.VMEM((rows, LANES), F32)],
    )
    return pl.pallas_call(
        body, grid_spec=grid_spec,
        out_shape=jax.ShapeDtypeStruct((ng, rows, LANES), BF16),
        compiler_params=pltpu.CompilerParams(dimension_semantics=("arbitrary", "arbitrary"),
                                             vmem_limit_bytes=VMEM_LIMIT),
        name="dsa_sample_select",
    )(page_table, *([cache_kidx] * n_pages), kin_new, qi_t, wi_t, _tri_ones(PAGE_SIZE))


def _dsa_sample_attend_body(pt_ref, *refs, n_pages):
    k_refs = refs[:n_pages]
    v_refs = refs[n_pages:2 * n_pages]
    (knew_ref, vnew_ref, mask_ref, qbig_ref, btab_ref, rb_ref, o_ref, lg_scr, tbl_scr) = refs[2 * n_pages:]
    b = pl.program_id(0)
    hq = DSA_HEADS * SUBLANES

    @pl.when(b == 0)
    def _():
        bt = btab_ref[...]
        t = jnp.zeros(bt.shape, F32)
        for bk in range(REL_BUCKETS):
            t = jnp.where(bt == bk, rb_ref[bk:bk + 1, :], t)
        tbl_scr[...] = t

    g = b % SEQ_GROUP
    r_i = lax.broadcasted_iota(I32, (LANES, hq), 0)
    c_i = lax.broadcasted_iota(I32, (LANES, hq), 1)
    place = jnp.where(((r_i // SUBLANES) == g) & ((r_i % SUBLANES) == (c_i % SUBLANES)), 1.0, 0.0).astype(BF16)
    far_bias = rb_ref[REL_BUCKETS - 1:REL_BUCKETS, :]

    m = jnp.full((1, hq), NEG, F32)
    for p in range(n_pages + 1):
        rows = slice(p * PAGE_SIZE, (p + 1) * PAGE_SIZE)
        k_rows = k_refs[p][...] if p < n_pages else knew_ref[...]
        lg = jnp.dot(k_rows.astype(BF16), qbig_ref[...], preferred_element_type=F32)
        if p >= n_pages - 1:
            lg = lg + tbl_scr[(p - n_pages + 1) * PAGE_SIZE:(p - n_pages + 2) * PAGE_SIZE, :]
        else:
            lg = lg + far_bias
        sel = jnp.dot(mask_ref[rows, :], place, preferred_element_type=F32)
        lg = lg + jnp.where(sel > 0.5, 0.0, NEG)
        lg_scr[rows, :] = lg
        m = jnp.maximum(m, jnp.max(lg, axis=0, keepdims=True))

    acc = jnp.zeros((hq, DSA_WIDTH), F32)
    lsum = jnp.zeros((hq, LANES), F32)
    ones = jnp.ones((PAGE_SIZE, LANES), BF16)
    for p in range(n_pages + 1):
        rows = slice(p * PAGE_SIZE, (p + 1) * PAGE_SIZE)
        v_rows = v_refs[p][...] if p < n_pages else vnew_ref[...]
        pr = jnp.exp(lg_scr[rows, :] - m).astype(BF16)
        acc = acc + lax.dot_general(pr, v_rows.astype(BF16), _TN, preferred_element_type=F32)
        lsum = lsum + lax.dot_general(pr, ones, _TN, preferred_element_type=F32)
    acc = acc / lsum[:, 0:1]
    head_of_lane = lax.broadcasted_iota(I32, (SUBLANES, DSA_WIDTH), 1) // DSA_HEAD_DIM
    out = jnp.zeros((SUBLANES, DSA_WIDTH), F32)
    for h in range(DSA_HEADS):
        out = jnp.where(head_of_lane == h, acc[h * SUBLANES:(h + 1) * SUBLANES, :], out)
    o_ref[...] = out


def _dsa_sample_attend(page_table, cache_k, cache_v, k_new, v_new, mask, qbig, rel_bias, t_new):
    db, n_pages = page_table.shape
    hq = DSA_HEADS * SUBLANES
    rows = (n_pages + 1) * PAGE_SIZE
    seq = lambda b, pt: (b, 0, 0)
    page = lambda p: functools.partial(lambda b, pt, p: (pt[b, p], 0, 0), p=p)
    kv_spec = lambda p: pl.BlockSpec((None, PAGE_SIZE, DSA_WIDTH), page(p))
    u = np.arange(2 * PAGE_SIZE)[:, None]
    q = (np.arange(hq) % SUBLANES)[None, :]
    btab = jnp.asarray(_t5_bucket_np(PAGE_SIZE + q - u), I32)
    rb = jnp.repeat(rel_bias, SUBLANES, axis=1)
    grid_spec = pltpu.PrefetchScalarGridSpec(
        num_scalar_prefetch=1,
        grid=(db,),
        in_specs=[kv_spec(p) for p in range(n_pages)] + [kv_spec(p) for p in range(n_pages)] + [
            pl.BlockSpec((None, PAGE_SIZE, DSA_WIDTH), seq),
            pl.BlockSpec((None, PAGE_SIZE, DSA_WIDTH), seq),
            pl.BlockSpec((None, rows, LANES), lambda b, pt: (b // SEQ_GROUP, 0, 0)),
            pl.BlockSpec((None, DSA_WIDTH, hq), seq),
            pl.BlockSpec(btab.shape, lambda b, pt: (0, 0)),
            pl.BlockSpec(rb.shape, lambda b, pt: (0, 0))],
        out_specs=pl.BlockSpec((None, SUBLANES, DSA_WIDTH), seq),
        scratch_shapes=[pltpu.VMEM((rows, hq), F32), pltpu.VMEM((2 * PAGE_SIZE, hq), F32)],
    )
    return pl.pallas_call(
        functools.partial(_dsa_sample_attend_body, n_pages=n_pages), grid_spec=grid_spec,
        out_shape=jax.ShapeDtypeStruct((db, SUBLANES, DSA_WIDTH), F32),
        compiler_params=pltpu.CompilerParams(dimension_semantics=("arbitrary",),
                                             vmem_limit_bytes=VMEM_LIMIT),
        name="dsa_sample_attend",
    )(page_table, *([cache_k] * n_pages), *([cache_v] * n_pages), k_new, v_new, mask, qbig, btab, rb)


def _dsa_sample(zA, kd, vd, misc, kin, rel_bias, cache_k, cache_v, cache_kidx, page_table, db, t_new):
    n_pages = page_table.shape[1]
    past = n_pages * PAGE_SIZE
    topk = min(TOPK_MAX, (past + t_new) // 4)
    assert db % SEQ_GROUP == 0 and t_new <= SUBLANES
    pad_rows = lambda a: jnp.pad(a, ((0, 0), (0, PAGE_SIZE - t_new), (0, 0)))
    kin_new = pad_rows(kin.reshape(db, t_new, IDX_DIM))
    k_new = pad_rows(kd.reshape(db, t_new, DSA_WIDTH))
    v_new = pad_rows(vd.reshape(db, t_new, DSA_WIDTH))
    qi = misc[:, MISC_QI:MISC_QI + IDX_HEADS * IDX_DIM].reshape(db, t_new, IDX_HEADS, IDX_DIM)
    qi = jnp.pad(qi.transpose(0, 2, 3, 1), ((0, 0), (0, 0), (0, 0), (0, SUBLANES - t_new)))
    qi_t = jnp.tile(qi, (1, 1, 1, SEQ_GROUP)).astype(BF16)
    wi = misc[:, MISC_WI:MISC_WI + IDX_HEADS] * ((IDX_DIM ** -0.5) * (IDX_HEADS ** -0.5))
    wi = jnp.pad(wi.reshape(db, t_new, IDX_HEADS).transpose(0, 2, 1),
                 ((0, 0), (0, SUBLANES - IDX_HEADS), (0, SUBLANES - t_new)))
    wi_t = jnp.tile(wi, (1, 1, SEQ_GROUP))
    mask = _dsa_sample_select(page_table, cache_kidx, kin_new, qi_t, wi_t, t_new, topk)
    qd = (zA[:, ZA_QD:ZA_QD + DSA_WIDTH] * (DSA_HEAD_DIM ** -0.5)).reshape(db, t_new, DSA_HEADS, DSA_HEAD_DIM)
    qd = jnp.pad(qd.transpose(0, 2, 3, 1), ((0, 0), (0, 0), (0, 0), (0, SUBLANES - t_new)))
    eye = jnp.eye(DSA_HEADS, dtype=qd.dtype)
    qbig = (qd[:, :, :, None, :] * eye[None, :, None, :, None]).reshape(db, DSA_WIDTH, DSA_HEADS * SUBLANES)
    o = _dsa_sample_attend(page_table, cache_k.reshape(-1, PAGE_SIZE, DSA_WIDTH),
                           cache_v.reshape(-1, PAGE_SIZE, DSA_WIDTH), k_new, v_new, mask,
                           qbig.astype(BF16), rel_bias, t_new)
    return o[:, :t_new, :].reshape(db * t_new, DSA_WIDTH).astype(BF16)


def _post_body(x_ref, og_ref, rg_ref, gg_ref, gd_ref, od_ref, wg_ref, wd_ref, wo_ref, gf_ref,
               x1_ref, hf_ref):
    rg = rg_ref[...]
    a = (og_ref[...] * (rg * _sigmoid(rg))).astype(BF16)
    y_g = jnp.dot(a, wg_ref[...], preferred_element_type=F32)
    y_d = jnp.dot(od_ref[...], wd_ref[...], preferred_element_type=F32)
    mix = (_sigmoid(gg_ref[...]) * y_g + _sigmoid(gd_ref[...]) * y_d).astype(BF16)
    x1 = x_ref[...] + jnp.dot(mix, wo_ref[...], preferred_element_type=F32)
    x1_ref[...] = x1
    hf_ref[...] = _rms(x1, gf_ref[...]).astype(BF16)


def _post(x2d, og, zA, od, w_gla, w_dsa, w_o, g_ffn):
    n = x2d.shape[0]
    tm = min(n, 512)
    row = lambda i: (i, 0)
    const = lambda i: (0, 0)
    return pl.pallas_call(
        _post_body,
        grid=(n // tm,),
        in_specs=[pl.BlockSpec((tm, D_MODEL), row),
                  pl.BlockSpec((tm, GLA_DV), row),
                  pl.BlockSpec((tm, GLA_DV), lambda i: (i, COL_RG)),
                  pl.BlockSpec((tm, D_MODEL), lambda i: (i, COL_GG)),
                  pl.BlockSpec((tm, D_MODEL), lambda i: (i, COL_GD)),
                  pl.BlockSpec((tm, DSA_WIDTH), row),
                  pl.BlockSpec((GLA_DV, D_MODEL), const),
                  pl.BlockSpec((DSA_WIDTH, D_MODEL), const),
                  pl.BlockSpec((D_MODEL, D_MODEL), const),
                  pl.BlockSpec((1, D_MODEL), const)],
        out_specs=[pl.BlockSpec((tm, D_MODEL), row), pl.BlockSpec((tm, D_MODEL), row)],
        out_shape=[jax.ShapeDtypeStruct((n, D_MODEL), F32), jax.ShapeDtypeStruct((n, D_MODEL), BF16)],
        compiler_params=pltpu.CompilerParams(dimension_semantics=("arbitrary",),
                                             vmem_limit_bytes=VMEM_LIMIT),
        name="post_mix",
    )(x2d, og, zA, zA, zA, od, w_gla, w_dsa, w_o, g_ffn.reshape(1, -1))


FFN_TILE = 256


def _ffn_body(hf_ref, x1_ref, wg_ref, wu_ref, wd_ref, gfin_ref, y_ref, acc_scr):
    k = pl.program_id(1)
    hf = hf_ref[...]
    gate = jnp.dot(hf, wg_ref[...], preferred_element_type=F32)
    up = jnp.dot(hf, wu_ref[...], preferred_element_type=F32)
    part = jnp.dot((gate * _sigmoid(gate) * up).astype(BF16), wd_ref[...], preferred_element_type=F32)

    @pl.when(k == 0)
    def _():
        acc_scr[...] = x1_ref[...] + part

    @pl.when(k > 0)
    def _():
        acc_scr[...] = acc_scr[...] + part

    @pl.when(k == pl.num_programs(1) - 1)
    def _():
        y_ref[...] = _rms(acc_scr[...], gfin_ref[...])


def _ffn(hf, x1, w_gate, w_up, w_down, g_final):
    n = hf.shape[0]
    d_ff = w_gate.shape[1]
    tm = min(n, 1024)
    row = lambda i, k: (i, 0)
    return pl.pallas_call(
        _ffn_body,
        grid=(n // tm, d_ff // FFN_TILE),
        in_specs=[pl.BlockSpec((tm, D_MODEL), row),
                  pl.BlockSpec((tm, D_MODEL), row),
                  pl.BlockSpec((D_MODEL, FFN_TILE), lambda i, k: (0, k)),
                  pl.BlockSpec((D_MODEL, FFN_TILE), lambda i, k: (0, k)),
                  pl.BlockSpec((FFN_TILE, D_MODEL), lambda i, k: (k, 0)),
                  pl.BlockSpec((1, D_MODEL), lambda i, k: (0, 0))],
        out_specs=pl.BlockSpec((tm, D_MODEL), row),
        out_shape=jax.ShapeDtypeStruct((n, D_MODEL), F32),
        scratch_shapes=[pltpu.VMEM((tm, D_MODEL), F32)],
        compiler_params=pltpu.CompilerParams(dimension_semantics=("arbitrary", "arbitrary"),
                                             vmem_limit_bytes=VMEM_LIMIT),
        name="ffn",
    )(hf, x1, w_gate, w_up, w_down, g_final.reshape(1, -1))


def _gate_weight_pad(w_gate_up):
    pad = jnp.zeros((LANES, GLA_DK), w_gate_up.dtype)
    return pad.at[MISC_ALOW:MISC_ALOW + GLA_GATE_RANK].set(w_gate_up).astype(BF16)


def kernel(x_prompt, x_sample, cache_k, cache_v, cache_kidx, state_gla, page_table, g_mix, w_in,
           w_gate_up, b_gate, gla_norm_g, w_gla_branch, idx_k_g, idx_k_b, w_dsa_branch, w_o, g_ffn,
           w_ffn_gate, w_ffn_up, w_ffn_down, rel_bias, g_final):
    depth = w_in.shape[0]
    assert depth == 1, "the final RMSNorm is fused into the FFN kernel of the single layer"
    nb, l, _ = x_prompt.shape
    db, t_new, _ = x_sample.shape
    layer = 0
    w_cat = _pack_w_in(w_in[layer])
    wgu_pad = _gate_weight_pad(w_gate_up[layer])
    w_gla = w_gla_branch[layer].astype(BF16)
    w_dsa = w_dsa_branch[layer].astype(BF16)
    w_out = w_o[layer].astype(BF16)
    w_fg, w_fu, w_fd = (w.astype(BF16) for w in (w_ffn_gate[layer], w_ffn_up[layer], w_ffn_down[layer]))

    xp = x_prompt.reshape(nb * l, D_MODEL)
    zA, kd, vd, misc, kin = _in_proj(xp, g_mix[layer], w_cat, idx_k_g[layer], idx_k_b[layer])
    zA3 = zA.reshape(nb, l, ZA_WIDTH)
    s0 = jnp.zeros((nb, GLA_HEADS, GLA_HEAD_DK, GLA_HEAD_DV), state_gla.dtype)
    og, s_p = _gla(zA3, zA3, zA3, misc.reshape(nb, l, PROJ_TILE), (COL_QG, COL_KG, COL_VG), wgu_pad,
                   b_gate[layer], gla_norm_g[layer], s0, GLA_CHUNK, GLA_TILE, GLA_TILE)
    od = _dsa_prompt(zA, kd, vd, misc, kin, rel_bias, nb, l)
    x1, hf = _post(xp, og.reshape(nb * l, GLA_DV), zA, od.reshape(nb * l, DSA_WIDTH), w_gla, w_dsa,
                   w_out, g_ffn[layer])
    y_p = _ffn(hf, x1, w_fg, w_fu, w_fd, g_final).reshape(nb, l, D_MODEL)

    xs = x_sample.reshape(db * t_new, D_MODEL)
    zA_s, kd_s, vd_s, misc_s, kin_s = _in_proj(xs, g_mix[layer], w_cat, idx_k_g[layer], idx_k_b[layer])
    pad_t = lambda a: jnp.pad(a.reshape(db, t_new, -1), ((0, 0), (0, SAMPLE_CHUNK - t_new), (0, 0)))
    og_s, s_s = _gla(pad_t(zA_s[:, 0:GLA_DK]), pad_t(zA_s[:, GLA_DK:2 * GLA_DK]),
                     pad_t(zA_s[:, 2 * GLA_DK:2 * GLA_DK + GLA_DV]), pad_t(misc_s[:, 0:LANES]), (0, 0, 0),
                     wgu_pad, b_gate[layer], gla_norm_g[layer], state_gla[layer], SAMPLE_CHUNK, SAMPLE_CHUNK, t_new)
    og_s = og_s[:, :t_new, :].reshape(db * t_new, GLA_DV)
    od_s = _dsa_sample(zA_s, kd_s, vd_s, misc_s, kin_s, rel_bias, cache_k[layer], cache_v[layer],
                       cache_kidx[layer], page_table, db, t_new)
    x1_s, hf_s = _post(xs, og_s, zA_s, od_s, w_gla, w_dsa, w_out, g_ffn[layer])
    y_s = _ffn(hf_s, x1_s, w_fg, w_fu, w_fd, g_final).reshape(db, t_new, D_MODEL)

    heads = lambda a, n, t: a.reshape(1, n, t, DSA_HEADS, DSA_HEAD_DIM)
    return (y_p, y_s,
            heads(kd, nb, l), heads(vd, nb, l), kin.reshape(1, nb, l, IDX_DIM), s_p[None],
            heads(kd_s, db, t_new), heads(vd_s, db, t_new), kin_s.reshape(1, db, t_new, IDX_DIM), s_s[None])
```

```python
import functools
import math

import numpy as np
import jax
import jax.numpy as jnp
from jax import lax
from jax.experimental import pallas as pl
from jax.experimental.pallas import tpu as pltpu

F32, BF16, I32 = jnp.float32, jnp.bfloat16, jnp.int32

D_MODEL = 1024
GLA_HEADS = 4
GLA_HEAD_DK = 128
GLA_HEAD_DV = 256
GLA_DK = GLA_HEADS * GLA_HEAD_DK
GLA_DV = GLA_HEADS * GLA_HEAD_DV
GLA_GATE_RANK = 16
GLA_GATE_TAU = 16.0
DSA_HEADS = 8
DSA_HEAD_DIM = 64
DSA_WIDTH = DSA_HEADS * DSA_HEAD_DIM
IDX_HEADS = 4
IDX_DIM = 64
TOPK_MAX = 256
QUERY_BLOCK = 128
PAGE_SIZE = 128
REL_BUCKETS = 32
REL_MAX_DIST = 128
RMS_EPS = 1e-6
LN_EPS = 1e-6
SPLIT_SIZES = (GLA_DK, GLA_DK, GLA_DV, GLA_DV, GLA_GATE_RANK, DSA_WIDTH, DSA_WIDTH, DSA_WIDTH,
               IDX_HEADS * IDX_DIM, IDX_DIM, IDX_HEADS, D_MODEL, D_MODEL)

LANES = 128
SUBLANES = 8
VMEM_LIMIT = 56 * 1024 * 1024

PROJ_TILE = 512
ZA_TILES = 11
ZA_WIDTH = ZA_TILES * PROJ_TILE
COL_QG, COL_KG = 0, 1
COL_VG, COL_RG, COL_GG, COL_GD = 1, 2, 3, 4
ZA_QD = 10 * PROJ_TILE
MISC_ALOW, MISC_WI, MISC_QI = 64, 80, 128

GLA_CHUNK = 64
GLA_TILE = 256
SAMPLE_CHUNK = 16
SAMPLE_GLA_SEQS = 4
DSA_TK = 512
BIAS_TAB_ROWS = 1408
BIAS_TAB_OFF = 896
V_ROWS = 80
NEG = -1e30
LOG2E = 1.4426950408889634
BISECT_STEPS = 15


def _sigmoid(x):
    return 1.0 / (1.0 + jnp.exp(-x))


def _rms(x, g):
    return x * lax.rsqrt(jnp.mean(x * x, axis=-1, keepdims=True) + RMS_EPS) * g


def _t5_bucket_np(n):
    n = np.maximum(np.asarray(n, np.int64), 0)
    max_exact = REL_BUCKETS // 2
    large = max_exact + (np.log(np.maximum(n, 1).astype(np.float32) / np.float32(max_exact))
                         / np.float32(math.log(REL_MAX_DIST / max_exact))
                         * np.float32(REL_BUCKETS - max_exact)).astype(np.int32)
    large = np.minimum(large, REL_BUCKETS - 1)
    return np.where(n < max_exact, n, large).astype(np.int32)


def _inproj_body(x_ref, g_ref, w_ref, ikg_ref, ikb_ref, za_ref, kd_ref, vd_ref, misc_ref, kin_ref, h_scr):
    j = pl.program_id(1)

    @pl.when(j == 0)
    def _():
        h_scr[...] = _rms(x_ref[...], g_ref[...]).astype(BF16)

    res = jnp.dot(h_scr[...], w_ref[...], preferred_element_type=F32)

    @pl.when(j < ZA_TILES)
    def _():
        za_ref[...] = res

    @pl.when(j == ZA_TILES)
    def _():
        kd_ref[...] = res

    @pl.when(j == ZA_TILES + 1)
    def _():
        vd_ref[...] = res

    @pl.when(j == ZA_TILES + 2)
    def _():
        misc_ref[...] = res
        ki = res[:, 0:IDX_DIM]
        mu = jnp.mean(ki, axis=-1, keepdims=True)
        var = jnp.mean(jnp.square(ki - mu), axis=-1, keepdims=True)
        kin_ref[...] = (ki - mu) * lax.rsqrt(var + LN_EPS) * ikg_ref[...] + ikb_ref[...]


def _pack_w_in(w_in):
    pts = np.cumsum((0,) + SPLIT_SIZES)
    seg = [w_in[:, int(pts[i]):int(pts[i + 1])] for i in range(len(SPLIT_SIZES))]
    q_g, k_g, v_g, r_g, a_low, q_d, k_d, v_d, q_i, k_i, w_i, gate_g, gate_d = seg
    z = lambda n: jnp.zeros((w_in.shape[0], n), w_in.dtype)
    misc = jnp.concatenate([k_i, a_low, w_i, z(LANES - MISC_WI - IDX_HEADS), q_i,
                            z(PROJ_TILE - MISC_QI - IDX_HEADS * IDX_DIM)], axis=1)
    return jnp.concatenate([q_g, k_g, v_g, r_g, gate_g, gate_d, q_d, k_d, v_d, misc], axis=1).astype(BF16)


def _in_proj(x2d, g_mix, w_cat, idx_k_g, idx_k_b):
    n = x2d.shape[0]
    tm = min(n, 1024)
    nj = w_cat.shape[1] // PROJ_TILE
    row = lambda i, j: (i, 0)
    return pl.pallas_call(
        _inproj_body,
        grid=(n // tm, nj),
        in_specs=[pl.BlockSpec((tm, D_MODEL), row),
                  pl.BlockSpec((1, D_MODEL), lambda i, j: (0, 0)),
                  pl.BlockSpec((D_MODEL, PROJ_TILE), lambda i, j: (0, j)),
                  pl.BlockSpec((1, IDX_DIM), lambda i, j: (0, 0)),
                  pl.BlockSpec((1, IDX_DIM), lambda i, j: (0, 0))],
        out_specs=[pl.BlockSpec((tm, PROJ_TILE), lambda i, j: (i, jnp.minimum(j, ZA_TILES - 1))),
                   pl.BlockSpec((tm, PROJ_TILE), row),
                   pl.BlockSpec((tm, PROJ_TILE), row),
                   pl.BlockSpec((tm, PROJ_TILE), row),
                   pl.BlockSpec((tm, IDX_DIM), row)],
        out_shape=[jax.ShapeDtypeStruct((n, ZA_WIDTH), F32),
                   jax.ShapeDtypeStruct((n, PROJ_TILE), F32),
                   jax.ShapeDtypeStruct((n, PROJ_TILE), F32),
                   jax.ShapeDtypeStruct((n, PROJ_TILE), F32),
                   jax.ShapeDtypeStruct((n, IDX_DIM), F32)],
        scratch_shapes=[pltpu.VMEM((tm, D_MODEL), BF16)],
        compiler_params=pltpu.CompilerParams(dimension_semantics=("arbitrary", "arbitrary"),
                                             vmem_limit_bytes=VMEM_LIMIT),
        name="in_proj",
    )(x2d, g_mix.reshape(1, -1), w_cat, idx_k_g.reshape(1, -1), idx_k_b.reshape(1, -1))


def _gla_consts(c):
    nlev = int(math.log2(c))
    t = np.arange(c)[:, None]
    s = np.arange(c)[None, :]
    mats = [(s <= t), np.ones((c, c), bool)]
    masks = []
    for l in range(nlev):
        mid = ((t >> (l + 1)) << (l + 1)) + (1 << l) - 1
        mats.append(s <= mid)
        masks.append(((t >> (l + 1)) == (s >> (l + 1))) & (((t >> l) & 1) == 1) & (((s >> l) & 1) == 0))
    masks.append(t == s)
    return (jnp.asarray(np.concatenate(mats, 0).astype(np.float32), BF16),
            jnp.asarray(np.stack(masks).astype(np.float32), F32), nlev)


_NT = (((1,), (1,)), ((), ()))
_TN = (((0,), (0,)), ((), ()))


def _gla_body(q_ref, k_ref, v_ref, misc_ref, wgu_ref, bg_ref, gn_ref, mst_ref, lmask_ref, s0_ref,
              o_ref, sout_ref, s_scr, la_scr, *, c, nc, nlev, valid_rows, bb):
    step = pl.program_id(1)
    tile = c * nc

    @pl.when(step == 0)
    def _():
        for bi in range(bb):
            for h in range(GLA_HEADS):
                s_scr[bi, h] = s0_ref[bi, h].T

    for bi in range(bb):
        x = jnp.dot(misc_ref[bi].astype(BF16), wgu_ref[...], preferred_element_type=F32) + bg_ref[...]
        log_a = (jnp.minimum(x, 0.0) - jnp.log1p(jnp.exp(-jnp.abs(x)))) * (1.0 / GLA_GATE_TAU)
        if valid_rows < tile:
            log_a = jnp.where(lax.broadcasted_iota(I32, log_a.shape, 0) < valid_rows, log_a, 0.0)
        la_scr[bi] = log_a
    scale = GLA_HEAD_DK ** -0.5

    def chunk(ci, carry):
        r0 = pl.multiple_of(ci * c, c)
        rows = pl.ds(r0, c)
        for bi in range(bb):
            g_all = la_scr[bi, rows, :]
            g_hi = g_all.astype(BF16)
            g_lo = (g_all - g_hi.astype(F32)).astype(BF16)
            cs = jnp.dot(mst_ref[...], jnp.concatenate([g_hi, g_lo], axis=1), preferred_element_type=F32)
            cs = cs[:, :GLA_DK] + cs[:, GLA_DK:]
            for h in range(GLA_HEADS):
                ksl = slice(h * GLA_HEAD_DK, (h + 1) * GLA_HEAD_DK)
                vsl = slice(h * GLA_HEAD_DV, (h + 1) * GLA_HEAD_DV)
                q = q_ref[bi, rows, ksl] * scale
                k = k_ref[bi, rows, ksl]
                v = v_ref[bi, rows, vsl].astype(BF16)
                b = cs[0:c, ksl]
                e_last = cs[c:2 * c, ksl]
                st = s_scr[bi, h]
                o = lax.dot_general((q * jnp.exp(b)).astype(BF16), st.astype(BF16), _NT,
                                    preferred_element_type=F32)
                att = lmask_ref[nlev] * lax.dot_general(q.astype(BF16), k.astype(BF16), _NT,
                                                        preferred_element_type=F32)
                for l in range(nlev):
                    e = cs[(2 + l) * c:(3 + l) * c, ksl]
                    ql = (q * jnp.exp(jnp.minimum(b - e, 0.0))).astype(BF16)
                    kl = (k * jnp.exp(jnp.minimum(e - b, 0.0))).astype(BF16)
                    att = att + lmask_ref[l] * lax.dot_general(ql, kl, _NT, preferred_element_type=F32)
                o = o + jnp.dot(att.astype(BF16), v, preferred_element_type=F32)
                k_st = (k * jnp.exp(e_last - b)).astype(BF16)
                s_scr[bi, h] = st * jnp.exp(e_last[0:1, :]) + lax.dot_general(v, k_st, _TN,
                                                                              preferred_element_type=F32)
                o_ref[bi, rows, vsl] = _rms(o, gn_ref[...])
        return carry

    lax.fori_loop(0, nc, chunk, 0)

    @pl.when(step == pl.num_programs(1) - 1)
    def _():
        for bi in range(bb):
            for h in range(GLA_HEADS):
                sout_ref[bi, h] = s_scr[bi, h].T


def _gla(q_arr, k_arr, v_arr, misc_arr, cols, wgu_pad, b_gate, gla_norm_g, s0, c, tile, valid_rows, bb):
    nb, l = q_arr.shape[0], q_arr.shape[1]
    assert nb % bb == 0 and l % tile == 0
    mst, lmask, nlev = _gla_consts(c)
    const2 = lambda b, s: (0, 0)
    state_spec = pl.BlockSpec((bb, GLA_HEADS, GLA_HEAD_DK, GLA_HEAD_DV), lambda b, s: (b, 0, 0, 0))
    body = functools.partial(_gla_body, c=c, nc=tile // c, nlev=nlev, valid_rows=valid_rows, bb=bb)
    return pl.pallas_call(
        body,
        grid=(nb // bb, l // tile),
        in_specs=[pl.BlockSpec((bb, tile, GLA_DK), lambda b, s: (b, s, cols[0])),
                  pl.BlockSpec((bb, tile, GLA_DK), lambda b, s: (b, s, cols[1])),
                  pl.BlockSpec((bb, tile, GLA_DV), lambda b, s: (b, s, cols[2])),
                  pl.BlockSpec((bb, tile, LANES), lambda b, s: (b, s, 0)),
                  pl.BlockSpec((LANES, GLA_DK), const2),
                  pl.BlockSpec((1, GLA_DK), const2),
                  pl.BlockSpec((1, GLA_HEAD_DV), const2),
                  pl.BlockSpec(mst.shape, const2),
                  pl.BlockSpec(lmask.shape, lambda b, s: (0, 0, 0)),
                  state_spec],
        out_specs=[pl.BlockSpec((bb, tile, GLA_DV), lambda b, s: (b, s, 0)), state_spec],
        out_shape=[jax.ShapeDtypeStruct((nb, l, GLA_DV), F32),
                   jax.ShapeDtypeStruct((nb, GLA_HEADS, GLA_HEAD_DK, GLA_HEAD_DV), F32)],
        scratch_shapes=[pltpu.VMEM((bb, GLA_HEADS, GLA_HEAD_DV, GLA_HEAD_DK), F32),
                        pltpu.VMEM((bb, tile, GLA_DK), F32)],
        compiler_params=pltpu.CompilerParams(dimension_semantics=("arbitrary", "arbitrary"),
                                             vmem_limit_bytes=VMEM_LIMIT),
        name="gla",
    )(q_arr, k_arr, v_arr, misc_arr, wgu_pad, b_gate.reshape(1, -1), gla_norm_g.reshape(1, -1),
      mst, lmask, s0)


def _score_stats_init(w):
    inf = jnp.full((SUBLANES, w), jnp.inf, F32)
    zero = jnp.zeros((SUBLANES, w), I32)
    return -inf, inf, zero, zero


def _score_stats_update(carry, blk):
    mx, mn, c_pos, c_nn = carry
    b3 = blk.reshape(blk.shape[0] // SUBLANES, SUBLANES, blk.shape[1])
    return (jnp.maximum(mx, jnp.max(b3, axis=0)),
            jnp.minimum(mn, jnp.min(jnp.where(b3 == -jnp.inf, jnp.inf, b3), axis=0)),
            c_pos + jnp.sum((b3 > 0.0).astype(I32), axis=0),
            c_nn + jnp.sum((b3 >= 0.0).astype(I32), axis=0))


def _select_threshold(sc_ref, tri_ref, nt, tr, topk, small, lane_ok, stats=None):
    w = sc_ref.shape[1]
    inf = jnp.float32(jnp.inf)

    def over_tiles(fn, init):
        def body(i, carry):
            r0 = pl.multiple_of(i * tr, tr)
            return fn(carry, sc_ref[pl.ds(r0, tr), :], r0)
        return lax.fori_loop(0, nt, body, init)

    fold = lambda x: x.reshape(tr // SUBLANES, SUBLANES, w)
    zeros8 = jnp.zeros((SUBLANES, w), I32)
    pinf8 = jnp.full((SUBLANES, w), inf, F32)

    def count(pred):
        acc = over_tiles(lambda a, blk, r0: a + jnp.sum(fold(pred(blk, r0).astype(I32)), axis=0), zeros8)
        return jnp.sum(acc, axis=0, keepdims=True)

    def min_where(pred):
        acc = over_tiles(lambda a, blk, r0: jnp.minimum(
            a, jnp.min(fold(jnp.where(pred(blk, r0), blk, inf)), axis=0)), pinf8)
        return jnp.min(acc, axis=0, keepdims=True)

    if stats is None:
        stats = over_tiles(lambda carry, blk, r0: _score_stats_update(carry, blk), _score_stats_init(w))
    mx8, mn8, cp8, cn8 = stats
    c_pos = jnp.sum(cp8, axis=0, keepdims=True)
    c_nn = jnp.sum(cn8, axis=0, keepdims=True)
    hi = jnp.where(c_pos >= topk, jnp.max(mx8, axis=0, keepdims=True), 0.0)
    lo = jnp.where(c_nn >= topk, 0.0, jnp.min(mn8, axis=0, keepdims=True))

    live = lane_ok & jnp.logical_not(small)

    def bisect(_, carry):
        lo, hi = carry
        mid = 0.5 * lo + 0.5 * hi
        ge = count(lambda blk, r0: blk >= mid) >= topk
        return jnp.where(ge, mid, lo), jnp.where(ge, hi, mid)

    lo, hi = lax.fori_loop(0, BISECT_STEPS, bisect, (lo, hi))
    v0 = min_where(lambda blk, r0: blk >= lo)

    def gt_next(v):
        def f(carry, blk, r0):
            cg, nx = carry
            gt = blk > v
            return (cg + jnp.sum(fold(gt.astype(I32)), axis=0),
                    jnp.minimum(nx, jnp.min(fold(jnp.where(gt, blk, inf)), axis=0)))
        cg8, nx8 = over_tiles(f, (zeros8, pinf8))
        return jnp.sum(cg8, axis=0, keepdims=True), jnp.min(nx8, axis=0, keepdims=True)

    def peel(state):
        v, _, _ = state
        cg, nx = gt_next(v)
        move = (cg >= topk) & live
        return jnp.where(move, nx, v), cg, jnp.max(move.astype(I32))

    v, cnt_gt, _ = lax.while_loop(lambda s: s[2] > 0, peel,
                                  (v0, jnp.zeros((1, w), I32), jnp.int32(1)))
    cnt_ge = count(lambda blk, r0: blk >= v)
    excess = (cnt_ge > topk) & live

    @pl.when(jnp.max(excess.astype(I32)) > 0)
    def _():
        need = (topk - cnt_gt).astype(F32)

        tb = tri_ref.shape[0]

        def drop_surplus(i, seen):
            r0 = pl.multiple_of(i * tr, tr)
            blk = sc_ref[pl.ds(r0, tr), :]
            eq = blk == v
            ones = jnp.where(eq, 1.0, 0.0).astype(BF16)
            ranks = [jnp.dot(tri_ref[...], ones[s * tb:(s + 1) * tb, :], preferred_element_type=F32)
                     for s in range(tr // tb)]
            for s in range(tr // tb):
                rows = slice(s * tb, (s + 1) * tb)
                rank = ranks[s] + seen
                sc_ref[pl.ds(r0 + s * tb, tb), :] = jnp.where(eq[rows] & (rank > need) & excess, -inf, blk[rows])
                seen = rank[tb - 1:tb, :]
            return seen

        lax.fori_loop(0, nt, drop_surplus, jnp.zeros((1, w), F32))

    return jnp.where(small, -inf, v)


def _dsa_prompt_body(relb_ref, btab_ref, tri_ref, ki_ref, qi_ref, wi_ref, kd_ref, vt_ref, qd_ref, o_ref,
                     sc_scr, tbl_scr, q2_scr, acc_scr, m_scr, o_scr, lg_scr, pr_scr, *,
                     topk, far_bucket):
    b = pl.program_id(0)
    j = pl.program_id(1)
    tk = DSA_TK
    hd = DSA_HEAD_DIM

    @pl.when((b == 0) & (j == 0))
    def _():
        q2_scr[...] = jnp.zeros(q2_scr.shape, BF16)

        def build(ci, carry):
            r0 = pl.multiple_of(ci * LANES, LANES)
            bt = btab_ref[pl.ds(r0, LANES), :]
            for h in range(DSA_HEADS):
                t = jnp.zeros(bt.shape, F32)
                for bk in range(REL_BUCKETS):
                    t = jnp.where(bt == bk, (relb_ref[bk, h] - relb_ref[far_bucket, h]) * LOG2E, t)
                tbl_scr[h, pl.ds(r0, LANES), :] = t
            return carry

        lax.fori_loop(0, BIAS_TAB_ROWS // LANES, build, 0)

    nt = j // (tk // QUERY_BLOCK) + 1
    qpos = j * QUERY_BLOCK + lax.broadcasted_iota(I32, (1, LANES), 1)
    for p in range(DSA_HEADS // 2):
        q2_scr[p, 0:hd, 0:LANES] = qd_ref[2 * p * hd:(2 * p + 1) * hd, :]
        q2_scr[p, hd:2 * hd, LANES:2 * LANES] = qd_ref[(2 * p + 1) * hd:(2 * p + 2) * hd, :]

    wi = wi_ref[...]

    def score_tile(i, carry):
        r0 = pl.multiple_of(i * tk, tk)
        s4 = jnp.dot(ki_ref[pl.ds(r0, tk), :], qi_ref[...], preferred_element_type=F32)
        sc = jnp.zeros((tk, LANES), F32)
        for h in range(IDX_HEADS):
            sc = sc + jnp.maximum(s4[:, h * LANES:(h + 1) * LANES], 0.0) * wi[h:h + 1, :]
        kpos = r0 + lax.broadcasted_iota(I32, (tk, LANES), 0)
        sc = jnp.where(kpos <= qpos, sc, -jnp.inf)
        sc_scr[pl.ds(r0, tk), :] = sc
        return _score_stats_update(carry, sc)

    stats = lax.fori_loop(0, nt, score_tile, _score_stats_init(LANES))

    small = (qpos + 1) < topk
    kstar = _select_threshold(sc_scr, tri_ref, nt, tk, topk, small, jnp.full((1, LANES), True), stats)

    m_scr[...] = jnp.full(m_scr.shape, NEG, F32)
    acc_scr[...] = jnp.zeros(acc_scr.shape, F32)
    vrows = acc_scr.shape[0] // DSA_HEADS

    def attend(i, near):
        r0 = pl.multiple_of(i * tk, tk)
        blk = sc_scr[pl.ds(r0, tk), :]
        if near:
            kpos = r0 + lax.broadcasted_iota(I32, (tk, LANES), 0)
            addm = jnp.where(blk >= kstar, jnp.where(kpos <= qpos, 0.0, NEG), NEG)
            off = pl.multiple_of(i * tk - j * QUERY_BLOCK + BIAS_TAB_OFF, LANES)
        else:
            addm = jnp.where(blk >= kstar, 0.0, NEG)
        tile_max = []
        for p in range(DSA_HEADS // 2):
            lg2 = jnp.dot(kd_ref[pl.ds(r0, tk), p * LANES:(p + 1) * LANES], q2_scr[p],
                          preferred_element_type=F32)
            for hh in range(2):
                h = 2 * p + hh
                lg = lg2[:, hh * LANES:(hh + 1) * LANES] + addm
                if near:
                    lg = lg + tbl_scr[h, pl.ds(off, tk), :]
                lg_scr[h] = lg
                tile_max.append(jnp.max(lg, axis=0, keepdims=True))
        alpha = []
        for h in range(DSA_HEADS):
            m_old = m_scr[h:h + 1, :]
            m_new = jnp.maximum(m_old, tile_max[h])
            m_scr[h:h + 1, :] = m_new
            alpha.append(jnp.exp2(m_old - m_new))
            pr_scr[h] = jnp.exp2((lg_scr[h] - m_new).astype(BF16))
        for h in range(DSA_HEADS):
            rows = slice(h * vrows, (h + 1) * vrows)
            acc_scr[rows, :] = alpha[h] * acc_scr[rows, :] + jnp.dot(
                vt_ref[rows, pl.ds(r0, tk)], pr_scr[h], preferred_element_type=F32)

    def step(near):
        def body(i, carry):
            attend(i, near)
            return carry
        return body

    n_far = jnp.maximum(nt - 2, 0)
    lax.fori_loop(0, n_far, step(False), 0)
    lax.fori_loop(n_far, nt, step(True), 0)

    for h in range(DSA_HEADS):
        o_scr[h * hd:(h + 1) * hd, :] = (acc_scr[h * vrows:h * vrows + hd, :]
                                         / acc_scr[h * vrows + hd:h * vrows + hd + 1, :])
    o_ref[...] = o_scr[...].T.astype(BF16)


def _tri_ones(n):
    return jnp.asarray(np.tril(np.ones((n, n), np.float32)), BF16)


def _bias_bucket_table():
    u = np.arange(BIAS_TAB_ROWS)[:, None]
    r = np.arange(LANES)[None, :]
    return jnp.asarray(_t5_bucket_np(r + BIAS_TAB_OFF - u), I32)


def _dsa_prompt(zA, kd, vd, misc, kin, rel_bias, nb, l):
    nq = l // QUERY_BLOCK
    topk = min(TOPK_MAX, l // 4)
    assert l % DSA_TK == 0
    far = _t5_bucket_np(np.arange(REL_MAX_DIST, max(l, REL_MAX_DIST + 1)))
    assert (far == far[0]).all()
    ki = kin.reshape(nb, l, IDX_DIM).astype(BF16)
    qi = misc[:, MISC_QI:MISC_QI + IDX_HEADS * IDX_DIM].reshape(nb, nq, QUERY_BLOCK, IDX_HEADS, IDX_DIM)
    qi = qi.transpose(0, 4, 1, 3, 2).reshape(nb, IDX_DIM, nq * IDX_HEADS * QUERY_BLOCK).astype(BF16)
    wi = misc[:, MISC_WI:MISC_WI + IDX_HEADS] * ((IDX_DIM ** -0.5) * (IDX_HEADS ** -0.5))
    wi = wi.reshape(nb, nq, QUERY_BLOCK, IDX_HEADS).transpose(0, 1, 3, 2)
    wi = jnp.pad(wi, ((0, 0), (0, 0), (0, SUBLANES - IDX_HEADS), (0, 0))).reshape(nb, nq * SUBLANES, QUERY_BLOCK)
    kd_bf = kd.reshape(nb, l, DSA_WIDTH).astype(BF16)
    vt = vd.reshape(nb, l, DSA_HEADS, DSA_HEAD_DIM).transpose(0, 2, 3, 1)
    vt = jnp.concatenate([vt, jnp.ones((nb, DSA_HEADS, 1, l), vt.dtype),
                          jnp.zeros((nb, DSA_HEADS, V_ROWS - DSA_HEAD_DIM - 1, l), vt.dtype)], axis=2)
    vt = vt.reshape(nb, DSA_HEADS * V_ROWS, l).astype(BF16)
    qd = (zA[:, ZA_QD:ZA_QD + DSA_WIDTH] * ((DSA_HEAD_DIM ** -0.5) * LOG2E)).reshape(nb, l, DSA_WIDTH)
    qd = qd.transpose(0, 2, 1).astype(BF16)
    body = functools.partial(_dsa_prompt_body, topk=topk, far_bucket=int(far[0]))
    whole = lambda b, j: (b, 0, 0)
    return pl.pallas_call(
        body,
        grid=(nb, nq),
        in_specs=[pl.BlockSpec(memory_space=pltpu.SMEM),
                  pl.BlockSpec((BIAS_TAB_ROWS, LANES), lambda b, j: (0, 0), pipeline_mode=pl.Buffered(1)),
                  pl.BlockSpec((LANES, LANES), lambda b, j: (0, 0), pipeline_mode=pl.Buffered(1)),
                  pl.BlockSpec((None, l, IDX_DIM), whole, pipeline_mode=pl.Buffered(1)),
                  pl.BlockSpec((None, IDX_DIM, IDX_HEADS * QUERY_BLOCK), lambda b, j: (b, 0, j)),
                  pl.BlockSpec((None, SUBLANES, QUERY_BLOCK), lambda b, j: (b, j, 0)),
                  pl.BlockSpec((None, l, DSA_WIDTH), whole, pipeline_mode=pl.Buffered(1)),
                  pl.BlockSpec((None, DSA_HEADS * V_ROWS, l), whole, pipeline_mode=pl.Buffered(1)),
                  pl.BlockSpec((None, DSA_WIDTH, QUERY_BLOCK), lambda b, j: (b, 0, j))],
        out_specs=pl.BlockSpec((None, QUERY_BLOCK, DSA_WIDTH), lambda b, j: (b, j, 0)),
        out_shape=jax.ShapeDtypeStruct((nb, l, DSA_WIDTH), BF16),
        scratch_shapes=[pltpu.VMEM((l, LANES), F32),
                        pltpu.VMEM((DSA_HEADS, BIAS_TAB_ROWS, LANES), F32),
                        pltpu.VMEM((DSA_HEADS // 2, LANES, 2 * LANES), BF16),
                        pltpu.VMEM((DSA_HEADS * V_ROWS, LANES), F32),
                        pltpu.VMEM((DSA_HEADS, LANES), F32),
                        pltpu.VMEM((DSA_WIDTH, LANES), F32),
                        pltpu.VMEM((DSA_HEADS, DSA_TK, LANES), F32),
                        pltpu.VMEM((DSA_HEADS, DSA_TK, LANES), BF16)],
        compiler_params=pltpu.CompilerParams(dimension_semantics=("arbitrary", "arbitrary"),
                                             vmem_limit_bytes=VMEM_LIMIT),
        name="dsa_prompt",
    )(rel_bias, _bias_bucket_table(), _tri_ones(LANES), ki, qi, wi, kd_bf, vt, qd)


def _select_threshold_lanes(sc_ref, triu_ref, nt, topk, small, row_ok):
    r = sc_ref.shape[0]
    inf = jnp.float32(jnp.inf)
    tiles = [slice(i * LANES, (i + 1) * LANES) for i in range(nt)]
    rowsum = lambda x: jnp.sum(x, axis=1, keepdims=True)
    rowmin = lambda x: jnp.min(x, axis=1, keepdims=True)

    def count(pred):
        acc = jnp.zeros((r, LANES), I32)
        for t in tiles:
            acc = acc + pred(sc_ref[:, t]).astype(I32)
        return rowsum(acc)

    def min_where(pred):
        acc = jnp.full((r, LANES), inf, F32)
        for t in tiles:
            blk = sc_ref[:, t]
            acc = jnp.minimum(acc, jnp.where(pred(blk), blk, inf))
        return rowmin(acc)

    mx = jnp.full((r, LANES), -inf, F32)
    mn = jnp.full((r, LANES), inf, F32)
    c_pos = jnp.zeros((r, LANES), I32)
    c_nn = jnp.zeros((r, LANES), I32)
    for t in tiles:
        blk = sc_ref[:, t]
        mx = jnp.maximum(mx, blk)
        mn = jnp.minimum(mn, jnp.where(blk == -inf, inf, blk))
        c_pos = c_pos + (blk > 0.0).astype(I32)
        c_nn = c_nn + (blk >= 0.0).astype(I32)
    hi = jnp.where(rowsum(c_pos) >= topk, jnp.max(mx, axis=1, keepdims=True), 0.0)
    lo = jnp.where(rowsum(c_nn) >= topk, 0.0, rowmin(mn))
    live = row_ok & jnp.logical_not(small)

    def bisect(_, carry):
        lo, hi = carry
        mid = 0.5 * lo + 0.5 * hi
        ge = count(lambda blk: blk >= mid) >= topk
        return jnp.where(ge, mid, lo), jnp.where(ge, hi, mid)

    lo, hi = lax.fori_loop(0, BISECT_STEPS, bisect, (lo, hi))
    v0 = min_where(lambda blk: blk >= lo)

    def peel(state):
        v, _, _ = state
        cg = jnp.zeros((r, LANES), I32)
        nx = jnp.full((r, LANES), inf, F32)
        for t in tiles:
            blk = sc_ref[:, t]
            gt = blk > v
            cg = cg + gt.astype(I32)
            nx = jnp.minimum(nx, jnp.where(gt, blk, inf))
        cg = rowsum(cg)
        move = (cg >= topk) & live
        return jnp.where(move, rowmin(nx), v), cg, jnp.max(move.astype(I32))

    v, cnt_gt, _ = lax.while_loop(lambda s: s[2] > 0, peel,
                                  (v0, jnp.zeros((r, 1), I32), jnp.int32(1)))
    excess = (count(lambda blk: blk >= v) > topk) & live

    @pl.when(jnp.max(excess.astype(I32)) > 0)
    def _():
        need = (topk - cnt_gt).astype(F32)
        seen = jnp.zeros((r, 1), F32)
        for t in tiles:
            blk = sc_ref[:, t]
            eq = blk == v
            rank = seen + jnp.dot(jnp.where(eq, 1.0, 0.0).astype(BF16), triu_ref[...],
                                  preferred_element_type=F32)
            sc_ref[:, t] = jnp.where(eq & (rank > need) & excess, -inf, blk)
            seen = rank[:, LANES - 1:LANES]

    return jnp.where(small, -inf, v)


SEQ_GROUP = LANES // SUBLANES


def _dsa_sample_select_body(pt_ref, *refs, n_pages, t_new, topk):
    page_refs = refs[:n_pages]
    knew_ref, qi_ref, wi_ref, triu_ref, mask_ref, sc_scr = refs[n_pages:]
    g = pl.program_id(1)
    past = n_pages * PAGE_SIZE
    rows = pl.ds(pl.multiple_of(g * SUBLANES, SUBLANES), SUBLANES)
    for p in range(n_pages + 1):
        keys_t = (page_refs[p][...] if p < n_pages else knew_ref[...]).astype(BF16)
        s = jnp.dot(qi_ref[...], keys_t, preferred_element_type=F32)
        sc = jnp.zeros((SUBLANES, PAGE_SIZE), F32)
        for h in range(IDX_HEADS):
            hs = slice(h * SUBLANES, (h + 1) * SUBLANES)
            sc = sc + jnp.maximum(s[hs, :], 0.0) * wi_ref[hs, :]
        sc_scr[rows, p * PAGE_SIZE:(p + 1) * PAGE_SIZE] = sc

    @pl.when(g == SEQ_GROUP - 1)
    def _():
        q_of_row = lax.broadcasted_iota(I32, (LANES, 1), 0) % SUBLANES
        new = slice(past, past + PAGE_SIZE)
        cpos = lax.broadcasted_iota(I32, (LANES, PAGE_SIZE), 1)
        sc_scr[:, new] = jnp.where(cpos <= q_of_row, sc_scr[:, new], -jnp.inf)
        row_ok = q_of_row < t_new
        small = (past + q_of_row + 1) < topk
        kstar = _select_threshold_lanes(sc_scr, triu_ref, n_pages + 1, topk, small, row_ok)
        for p in range(n_pages + 1):
            t = slice(p * PAGE_SIZE, (p + 1) * PAGE_SIZE)
            mask_ref[:, t] = jnp.where(sc_scr[:, t] >= kstar, 1.0, 0.0)


def _dsa_sample_select(page_table, kidx_t, kin_new_t, qi_rows, wi_rows, t_new, topk):
    db, n_pages = page_table.shape
    ng = db // SEQ_GROUP
    width = (n_pages + 1) * PAGE_SIZE
    seq = lambda gi, g, pt: (gi * SEQ_GROUP + g, 0, 0)
    page_specs = [pl.BlockSpec((None, IDX_DIM, PAGE_SIZE),
                               functools.partial(lambda gi, g, pt, p: (pt[gi * SEQ_GROUP + g, p], 0, 0), p=p))
                  for p in range(n_pages)]
    body = functools.partial(_dsa_sample_select_body, n_pages=n_pages, t_new=t_new, topk=topk)
    grid_spec = pltpu.PrefetchScalarGridSpec(
        num_scalar_prefetch=1,
        grid=(ng, SEQ_GROUP),
        in_specs=page_specs + [pl.BlockSpec((None, IDX_DIM, PAGE_SIZE), seq),
                               pl.BlockSpec((None, IDX_HEADS * SUBLANES, IDX_DIM), seq),
                               pl.BlockSpec((None, IDX_HEADS * SUBLANES, PAGE_SIZE), seq),
                               pl.BlockSpec((LANES, LANES), lambda gi, g, pt: (0, 0))],
        out_specs=pl.BlockSpec((None, LANES, width), lambda gi, g, pt: (gi, 0, 0)),
        scratch_shapes=[pltpu.VMEM((LANES, width), F32)],
    )
    triu = jnp.asarray(np.triu(np.ones((LANES, LANES), np.float32)), BF16)
    return pl.pallas_call(
        body, grid_spec=grid_spec,
        out_shape=jax.ShapeDtypeStruct((ng, LANES, width), F32),
        compiler_params=pltpu.CompilerParams(dimension_semantics=("arbitrary", "arbitrary"),
                                             vmem_limit_bytes=VMEM_LIMIT),
        name="dsa_sample_select",
    )(page_table, *([kidx_t] * n_pages), kin_new_t, qi_rows, wi_rows, triu)


def _dsa_sample_attend_body(pt_ref, *refs, n_pages):
    k_refs = refs[:n_pages]
    v_refs = refs[n_pages:2 * n_pages]
    (knew_ref, vnew_ref, mask_ref, qbd_ref, btab_ref, rb_ref, o_ref, lg_scr, tbl_scr) = refs[2 * n_pages:]
    b = pl.program_id(0)
    hq = DSA_HEADS * SUBLANES

    @pl.when(b == 0)
    def _():
        for half in range(2):
            cols = slice(half * PAGE_SIZE, (half + 1) * PAGE_SIZE)
            bt = btab_ref[:, cols]
            t = jnp.zeros(bt.shape, F32)
            for bk in range(REL_BUCKETS):
                t = jnp.where(bt == bk, rb_ref[bk], t)
            tbl_scr[:, cols] = t

    far_bias = rb_ref[REL_BUCKETS - 1]
    page = lambda ref: ref[...].reshape(DSA_WIDTH, PAGE_SIZE).astype(BF16)

    m = jnp.full((hq, LANES), NEG, F32)
    for p in range(n_pages + 1):
        cols = slice(p * PAGE_SIZE, (p + 1) * PAGE_SIZE)
        k_t = page(k_refs[p]) if p < n_pages else knew_ref[...].astype(BF16)
        lg = jnp.dot(qbd_ref[...], k_t, preferred_element_type=F32)
        if p >= n_pages - 1:
            lg = lg + tbl_scr[:, (p - n_pages + 1) * PAGE_SIZE:(p - n_pages + 2) * PAGE_SIZE]
        else:
            lg = lg + far_bias
        sel = jnp.concatenate([mask_ref[:, cols]] * DSA_HEADS, axis=0)
        lg = lg + jnp.where(sel > 0.5, 0.0, NEG)
        lg_scr[:, cols] = lg
        m = jnp.maximum(m, lg)
    m = jnp.max(m, axis=1, keepdims=True)

    acc = jnp.zeros((hq, DSA_WIDTH), F32)
    lsum = jnp.zeros((hq, LANES), F32)
    for p in range(n_pages + 1):
        cols = slice(p * PAGE_SIZE, (p + 1) * PAGE_SIZE)
        v_t = page(v_refs[p]) if p < n_pages else vnew_ref[...].astype(BF16)
        pr = jnp.exp(lg_scr[:, cols] - m)
        lsum = lsum + pr
        acc = acc + lax.dot_general(pr.astype(BF16), v_t, _NT, preferred_element_type=F32)
    acc = acc / jnp.sum(lsum, axis=1, keepdims=True)
    head_of_lane = lax.broadcasted_iota(I32, (SUBLANES, DSA_WIDTH), 1) // DSA_HEAD_DIM
    out = jnp.zeros((SUBLANES, DSA_WIDTH), F32)
    for h in range(DSA_HEADS):
        out = jnp.where(head_of_lane == h, acc[h * SUBLANES:(h + 1) * SUBLANES, :], out)
    o_ref[...] = out


def _dsa_sample_attend(page_table, k_t, v_t, k_new_t, v_new_t, mask, qbd, rel_bias):
    db, n_pages = page_table.shape
    hq = DSA_HEADS * SUBLANES
    width = (n_pages + 1) * PAGE_SIZE
    seq = lambda b, pt: (b, 0, 0)
    page = lambda p: functools.partial(lambda b, pt, p: (pt[b, p], 0, 0, 0), p=p)
    kv_spec = lambda p: pl.BlockSpec((None, DSA_HEADS, DSA_HEAD_DIM, PAGE_SIZE), page(p))
    u = np.arange(2 * PAGE_SIZE)[None, :]
    q = (np.arange(hq) % SUBLANES)[:, None]
    btab = jnp.asarray(_t5_bucket_np(PAGE_SIZE + q - u), I32)
    rb = jnp.broadcast_to(jnp.repeat(rel_bias, SUBLANES, axis=1)[:, :, None], (REL_BUCKETS, hq, PAGE_SIZE))
    grid_spec = pltpu.PrefetchScalarGridSpec(
        num_scalar_prefetch=1,
        grid=(db,),
        in_specs=[kv_spec(p) for p in range(n_pages)] + [kv_spec(p) for p in range(n_pages)] + [
            pl.BlockSpec((None, DSA_WIDTH, PAGE_SIZE), seq),
            pl.BlockSpec((None, DSA_WIDTH, PAGE_SIZE), seq),
            pl.BlockSpec((None, SUBLANES, width), lambda b, pt: (b // SEQ_GROUP, b % SEQ_GROUP, 0)),
            pl.BlockSpec((None, hq, DSA_WIDTH), seq),
            pl.BlockSpec(btab.shape, lambda b, pt: (0, 0)),
            pl.BlockSpec(rb.shape, lambda b, pt: (0, 0, 0))],
        out_specs=pl.BlockSpec((None, SUBLANES, DSA_WIDTH), seq),
        scratch_shapes=[pltpu.VMEM((hq, width), F32), pltpu.VMEM((hq, 2 * PAGE_SIZE), F32)],
    )
    return pl.pallas_call(
        functools.partial(_dsa_sample_attend_body, n_pages=n_pages), grid_spec=grid_spec,
        out_shape=jax.ShapeDtypeStruct((db, SUBLANES, DSA_WIDTH), F32),
        compiler_params=pltpu.CompilerParams(dimension_semantics=("arbitrary",),
                                             vmem_limit_bytes=VMEM_LIMIT),
        name="dsa_sample_attend",
    )(page_table, *([k_t] * n_pages), *([v_t] * n_pages), k_new_t, v_new_t, mask, qbd, btab, rb)


def _dsa_sample(zA, kd, vd, misc, kin, rel_bias, cache_k, cache_v, cache_kidx, page_table, db, t_new):
    n_pages = page_table.shape[1]
    past = n_pages * PAGE_SIZE
    topk = min(TOPK_MAX, (past + t_new) // 4)
    assert db % SEQ_GROUP == 0 and t_new <= SUBLANES
    new_t = lambda a, w: jnp.pad(a.reshape(db, t_new, w).transpose(0, 2, 1),
                                 ((0, 0), (0, 0), (0, PAGE_SIZE - t_new)))
    pad_q = lambda a: jnp.pad(a, ((0, 0), (0, 0), (0, SUBLANES - t_new), (0, 0)))
    qi = misc[:, MISC_QI:MISC_QI + IDX_HEADS * IDX_DIM].reshape(db, t_new, IDX_HEADS, IDX_DIM)
    qi_rows = pad_q(qi.transpose(0, 2, 1, 3)).reshape(db, IDX_HEADS * SUBLANES, IDX_DIM).astype(BF16)
    wi = misc[:, MISC_WI:MISC_WI + IDX_HEADS] * ((IDX_DIM ** -0.5) * (IDX_HEADS ** -0.5))
    wi = pad_q(wi.reshape(db, t_new, IDX_HEADS, 1).transpose(0, 2, 1, 3)).reshape(db, IDX_HEADS * SUBLANES, 1)
    wi_rows = jnp.broadcast_to(wi, (db, IDX_HEADS * SUBLANES, PAGE_SIZE))
    mask = _dsa_sample_select(page_table, cache_kidx.transpose(0, 2, 1), new_t(kin, IDX_DIM),
                              qi_rows, wi_rows, t_new, topk)
    qd = (zA[:, ZA_QD:ZA_QD + DSA_WIDTH] * (DSA_HEAD_DIM ** -0.5)).reshape(db, t_new, DSA_HEADS, DSA_HEAD_DIM)
    qd = pad_q(qd.transpose(0, 2, 1, 3))
    eye = jnp.eye(DSA_HEADS, dtype=qd.dtype)
    qbd = (qd[:, :, :, None, :] * eye[None, :, None, :, None]).reshape(db, DSA_HEADS * SUBLANES, DSA_WIDTH)
    o = _dsa_sample_attend(page_table, cache_k.transpose(0, 2, 3, 1), cache_v.transpose(0, 2, 3, 1),
                           new_t(kd, DSA_WIDTH), new_t(vd, DSA_WIDTH), mask, qbd.astype(BF16), rel_bias)
    return o[:, :t_new, :].reshape(db * t_new, DSA_WIDTH).astype(BF16)


def _post_body(x_ref, og_ref, rg_ref, gg_ref, gd_ref, od_ref, wg_ref, wd_ref, wo_ref, gf_ref,
               x1_ref, hf_ref):
    rg = rg_ref[...]
    a = (og_ref[...] * (rg * _sigmoid(rg))).astype(BF16)
    y_g = jnp.dot(a, wg_ref[...], preferred_element_type=F32)
    y_d = jnp.dot(od_ref[...], wd_ref[...], preferred_element_type=F32)
    mix = (_sigmoid(gg_ref[...]) * y_g + _sigmoid(gd_ref[...]) * y_d).astype(BF16)
    x1 = x_ref[...] + jnp.dot(mix, wo_ref[...], preferred_element_type=F32)
    x1_ref[...] = x1
    hf_ref[...] = _rms(x1, gf_ref[...]).astype(BF16)


def _post(x2d, og, zA, od, w_gla, w_dsa, w_o, g_ffn):
    n = x2d.shape[0]
    tm = min(n, 512)
    row = lambda i: (i, 0)
    const = lambda i: (0, 0)
    return pl.pallas_call(
        _post_body,
        grid=(n // tm,),
        in_specs=[pl.BlockSpec((tm, D_MODEL), row),
                  pl.BlockSpec((tm, GLA_DV), row),
                  pl.BlockSpec((tm, GLA_DV), lambda i: (i, COL_RG)),
                  pl.BlockSpec((tm, D_MODEL), lambda i: (i, COL_GG)),
                  pl.BlockSpec((tm, D_MODEL), lambda i: (i, COL_GD)),
                  pl.BlockSpec((tm, DSA_WIDTH), row),
                  pl.BlockSpec((GLA_DV, D_MODEL), const),
                  pl.BlockSpec((DSA_WIDTH, D_MODEL), const),
                  pl.BlockSpec((D_MODEL, D_MODEL), const),
                  pl.BlockSpec((1, D_MODEL), const)],
        out_specs=[pl.BlockSpec((tm, D_MODEL), row), pl.BlockSpec((tm, D_MODEL), row)],
        out_shape=[jax.ShapeDtypeStruct((n, D_MODEL), F32), jax.ShapeDtypeStruct((n, D_MODEL), BF16)],
        compiler_params=pltpu.CompilerParams(dimension_semantics=("arbitrary",),
                                             vmem_limit_bytes=VMEM_LIMIT),
        name="post_mix",
    )(x2d, og, zA, zA, zA, od, w_gla, w_dsa, w_o, g_ffn.reshape(1, -1))


FFN_TILE = 256


def _ffn_body(hf_ref, x1_ref, wg_ref, wu_ref, wd_ref, gfin_ref, y_ref, acc_scr):
    k = pl.program_id(1)
    hf = hf_ref[...]
    gate = jnp.dot(hf, wg_ref[...], preferred_element_type=F32)
    up = jnp.dot(hf, wu_ref[...], preferred_element_type=F32)
    part = jnp.dot((gate * _sigmoid(gate) * up).astype(BF16), wd_ref[...], preferred_element_type=F32)

    @pl.when(k == 0)
    def _():
        acc_scr[...] = x1_ref[...] + part

    @pl.when(k > 0)
    def _():
        acc_scr[...] = acc_scr[...] + part

    @pl.when(k == pl.num_programs(1) - 1)
    def _():
        y_ref[...] = _rms(acc_scr[...], gfin_ref[...])


def _ffn(hf, x1, w_gate, w_up, w_down, g_final):
    n = hf.shape[0]
    d_ff = w_gate.shape[1]
    tm = min(n, 1024)
    row = lambda i, k: (i, 0)
    return pl.pallas_call(
        _ffn_body,
        grid=(n // tm, d_ff // FFN_TILE),
        in_specs=[pl.BlockSpec((tm, D_MODEL), row),
                  pl.BlockSpec((tm, D_MODEL), row),
                  pl.BlockSpec((D_MODEL, FFN_TILE), lambda i, k: (0, k)),
                  pl.BlockSpec((D_MODEL, FFN_TILE), lambda i, k: (0, k)),
                  pl.BlockSpec((FFN_TILE, D_MODEL), lambda i, k: (k, 0)),
                  pl.BlockSpec((1, D_MODEL), lambda i, k: (0, 0))],
        out_specs=pl.BlockSpec((tm, D_MODEL), row),
        out_shape=jax.ShapeDtypeStruct((n, D_MODEL), F32),
        scratch_shapes=[pltpu.VMEM((tm, D_MODEL), F32)],
        compiler_params=pltpu.CompilerParams(dimension_semantics=("arbitrary", "arbitrary"),
                                             vmem_limit_bytes=VMEM_LIMIT),
        name="ffn",
    )(hf, x1, w_gate, w_up, w_down, g_final.reshape(1, -1))


def _gate_weight_pad(w_gate_up):
    pad = jnp.zeros((LANES, GLA_DK), w_gate_up.dtype)
    return pad.at[MISC_ALOW:MISC_ALOW + GLA_GATE_RANK].set(w_gate_up).astype(BF16)


def kernel(x_prompt, x_sample, cache_k, cache_v, cache_kidx, state_gla, page_table, g_mix, w_in,
           w_gate_up, b_gate, gla_norm_g, w_gla_branch, idx_k_g, idx_k_b, w_dsa_branch, w_o, g_ffn,
           w_ffn_gate, w_ffn_up, w_ffn_down, rel_bias, g_final):
    depth = w_in.shape[0]
    assert depth == 1, "the final RMSNorm is fused into the FFN kernel of the single layer"
    nb, l, _ = x_prompt.shape
    db, t_new, _ = x_sample.shape
    layer = 0
    w_cat = _pack_w_in(w_in[layer])
    wgu_pad = _gate_weight_pad(w_gate_up[layer])
    w_gla = w_gla_branch[layer].astype(BF16)
    w_dsa = w_dsa_branch[layer].astype(BF16)
    w_out = w_o[layer].astype(BF16)
    w_fg, w_fu, w_fd = (w.astype(BF16) for w in (w_ffn_gate[layer], w_ffn_up[layer], w_ffn_down[layer]))

    xp = x_prompt.reshape(nb * l, D_MODEL)
    zA, kd, vd, misc, kin = _in_proj(xp, g_mix[layer], w_cat, idx_k_g[layer], idx_k_b[layer])
    zA3 = zA.reshape(nb, l, ZA_WIDTH)
    s0 = jnp.zeros((nb, GLA_HEADS, GLA_HEAD_DK, GLA_HEAD_DV), state_gla.dtype)
    og, s_p = _gla(zA3, zA3, zA3, misc.reshape(nb, l, PROJ_TILE), (COL_QG, COL_KG, COL_VG), wgu_pad,
                   b_gate[layer], gla_norm_g[layer], s0, GLA_CHUNK, GLA_TILE, GLA_TILE, nb)
    od = _dsa_prompt(zA, kd, vd, misc, kin, rel_bias, nb, l)
    x1, hf = _post(xp, og.reshape(nb * l, GLA_DV), zA, od.reshape(nb * l, DSA_WIDTH), w_gla, w_dsa,
                   w_out, g_ffn[layer])
    y_p = _ffn(hf, x1, w_fg, w_fu, w_fd, g_final).reshape(nb, l, D_MODEL)

    xs = x_sample.reshape(db * t_new, D_MODEL)
    zA_s, kd_s, vd_s, misc_s, kin_s = _in_proj(xs, g_mix[layer], w_cat, idx_k_g[layer], idx_k_b[layer])
    pad_t = lambda a: jnp.pad(a.reshape(db, t_new, -1), ((0, 0), (0, SAMPLE_CHUNK - t_new), (0, 0)))
    og_s, s_s = _gla(pad_t(zA_s[:, 0:GLA_DK]), pad_t(zA_s[:, GLA_DK:2 * GLA_DK]),
                     pad_t(zA_s[:, 2 * GLA_DK:2 * GLA_DK + GLA_DV]), pad_t(misc_s[:, 0:LANES]), (0, 0, 0),
                     wgu_pad, b_gate[layer], gla_norm_g[layer], state_gla[layer], SAMPLE_CHUNK, SAMPLE_CHUNK, t_new, SAMPLE_GLA_SEQS)
    og_s = og_s[:, :t_new, :].reshape(db * t_new, GLA_DV)
    od_s = _dsa_sample(zA_s, kd_s, vd_s, misc_s, kin_s, rel_bias, cache_k[layer], cache_v[layer],
                       cache_kidx[layer], page_table, db, t_new)
    x1_s, hf_s = _post(xs, og_s, zA_s, od_s, w_gla, w_dsa, w_out, g_ffn[layer])
    y_s = _ffn(hf_s, x1_s, w_fg, w_fu, w_fd, g_final).reshape(db, t_new, D_MODEL)

    heads = lambda a, n, t: a.reshape(1, n, t, DSA_HEADS, DSA_HEAD_DIM)
    return (y_p, y_s,
            heads(kd, nb, l), heads(vd, nb, l), kin.reshape(1, nb, l, IDX_DIM), s_p[None],
            heads(kd_s, db, t_new), heads(vd_s, db, t_new), kin_s.reshape(1, db, t_new, IDX_DIM), s_s[None])
```

```python
import functools
import math

import numpy as np
import jax
import jax.numpy as jnp
from jax import lax
from jax.experimental import pallas as pl
from jax.experimental.pallas import tpu as pltpu

F32, BF16, I32 = jnp.float32, jnp.bfloat16, jnp.int32

D_MODEL = 1024
GLA_HEADS = 4
GLA_HEAD_DK = 128
GLA_HEAD_DV = 256
GLA_DK = GLA_HEADS * GLA_HEAD_DK
GLA_DV = GLA_HEADS * GLA_HEAD_DV
GLA_GATE_RANK = 16
GLA_GATE_TAU = 16.0
DSA_HEADS = 8
DSA_HEAD_DIM = 64
DSA_WIDTH = DSA_HEADS * DSA_HEAD_DIM
IDX_HEADS = 4
IDX_DIM = 64
TOPK_MAX = 256
QUERY_BLOCK = 128
PAGE_SIZE = 128
REL_BUCKETS = 32
REL_MAX_DIST = 128
RMS_EPS = 1e-6
LN_EPS = 1e-6
SPLIT_SIZES = (GLA_DK, GLA_DK, GLA_DV, GLA_DV, GLA_GATE_RANK, DSA_WIDTH, DSA_WIDTH, DSA_WIDTH,
               IDX_HEADS * IDX_DIM, IDX_DIM, IDX_HEADS, D_MODEL, D_MODEL)

LANES = 128
SUBLANES = 8
VMEM_LIMIT = 56 * 1024 * 1024

PROJ_TILE = 512
ZA_TILES = 11
ZA_WIDTH = ZA_TILES * PROJ_TILE
COL_QG, COL_KG = 0, 1
COL_VG, COL_RG, COL_GG, COL_GD = 1, 2, 3, 4
ZA_QD = 10 * PROJ_TILE
MISC_ALOW, MISC_WI, MISC_QI = 64, 80, 128

GLA_CHUNK = 64
GLA_TILE = 256
SAMPLE_CHUNK = 16
SAMPLE_GLA_SEQS = 4
DSA_TK = 512
BIAS_TAB_ROWS = 1408
BIAS_TAB_OFF = 896
V_ROWS = 80
NEG = -1e30
LOG2E = 1.4426950408889634
BISECT_STEPS = 15


def _sigmoid(x):
    return 1.0 / (1.0 + jnp.exp(-x))


def _rms(x, g):
    return x * lax.rsqrt(jnp.mean(x * x, axis=-1, keepdims=True) + RMS_EPS) * g


def _t5_bucket_np(n):
    n = np.maximum(np.asarray(n, np.int64), 0)
    max_exact = REL_BUCKETS // 2
    large = max_exact + (np.log(np.maximum(n, 1).astype(np.float32) / np.float32(max_exact))
                         / np.float32(math.log(REL_MAX_DIST / max_exact))
                         * np.float32(REL_BUCKETS - max_exact)).astype(np.int32)
    large = np.minimum(large, REL_BUCKETS - 1)
    return np.where(n < max_exact, n, large).astype(np.int32)


def _inproj_wide_body(x_ref, g_ref, w_ref, za_ref):
    h = _rms(x_ref[...], g_ref[...]).astype(BF16)
    for j in range(ZA_TILES):
        cols = slice(j * PROJ_TILE, (j + 1) * PROJ_TILE)
        za_ref[:, cols] = jnp.dot(h, w_ref[:, cols], preferred_element_type=F32)


def _inproj_tail_body(x_ref, g_ref, w_ref, ikg_ref, ikb_ref, kd_ref, vd_ref, misc_ref, kin_ref):
    h = _rms(x_ref[...], g_ref[...]).astype(BF16)
    tile = lambda j: jnp.dot(h, w_ref[:, j * PROJ_TILE:(j + 1) * PROJ_TILE], preferred_element_type=F32)
    kd_ref[...] = tile(0)
    vd_ref[...] = tile(1)
    res = tile(2)
    misc_ref[...] = res
    ki = res[:, 0:IDX_DIM]
    mu = jnp.mean(ki, axis=-1, keepdims=True)
    var = jnp.mean(jnp.square(ki - mu), axis=-1, keepdims=True)
    kin_ref[...] = (ki - mu) * lax.rsqrt(var + LN_EPS) * ikg_ref[...] + ikb_ref[...]


def _pack_w_in(w_in):
    pts = np.cumsum((0,) + SPLIT_SIZES)
    seg = [w_in[:, int(pts[i]):int(pts[i + 1])] for i in range(len(SPLIT_SIZES))]
    q_g, k_g, v_g, r_g, a_low, q_d, k_d, v_d, q_i, k_i, w_i, gate_g, gate_d = seg
    z = lambda n: jnp.zeros((w_in.shape[0], n), w_in.dtype)
    misc = jnp.concatenate([k_i, a_low, w_i, z(LANES - MISC_WI - IDX_HEADS), q_i,
                            z(PROJ_TILE - MISC_QI - IDX_HEADS * IDX_DIM)], axis=1)
    return jnp.concatenate([q_g, k_g, v_g, r_g, gate_g, gate_d, q_d, k_d, v_d, misc], axis=1).astype(BF16)


def _in_proj(x2d, g_mix, w_cat, idx_k_g, idx_k_b):
    n = x2d.shape[0]
    row = lambda i: (i, 0)
    const = lambda i: (0, 0)
    resident = lambda shape: pl.BlockSpec(shape, const, pipeline_mode=pl.Buffered(1))
    params = pltpu.CompilerParams(dimension_semantics=("arbitrary",), vmem_limit_bytes=VMEM_LIMIT)
    g2 = g_mix.reshape(1, -1)
    tm = min(n, 512)
    zA = pl.pallas_call(
        _inproj_wide_body,
        grid=(n // tm,),
        in_specs=[pl.BlockSpec((tm, D_MODEL), row), pl.BlockSpec((1, D_MODEL), const),
                  resident((D_MODEL, ZA_WIDTH))],
        out_specs=pl.BlockSpec((tm, ZA_WIDTH), row),
        out_shape=jax.ShapeDtypeStruct((n, ZA_WIDTH), F32),
        compiler_params=params,
        name="in_proj_wide",
    )(x2d, g2, w_cat[:, :ZA_WIDTH])
    tm = min(n, 1024)
    tail = w_cat.shape[1] - ZA_WIDTH
    kd, vd, misc, kin = pl.pallas_call(
        _inproj_tail_body,
        grid=(n // tm,),
        in_specs=[pl.BlockSpec((tm, D_MODEL), row), pl.BlockSpec((1, D_MODEL), const),
                  resident((D_MODEL, tail)),
                  pl.BlockSpec((1, IDX_DIM), const), pl.BlockSpec((1, IDX_DIM), const)],
        out_specs=[pl.BlockSpec((tm, PROJ_TILE), row)] * 3 + [pl.BlockSpec((tm, IDX_DIM), row)],
        out_shape=[jax.ShapeDtypeStruct((n, PROJ_TILE), F32)] * 3 + [jax.ShapeDtypeStruct((n, IDX_DIM), F32)],
        compiler_params=params,
        name="in_proj_tail",
    )(x2d, g2, w_cat[:, ZA_WIDTH:], idx_k_g.reshape(1, -1), idx_k_b.reshape(1, -1))
    return zA, kd, vd, misc, kin


def _gla_consts(c):
    nlev = int(math.log2(c))
    t = np.arange(c)[:, None]
    s = np.arange(c)[None, :]
    mats = [(s <= t), np.ones((c, c), bool)]
    masks = []
    for l in range(nlev):
        mid = ((t >> (l + 1)) << (l + 1)) + (1 << l) - 1
        mats.append(s <= mid)
        masks.append(((t >> (l + 1)) == (s >> (l + 1))) & (((t >> l) & 1) == 1) & (((s >> l) & 1) == 0))
    masks.append(t == s)
    return (jnp.asarray(np.concatenate(mats, 0).astype(np.float32), BF16),
            jnp.asarray(np.stack(masks).astype(np.float32), F32), nlev)


_NT = (((1,), (1,)), ((), ()))
_TN = (((0,), (0,)), ((), ()))


def _gla_body(q_ref, k_ref, v_ref, misc_ref, wgu_ref, bg_ref, gn_ref, mst_ref, lmask_ref, s0_ref,
              o_ref, sout_ref, s_scr, la_scr, *, c, nc, nlev, valid_rows, bb):
    step = pl.program_id(1)
    tile = c * nc

    @pl.when(step == 0)
    def _():
        for bi in range(bb):
            for h in range(GLA_HEADS):
                s_scr[bi, h] = s0_ref[bi, h].T

    for bi in range(bb):
        x = jnp.dot(misc_ref[bi].astype(BF16), wgu_ref[...], preferred_element_type=F32) + bg_ref[...]
        log_a = (jnp.minimum(x, 0.0) - jnp.log1p(jnp.exp(-jnp.abs(x)))) * (1.0 / GLA_GATE_TAU)
        if valid_rows < tile:
            log_a = jnp.where(lax.broadcasted_iota(I32, log_a.shape, 0) < valid_rows, log_a, 0.0)
        la_scr[bi] = log_a
    scale = GLA_HEAD_DK ** -0.5

    def chunk(ci, carry):
        r0 = pl.multiple_of(ci * c, c)
        rows = pl.ds(r0, c)
        chains = [(bi, h) for bi in range(bb) for h in range(GLA_HEADS)]
        cs = []
        for bi in range(bb):
            g_all = la_scr[bi, rows, :]
            g_hi = g_all.astype(BF16)
            g_lo = (g_all - g_hi.astype(F32)).astype(BF16)
            t = jnp.dot(mst_ref[...], jnp.concatenate([g_hi, g_lo], axis=1), preferred_element_type=F32)
            cs.append(t[:, :GLA_DK] + t[:, GLA_DK:])
        ksl = lambda h: slice(h * GLA_HEAD_DK, (h + 1) * GLA_HEAD_DK)
        vsl = lambda h: slice(h * GLA_HEAD_DV, (h + 1) * GLA_HEAD_DV)
        qs, ks, bs, els, o_inter, att = [], [], [], [], [], []
        for bi, h in chains:
            q = q_ref[bi, rows, ksl(h)] * scale
            k = k_ref[bi, rows, ksl(h)]
            b = cs[bi][0:c, ksl(h)]
            qs.append(q), ks.append(k), bs.append(b), els.append(cs[bi][c:2 * c, ksl(h)])
            o_inter.append(lax.dot_general((q * jnp.exp(b)).astype(BF16), s_scr[bi, h].astype(BF16), _NT,
                                           preferred_element_type=F32))
            a = lmask_ref[nlev] * lax.dot_general(q.astype(BF16), k.astype(BF16), _NT,
                                                  preferred_element_type=F32)
            for l in range(nlev):
                e = cs[bi][(2 + l) * c:(3 + l) * c, ksl(h)]
                ql = (q * jnp.exp(jnp.minimum(b - e, 0.0))).astype(BF16)
                kl = (k * jnp.exp(jnp.minimum(e - b, 0.0))).astype(BF16)
                a = a + lmask_ref[l] * lax.dot_general(ql, kl, _NT, preferred_element_type=F32)
            att.append(a)
        vs = [v_ref[bi, rows, vsl(h)].astype(BF16) for bi, h in chains]
        outs = [o_inter[n] + jnp.dot(att[n].astype(BF16), vs[n], preferred_element_type=F32)
                for n in range(len(chains))]
        for n, (bi, h) in enumerate(chains):
            k_st = (ks[n] * jnp.exp(els[n] - bs[n])).astype(BF16)
            s_scr[bi, h] = s_scr[bi, h] * jnp.exp(els[n][0:1, :]) + lax.dot_general(
                vs[n], k_st, _TN, preferred_element_type=F32)
            o_ref[bi, rows, vsl(h)] = _rms(outs[n], gn_ref[...])
        return carry

    lax.fori_loop(0, nc, chunk, 0)

    @pl.when(step == pl.num_programs(1) - 1)
    def _():
        for bi in range(bb):
            for h in range(GLA_HEADS):
                sout_ref[bi, h] = s_scr[bi, h].T


def _gla(q_arr, k_arr, v_arr, misc_arr, cols, wgu_pad, b_gate, gla_norm_g, s0, c, tile, valid_rows, bb):
    nb, l = q_arr.shape[0], q_arr.shape[1]
    assert nb % bb == 0 and l % tile == 0
    mst, lmask, nlev = _gla_consts(c)
    const2 = lambda b, s: (0, 0)
    state_spec = pl.BlockSpec((bb, GLA_HEADS, GLA_HEAD_DK, GLA_HEAD_DV), lambda b, s: (b, 0, 0, 0))
    body = functools.partial(_gla_body, c=c, nc=tile // c, nlev=nlev, valid_rows=valid_rows, bb=bb)
    return pl.pallas_call(
        body,
        grid=(nb // bb, l // tile),
        in_specs=[pl.BlockSpec((bb, tile, GLA_DK), lambda b, s: (b, s, cols[0])),
                  pl.BlockSpec((bb, tile, GLA_DK), lambda b, s: (b, s, cols[1])),
                  pl.BlockSpec((bb, tile, GLA_DV), lambda b, s: (b, s, cols[2])),
                  pl.BlockSpec((bb, tile, LANES), lambda b, s: (b, s, 0)),
                  pl.BlockSpec((LANES, GLA_DK), const2),
                  pl.BlockSpec((1, GLA_DK), const2),
                  pl.BlockSpec((1, GLA_HEAD_DV), const2),
                  pl.BlockSpec(mst.shape, const2),
                  pl.BlockSpec(lmask.shape, lambda b, s: (0, 0, 0)),
                  state_spec],
        out_specs=[pl.BlockSpec((bb, tile, GLA_DV), lambda b, s: (b, s, 0)), state_spec],
        out_shape=[jax.ShapeDtypeStruct((nb, l, GLA_DV), F32),
                   jax.ShapeDtypeStruct((nb, GLA_HEADS, GLA_HEAD_DK, GLA_HEAD_DV), F32)],
        scratch_shapes=[pltpu.VMEM((bb, GLA_HEADS, GLA_HEAD_DV, GLA_HEAD_DK), F32),
                        pltpu.VMEM((bb, tile, GLA_DK), F32)],
        compiler_params=pltpu.CompilerParams(dimension_semantics=("arbitrary", "arbitrary"),
                                             vmem_limit_bytes=VMEM_LIMIT),
        name="gla",
    )(q_arr, k_arr, v_arr, misc_arr, wgu_pad, b_gate.reshape(1, -1), gla_norm_g.reshape(1, -1),
      mst, lmask, s0)


def _score_stats_init(w):
    inf = jnp.full((SUBLANES, w), jnp.inf, F32)
    zero = jnp.zeros((SUBLANES, w), I32)
    return -inf, inf, zero, zero


def _score_stats_update(carry, blk):
    mx, mn, c_pos, c_nn = carry
    b3 = blk.reshape(blk.shape[0] // SUBLANES, SUBLANES, blk.shape[1])
    return (jnp.maximum(mx, jnp.max(b3, axis=0)),
            jnp.minimum(mn, jnp.min(jnp.where(b3 == -jnp.inf, jnp.inf, b3), axis=0)),
            c_pos + jnp.sum((b3 > 0.0).astype(I32), axis=0),
            c_nn + jnp.sum((b3 >= 0.0).astype(I32), axis=0))


def _select_threshold(sc_ref, tri_ref, nt, tr, topk, small, lane_ok, stats=None):
    w = sc_ref.shape[1]
    inf = jnp.float32(jnp.inf)

    def over_tiles(fn, init):
        def body(i, carry):
            r0 = pl.multiple_of(i * tr, tr)
            return fn(carry, sc_ref[pl.ds(r0, tr), :], r0)
        return lax.fori_loop(0, nt, body, init)

    fold = lambda x: x.reshape(tr // SUBLANES, SUBLANES, w)
    zeros8 = jnp.zeros((SUBLANES, w), I32)
    pinf8 = jnp.full((SUBLANES, w), inf, F32)

    def count(pred):
        acc = over_tiles(lambda a, blk, r0: a + jnp.sum(fold(pred(blk, r0).astype(I32)), axis=0), zeros8)
        return jnp.sum(acc, axis=0, keepdims=True)

    def min_where(pred):
        acc = over_tiles(lambda a, blk, r0: jnp.minimum(
            a, jnp.min(fold(jnp.where(pred(blk, r0), blk, inf)), axis=0)), pinf8)
        return jnp.min(acc, axis=0, keepdims=True)

    if stats is None:
        stats = over_tiles(lambda carry, blk, r0: _score_stats_update(carry, blk), _score_stats_init(w))
    mx8, mn8, cp8, cn8 = stats
    c_pos = jnp.sum(cp8, axis=0, keepdims=True)
    c_nn = jnp.sum(cn8, axis=0, keepdims=True)
    hi = jnp.where(c_pos >= topk, jnp.max(mx8, axis=0, keepdims=True), 0.0)
    lo = jnp.where(c_nn >= topk, 0.0, jnp.min(mn8, axis=0, keepdims=True))

    live = lane_ok & jnp.logical_not(small)

    def bisect(_, carry):
        lo, hi = carry
        mid = 0.5 * lo + 0.5 * hi
        ge = count(lambda blk, r0: blk >= mid) >= topk
        return jnp.where(ge, mid, lo), jnp.where(ge, hi, mid)

    lo, hi = lax.fori_loop(0, BISECT_STEPS, bisect, (lo, hi))
    v0 = min_where(lambda blk, r0: blk >= lo)

    def gt_next(v):
        def f(carry, blk, r0):
            cg, nx = carry
            gt = blk > v
            return (cg + jnp.sum(fold(gt.astype(I32)), axis=0),
                    jnp.minimum(nx, jnp.min(fold(jnp.where(gt, blk, inf)), axis=0)))
        cg8, nx8 = over_tiles(f, (zeros8, pinf8))
        return jnp.sum(cg8, axis=0, keepdims=True), jnp.min(nx8, axis=0, keepdims=True)

    def peel(state):
        v, _, _ = state
        cg, nx = gt_next(v)
        move = (cg >= topk) & live
        return jnp.where(move, nx, v), cg, jnp.max(move.astype(I32))

    v, cnt_gt, _ = lax.while_loop(lambda s: s[2] > 0, peel,
                                  (v0, jnp.zeros((1, w), I32), jnp.int32(1)))
    cnt_ge = count(lambda blk, r0: blk >= v)
    excess = (cnt_ge > topk) & live

    @pl.when(jnp.max(excess.astype(I32)) > 0)
    def _():
        need = jnp.where(excess, (topk - cnt_gt).astype(F32), jnp.float32(2 ** 30))

        tb = tri_ref.shape[0]

        def drop_surplus(i, seen):
            r0 = pl.multiple_of(i * tr, tr)
            blk = sc_ref[pl.ds(r0, tr), :]
            eq = blk == v
            ones = jnp.where(eq, 1.0, 0.0).astype(BF16)
            ranks = [jnp.dot(tri_ref[...], ones[s * tb:(s + 1) * tb, :], preferred_element_type=F32)
                     for s in range(tr // tb)]
            for s in range(tr // tb):
                rows = slice(s * tb, (s + 1) * tb)
                rank = ranks[s] + seen
                sc_ref[pl.ds(r0 + s * tb, tb), :] = jnp.where(eq[rows], jnp.where(rank > need, -inf, blk[rows]),
                                                              blk[rows])
                seen = rank[tb - 1:tb, :]
            return seen

        lax.fori_loop(0, nt, drop_surplus, jnp.zeros((1, w), F32))

    return jnp.where(small, -inf, v)


def _dsa_prompt_body(relb_ref, btab_ref, tri_ref, ki_ref, qi_ref, wi_ref, kd_ref, vt_ref, qd_ref, o_ref,
                     sc_scr, tbl_scr, q2_scr, acc_scr, m_scr, o_scr, lg_scr, pr_scr, *,
                     topk, far_bucket):
    b = pl.program_id(0)
    j = pl.program_id(1)
    tk = DSA_TK
    hd = DSA_HEAD_DIM

    @pl.when((b == 0) & (j == 0))
    def _():
        q2_scr[...] = jnp.zeros(q2_scr.shape, BF16)

        def build(ci, carry):
            r0 = pl.multiple_of(ci * LANES, LANES)
            bt = btab_ref[pl.ds(r0, LANES), :]
            for h in range(DSA_HEADS):
                t = jnp.zeros(bt.shape, F32)
                for bk in range(REL_BUCKETS):
                    t = jnp.where(bt == bk, (relb_ref[bk, h] - relb_ref[far_bucket, h]) * LOG2E, t)
                tbl_scr[h, pl.ds(r0, LANES), :] = t
            return carry

        lax.fori_loop(0, BIAS_TAB_ROWS // LANES, build, 0)

    nt = j // (tk // QUERY_BLOCK) + 1
    qpos = j * QUERY_BLOCK + lax.broadcasted_iota(I32, (1, LANES), 1)
    for p in range(DSA_HEADS // 2):
        q2_scr[p, 0:hd, 0:LANES] = qd_ref[2 * p * hd:(2 * p + 1) * hd, :]
        q2_scr[p, hd:2 * hd, LANES:2 * LANES] = qd_ref[(2 * p + 1) * hd:(2 * p + 2) * hd, :]

    wi = wi_ref[...]

    def score_tile(i, carry):
        r0 = pl.multiple_of(i * tk, tk)
        s4 = jnp.dot(ki_ref[pl.ds(r0, tk), :], qi_ref[...], preferred_element_type=F32)
        sc = jnp.zeros((tk, LANES), F32)
        for h in range(IDX_HEADS):
            sc = sc + jnp.maximum(s4[:, h * LANES:(h + 1) * LANES], 0.0) * wi[h:h + 1, :]
        kpos = r0 + lax.broadcasted_iota(I32, (tk, LANES), 0)
        sc = jnp.where(kpos <= qpos, sc, -jnp.inf)
        sc_scr[pl.ds(r0, tk), :] = sc
        return _score_stats_update(carry, sc)

    stats = lax.fori_loop(0, nt, score_tile, _score_stats_init(LANES))

    small = (qpos + 1) < topk
    kstar = _select_threshold(sc_scr, tri_ref, nt, tk, topk, small, jnp.full((1, LANES), True), stats)

    m_scr[...] = jnp.full(m_scr.shape, NEG, F32)
    acc_scr[...] = jnp.zeros(acc_scr.shape, F32)
    vrows = acc_scr.shape[0] // DSA_HEADS

    def logits_stage(i, buf, near):
        r0 = pl.multiple_of(i * tk, tk)
        blk = sc_scr[pl.ds(r0, tk), :]
        if near:
            kpos = r0 + lax.broadcasted_iota(I32, (tk, LANES), 0)
            addm = jnp.where(blk >= kstar, jnp.where(kpos <= qpos, 0.0, NEG), NEG)
            off = pl.multiple_of(i * tk - j * QUERY_BLOCK + BIAS_TAB_OFF, LANES)
        else:
            addm = jnp.where(blk >= kstar, 0.0, NEG)
        tile_max = []
        for p in range(DSA_HEADS // 2):
            lg2 = jnp.dot(kd_ref[pl.ds(r0, tk), p * LANES:(p + 1) * LANES], q2_scr[p],
                          preferred_element_type=F32)
            for hh in range(2):
                h = 2 * p + hh
                lg = lg2[:, hh * LANES:(hh + 1) * LANES] + addm
                if near:
                    lg = lg + tbl_scr[h, pl.ds(off, tk), :]
                lg_scr[buf, h] = lg
                tile_max.append(jnp.max(lg, axis=0, keepdims=True))
        return tuple(tile_max)

    def softmax_pv_stage(i, buf, tile_max):
        alpha = []
        for h in range(DSA_HEADS):
            m_old = m_scr[h:h + 1, :]
            m_new = jnp.maximum(m_old, tile_max[h])
            m_scr[h:h + 1, :] = m_new
            alpha.append(jnp.exp2(m_old - m_new))
            pr_scr[h] = jnp.exp2((lg_scr[buf, h] - m_new).astype(BF16))
        for h in range(DSA_HEADS):
            rows = slice(h * vrows, (h + 1) * vrows)
            acc_scr[rows, :] = alpha[h] * acc_scr[rows, :] + jnp.dot(
                vt_ref[i, rows, :], pr_scr[h], preferred_element_type=F32)

    n_far = jnp.maximum(nt - 2, 0)

    @pl.when(n_far > 0)
    def _():
        n_pairs = (n_far - 1) // 2

        def pair(it, tile_max):
            i0 = 2 * it
            max1 = logits_stage(i0 + 1, 1, False)
            softmax_pv_stage(i0, 0, tile_max)
            max2 = logits_stage(i0 + 2, 0, False)
            softmax_pv_stage(i0 + 1, 1, max1)
            return max2

        tile_max = lax.fori_loop(0, n_pairs, pair, logits_stage(0, 0, False))
        last = 2 * n_pairs

        @pl.when(n_far - last == 1)
        def _():
            softmax_pv_stage(last, 0, tile_max)

        @pl.when(n_far - last == 2)
        def _():
            max1 = logits_stage(last + 1, 1, False)
            softmax_pv_stage(last, 0, tile_max)
            softmax_pv_stage(last + 1, 1, max1)

    def near_tile(i, carry):
        softmax_pv_stage(i, 0, logits_stage(i, 0, True))
        return carry

    lax.fori_loop(n_far, nt, near_tile, 0)

    for h in range(DSA_HEADS):
        o_scr[h * hd:(h + 1) * hd, :] = (acc_scr[h * vrows:h * vrows + hd, :]
                                         / acc_scr[h * vrows + hd:h * vrows + hd + 1, :])
    o_ref[...] = o_scr[...].T.astype(BF16)


def _tri_ones(n):
    return jnp.asarray(np.tril(np.ones((n, n), np.float32)), BF16)


def _bias_bucket_table():
    u = np.arange(BIAS_TAB_ROWS)[:, None]
    r = np.arange(LANES)[None, :]
    return jnp.asarray(_t5_bucket_np(r + BIAS_TAB_OFF - u), I32)


def _dsa_prompt(zA, kd, vd, misc, kin, rel_bias, nb, l):
    nq = l // QUERY_BLOCK
    topk = min(TOPK_MAX, l // 4)
    assert l % DSA_TK == 0
    far = _t5_bucket_np(np.arange(REL_MAX_DIST, max(l, REL_MAX_DIST + 1)))
    assert (far == far[0]).all()
    ki = kin.reshape(nb, l, IDX_DIM).astype(BF16)
    qi = misc[:, MISC_QI:MISC_QI + IDX_HEADS * IDX_DIM].reshape(nb, nq, QUERY_BLOCK, IDX_HEADS, IDX_DIM)
    qi = qi.transpose(0, 4, 1, 3, 2).reshape(nb, IDX_DIM, nq * IDX_HEADS * QUERY_BLOCK).astype(BF16)
    wi = misc[:, MISC_WI:MISC_WI + IDX_HEADS] * ((IDX_DIM ** -0.5) * (IDX_HEADS ** -0.5))
    wi = wi.reshape(nb, nq, QUERY_BLOCK, IDX_HEADS).transpose(0, 1, 3, 2)
    wi = jnp.pad(wi, ((0, 0), (0, 0), (0, SUBLANES - IDX_HEADS), (0, 0))).reshape(nb, nq * SUBLANES, QUERY_BLOCK)
    kd_bf = kd.reshape(nb, l, DSA_WIDTH).astype(BF16)
    vt = vd.reshape(nb, l, DSA_HEADS, DSA_HEAD_DIM).transpose(0, 2, 3, 1)
    vt = jnp.concatenate([vt, jnp.ones((nb, DSA_HEADS, 1, l), vt.dtype),
                          jnp.zeros((nb, DSA_HEADS, V_ROWS - DSA_HEAD_DIM - 1, l), vt.dtype)], axis=2)
    vt = vt.reshape(nb, DSA_HEADS * V_ROWS, l // DSA_TK, DSA_TK).transpose(0, 2, 1, 3).astype(BF16)
    qd = (zA[:, ZA_QD:ZA_QD + DSA_WIDTH] * ((DSA_HEAD_DIM ** -0.5) * LOG2E)).reshape(nb, l, DSA_WIDTH)
    qd = qd.transpose(0, 2, 1).astype(BF16)
    body = functools.partial(_dsa_prompt_body, topk=topk, far_bucket=int(far[0]))
    whole = lambda b, j: (b, 0, 0)
    return pl.pallas_call(
        body,
        grid=(nb, nq),
        in_specs=[pl.BlockSpec(memory_space=pltpu.SMEM),
                  pl.BlockSpec((BIAS_TAB_ROWS, LANES), lambda b, j: (0, 0), pipeline_mode=pl.Buffered(1)),
                  pl.BlockSpec((LANES, LANES), lambda b, j: (0, 0), pipeline_mode=pl.Buffered(1)),
                  pl.BlockSpec((None, l, IDX_DIM), whole, pipeline_mode=pl.Buffered(1)),
                  pl.BlockSpec((None, IDX_DIM, IDX_HEADS * QUERY_BLOCK), lambda b, j: (b, 0, j)),
                  pl.BlockSpec((None, SUBLANES, QUERY_BLOCK), lambda b, j: (b, j, 0)),
                  pl.BlockSpec((None, l, DSA_WIDTH), whole, pipeline_mode=pl.Buffered(1)),
                  pl.BlockSpec((None, l // DSA_TK, DSA_HEADS * V_ROWS, DSA_TK), lambda b, j: (b, 0, 0, 0),
                               pipeline_mode=pl.Buffered(1)),
                  pl.BlockSpec((None, DSA_WIDTH, QUERY_BLOCK), lambda b, j: (b, 0, j))],
        out_specs=pl.BlockSpec((None, QUERY_BLOCK, DSA_WIDTH), lambda b, j: (b, j, 0)),
        out_shape=jax.ShapeDtypeStruct((nb, l, DSA_WIDTH), BF16),
        scratch_shapes=[pltpu.VMEM((l, LANES), F32),
                        pltpu.VMEM((DSA_HEADS, BIAS_TAB_ROWS, LANES), F32),
                        pltpu.VMEM((DSA_HEADS // 2, LANES, 2 * LANES), BF16),
                        pltpu.VMEM((DSA_HEADS * V_ROWS, LANES), F32),
                        pltpu.VMEM((DSA_HEADS, LANES), F32),
                        pltpu.VMEM((DSA_WIDTH, LANES), F32),
                        pltpu.VMEM((2, DSA_HEADS, DSA_TK, LANES), F32),
                        pltpu.VMEM((DSA_HEADS, DSA_TK, LANES), BF16)],
        compiler_params=pltpu.CompilerParams(dimension_semantics=("arbitrary", "arbitrary"),
                                             vmem_limit_bytes=VMEM_LIMIT),
        name="dsa_prompt",
    )(rel_bias, _bias_bucket_table(), _tri_ones(LANES), ki, qi, wi, kd_bf, vt, qd)


def _select_threshold_lanes(sc_ref, triu_ref, nt, topk, small, row_ok):
    r = sc_ref.shape[0]
    inf = jnp.float32(jnp.inf)
    tiles = [slice(i * LANES, (i + 1) * LANES) for i in range(nt)]
    rowsum = lambda x: jnp.sum(x, axis=1, keepdims=True)
    rowmin = lambda x: jnp.min(x, axis=1, keepdims=True)

    def count(pred):
        acc = jnp.zeros((r, LANES), I32)
        for t in tiles:
            acc = acc + pred(sc_ref[:, t]).astype(I32)
        return rowsum(acc)

    def min_where(pred):
        acc = jnp.full((r, LANES), inf, F32)
        for t in tiles:
            blk = sc_ref[:, t]
            acc = jnp.minimum(acc, jnp.where(pred(blk), blk, inf))
        return rowmin(acc)

    mx = jnp.full((r, LANES), -inf, F32)
    mn = jnp.full((r, LANES), inf, F32)
    c_pos = jnp.zeros((r, LANES), I32)
    c_nn = jnp.zeros((r, LANES), I32)
    for t in tiles:
        blk = sc_ref[:, t]
        mx = jnp.maximum(mx, blk)
        mn = jnp.minimum(mn, jnp.where(blk == -inf, inf, blk))
        c_pos = c_pos + (blk > 0.0).astype(I32)
        c_nn = c_nn + (blk >= 0.0).astype(I32)
    hi = jnp.where(rowsum(c_pos) >= topk, jnp.max(mx, axis=1, keepdims=True), 0.0)
    lo = jnp.where(rowsum(c_nn) >= topk, 0.0, rowmin(mn))
    live = row_ok & jnp.logical_not(small)

    def bisect(_, carry):
        lo, hi = carry
        mid = 0.5 * lo + 0.5 * hi
        ge = count(lambda blk: blk >= mid) >= topk
        return jnp.where(ge, mid, lo), jnp.where(ge, hi, mid)

    lo, hi = lax.fori_loop(0, BISECT_STEPS, bisect, (lo, hi))
    v0 = min_where(lambda blk: blk >= lo)

    def peel(state):
        v, _, _ = state
        cg = jnp.zeros((r, LANES), I32)
        nx = jnp.full((r, LANES), inf, F32)
        for t in tiles:
            blk = sc_ref[:, t]
            gt = blk > v
            cg = cg + gt.astype(I32)
            nx = jnp.minimum(nx, jnp.where(gt, blk, inf))
        cg = rowsum(cg)
        move = (cg >= topk) & live
        return jnp.where(move, rowmin(nx), v), cg, jnp.max(move.astype(I32))

    v, cnt_gt, _ = lax.while_loop(lambda s: s[2] > 0, peel,
                                  (v0, jnp.zeros((r, 1), I32), jnp.int32(1)))
    excess = (count(lambda blk: blk >= v) > topk) & live

    @pl.when(jnp.max(excess.astype(I32)) > 0)
    def _():
        need = (topk - cnt_gt).astype(F32)
        seen = jnp.zeros((r, 1), F32)
        for t in tiles:
            blk = sc_ref[:, t]
            eq = blk == v
            rank = seen + jnp.dot(jnp.where(eq, 1.0, 0.0).astype(BF16), triu_ref[...],
                                  preferred_element_type=F32)
            sc_ref[:, t] = jnp.where(eq & (rank > need) & excess, -inf, blk)
            seen = rank[:, LANES - 1:LANES]

    return jnp.where(small, -inf, v)


SEQ_GROUP = LANES // SUBLANES


def _dsa_sample_select_body(pt_ref, *refs, n_pages, t_new, topk):
    page_refs = refs[:n_pages]
    knew_ref, qi_ref, wi_ref, triu_ref, mask_ref, sc_scr = refs[n_pages:]
    g = pl.program_id(1)
    past = n_pages * PAGE_SIZE
    rows = pl.ds(pl.multiple_of(g * SUBLANES, SUBLANES), SUBLANES)
    for p in range(n_pages + 1):
        keys_t = (page_refs[p][...] if p < n_pages else knew_ref[...]).astype(BF16)
        s = jnp.dot(qi_ref[...], keys_t, preferred_element_type=F32)
        sc = jnp.zeros((SUBLANES, PAGE_SIZE), F32)
        for h in range(IDX_HEADS):
            hs = slice(h * SUBLANES, (h + 1) * SUBLANES)
            sc = sc + jnp.maximum(s[hs, :], 0.0) * wi_ref[hs, :]
        sc_scr[rows, p * PAGE_SIZE:(p + 1) * PAGE_SIZE] = sc

    @pl.when(g == SEQ_GROUP - 1)
    def _():
        q_of_row = lax.broadcasted_iota(I32, (LANES, 1), 0) % SUBLANES
        new = slice(past, past + PAGE_SIZE)
        cpos = lax.broadcasted_iota(I32, (LANES, PAGE_SIZE), 1)
        sc_scr[:, new] = jnp.where(cpos <= q_of_row, sc_scr[:, new], -jnp.inf)
        row_ok = q_of_row < t_new
        small = (past + q_of_row + 1) < topk
        kstar = _select_threshold_lanes(sc_scr, triu_ref, n_pages + 1, topk, small, row_ok)
        for p in range(n_pages + 1):
            t = slice(p * PAGE_SIZE, (p + 1) * PAGE_SIZE)
            mask_ref[:, t] = jnp.where(sc_scr[:, t] >= kstar, 1.0, 0.0)


def _dsa_sample_select(page_table, kidx_t, kin_new_t, qi_rows, wi_rows, t_new, topk):
    db, n_pages = page_table.shape
    ng = db // SEQ_GROUP
    width = (n_pages + 1) * PAGE_SIZE
    seq = lambda gi, g, pt: (gi * SEQ_GROUP + g, 0, 0)
    page_specs = [pl.BlockSpec((None, IDX_DIM, PAGE_SIZE),
                               functools.partial(lambda gi, g, pt, p: (pt[gi * SEQ_GROUP + g, p], 0, 0), p=p))
                  for p in range(n_pages)]
    body = functools.partial(_dsa_sample_select_body, n_pages=n_pages, t_new=t_new, topk=topk)
    grid_spec = pltpu.PrefetchScalarGridSpec(
        num_scalar_prefetch=1,
        grid=(ng, SEQ_GROUP),
        in_specs=page_specs + [pl.BlockSpec((None, IDX_DIM, PAGE_SIZE), seq),
                               pl.BlockSpec((None, IDX_HEADS * SUBLANES, IDX_DIM), seq),
                               pl.BlockSpec((None, IDX_HEADS * SUBLANES, PAGE_SIZE), seq),
                               pl.BlockSpec((LANES, LANES), lambda gi, g, pt: (0, 0))],
        out_specs=pl.BlockSpec((None, LANES, width), lambda gi, g, pt: (gi, 0, 0)),
        scratch_shapes=[pltpu.VMEM((LANES, width), F32)],
    )
    triu = jnp.asarray(np.triu(np.ones((LANES, LANES), np.float32)), BF16)
    return pl.pallas_call(
        body, grid_spec=grid_spec,
        out_shape=jax.ShapeDtypeStruct((ng, LANES, width), F32),
        compiler_params=pltpu.CompilerParams(dimension_semantics=("arbitrary", "arbitrary"),
                                             vmem_limit_bytes=VMEM_LIMIT),
        name="dsa_sample_select",
    )(page_table, *([kidx_t] * n_pages), kin_new_t, qi_rows, wi_rows, triu)


def _dsa_sample_attend_body(pt_ref, *refs, n_pages):
    k_refs = refs[:n_pages]
    v_refs = refs[n_pages:2 * n_pages]
    (knew_ref, vnew_ref, mask_ref, qbd_ref, btab_ref, rb_ref, o_ref, lg_scr, tbl_scr) = refs[2 * n_pages:]
    b = pl.program_id(0)
    hq = DSA_HEADS * SUBLANES

    @pl.when(b == 0)
    def _():
        for half in range(2):
            cols = slice(half * PAGE_SIZE, (half + 1) * PAGE_SIZE)
            bt = btab_ref[:, cols]
            t = jnp.zeros(bt.shape, F32)
            for bk in range(REL_BUCKETS):
                t = jnp.where(bt == bk, rb_ref[bk], t)
            tbl_scr[:, cols] = t

    far_bias = rb_ref[REL_BUCKETS - 1]
    page = lambda ref: ref[...].reshape(DSA_WIDTH, PAGE_SIZE).astype(BF16)

    m = jnp.full((hq, LANES), NEG, F32)
    for p in range(n_pages + 1):
        cols = slice(p * PAGE_SIZE, (p + 1) * PAGE_SIZE)
        k_t = page(k_refs[p]) if p < n_pages else knew_ref[...].astype(BF16)
        lg = jnp.dot(qbd_ref[...], k_t, preferred_element_type=F32)
        if p >= n_pages - 1:
            lg = lg + tbl_scr[:, (p - n_pages + 1) * PAGE_SIZE:(p - n_pages + 2) * PAGE_SIZE]
        else:
            lg = lg + far_bias
        sel = jnp.concatenate([mask_ref[:, cols]] * DSA_HEADS, axis=0)
        lg = lg + jnp.where(sel > 0.5, 0.0, NEG)
        lg_scr[:, cols] = lg
        m = jnp.maximum(m, lg)
    m = jnp.max(m, axis=1, keepdims=True)

    acc = jnp.zeros((hq, DSA_WIDTH), F32)
    lsum = jnp.zeros((hq, LANES), F32)
    for p in range(n_pages + 1):
        cols = slice(p * PAGE_SIZE, (p + 1) * PAGE_SIZE)
        v_t = page(v_refs[p]) if p < n_pages else vnew_ref[...].astype(BF16)
        pr = jnp.exp(lg_scr[:, cols] - m)
        lsum = lsum + pr
        acc = acc + lax.dot_general(pr.astype(BF16), v_t, _NT, preferred_element_type=F32)
    acc = acc / jnp.sum(lsum, axis=1, keepdims=True)
    head_of_lane = lax.broadcasted_iota(I32, (SUBLANES, DSA_WIDTH), 1) // DSA_HEAD_DIM
    out = jnp.zeros((SUBLANES, DSA_WIDTH), F32)
    for h in range(DSA_HEADS):
        out = jnp.where(head_of_lane == h, acc[h * SUBLANES:(h + 1) * SUBLANES, :], out)
    o_ref[...] = out


def _dsa_sample_attend(page_table, k_t, v_t, k_new_t, v_new_t, mask, qbd, rel_bias):
    db, n_pages = page_table.shape
    hq = DSA_HEADS * SUBLANES
    width = (n_pages + 1) * PAGE_SIZE
    seq = lambda b, pt: (b, 0, 0)
    page = lambda p: functools.partial(lambda b, pt, p: (pt[b, p], 0, 0, 0), p=p)
    kv_spec = lambda p: pl.BlockSpec((None, DSA_HEADS, DSA_HEAD_DIM, PAGE_SIZE), page(p))
    u = np.arange(2 * PAGE_SIZE)[None, :]
    q = (np.arange(hq) % SUBLANES)[:, None]
    btab = jnp.asarray(_t5_bucket_np(PAGE_SIZE + q - u), I32)
    rb = jnp.broadcast_to(jnp.repeat(rel_bias, SUBLANES, axis=1)[:, :, None], (REL_BUCKETS, hq, PAGE_SIZE))
    grid_spec = pltpu.PrefetchScalarGridSpec(
        num_scalar_prefetch=1,
        grid=(db,),
        in_specs=[kv_spec(p) for p in range(n_pages)] + [kv_spec(p) for p in range(n_pages)] + [
            pl.BlockSpec((None, DSA_WIDTH, PAGE_SIZE), seq),
            pl.BlockSpec((None, DSA_WIDTH, PAGE_SIZE), seq),
            pl.BlockSpec((None, SUBLANES, width), lambda b, pt: (b // SEQ_GROUP, b % SEQ_GROUP, 0)),
            pl.BlockSpec((None, hq, DSA_WIDTH), seq),
            pl.BlockSpec(btab.shape, lambda b, pt: (0, 0)),
            pl.BlockSpec(rb.shape, lambda b, pt: (0, 0, 0))],
        out_specs=pl.BlockSpec((None, SUBLANES, DSA_WIDTH), seq),
        scratch_shapes=[pltpu.VMEM((hq, width), F32), pltpu.VMEM((hq, 2 * PAGE_SIZE), F32)],
    )
    return pl.pallas_call(
        functools.partial(_dsa_sample_attend_body, n_pages=n_pages), grid_spec=grid_spec,
        out_shape=jax.ShapeDtypeStruct((db, SUBLANES, DSA_WIDTH), F32),
        compiler_params=pltpu.CompilerParams(dimension_semantics=("arbitrary",),
                                             vmem_limit_bytes=VMEM_LIMIT),
        name="dsa_sample_attend",
    )(page_table, *([k_t] * n_pages), *([v_t] * n_pages), k_new_t, v_new_t, mask, qbd, btab, rb)


def _dsa_sample(zA, kd, vd, misc, kin, rel_bias, cache_k, cache_v, cache_kidx, page_table, db, t_new):
    n_pages = page_table.shape[1]
    past = n_pages * PAGE_SIZE
    topk = min(TOPK_MAX, (past + t_new) // 4)
    assert db % SEQ_GROUP == 0 and t_new <= SUBLANES
    new_t = lambda a, w: jnp.pad(a.reshape(db, t_new, w).transpose(0, 2, 1),
                                 ((0, 0), (0, 0), (0, PAGE_SIZE - t_new)))
    pad_q = lambda a: jnp.pad(a, ((0, 0), (0, 0), (0, SUBLANES - t_new), (0, 0)))
    qi = misc[:, MISC_QI:MISC_QI + IDX_HEADS * IDX_DIM].reshape(db, t_new, IDX_HEADS, IDX_DIM)
    qi_rows = pad_q(qi.transpose(0, 2, 1, 3)).reshape(db, IDX_HEADS * SUBLANES, IDX_DIM).astype(BF16)
    wi = misc[:, MISC_WI:MISC_WI + IDX_HEADS] * ((IDX_DIM ** -0.5) * (IDX_HEADS ** -0.5))
    wi = pad_q(wi.reshape(db, t_new, IDX_HEADS, 1).transpose(0, 2, 1, 3)).reshape(db, IDX_HEADS * SUBLANES, 1)
    wi_rows = jnp.broadcast_to(wi, (db, IDX_HEADS * SUBLANES, PAGE_SIZE))
    mask = _dsa_sample_select(page_table, cache_kidx.transpose(0, 2, 1), new_t(kin, IDX_DIM),
                              qi_rows, wi_rows, t_new, topk)
    qd = (zA[:, ZA_QD:ZA_QD + DSA_WIDTH] * (DSA_HEAD_DIM ** -0.5)).reshape(db, t_new, DSA_HEADS, DSA_HEAD_DIM)
    qd = pad_q(qd.transpose(0, 2, 1, 3))
    eye = jnp.eye(DSA_HEADS, dtype=qd.dtype)
    qbd = (qd[:, :, :, None, :] * eye[None, :, None, :, None]).reshape(db, DSA_HEADS * SUBLANES, DSA_WIDTH)
    o = _dsa_sample_attend(page_table, cache_k.transpose(0, 2, 3, 1), cache_v.transpose(0, 2, 3, 1),
                           new_t(kd, DSA_WIDTH), new_t(vd, DSA_WIDTH), mask, qbd.astype(BF16), rel_bias)
    return o[:, :t_new, :].reshape(db * t_new, DSA_WIDTH).astype(BF16)


def _post_body(x_ref, og_ref, rg_ref, gg_ref, gd_ref, od_ref, wg_ref, wd_ref, wo_ref, gf_ref,
               x1_ref, hf_ref):
    rg = rg_ref[...]
    a = (og_ref[...] * (rg * _sigmoid(rg))).astype(BF16)
    y_g = jnp.dot(a, wg_ref[...], preferred_element_type=F32)
    y_d = jnp.dot(od_ref[...], wd_ref[...], preferred_element_type=F32)
    mix = (_sigmoid(gg_ref[...]) * y_g + _sigmoid(gd_ref[...]) * y_d).astype(BF16)
    x1 = x_ref[...] + jnp.dot(mix, wo_ref[...], preferred_element_type=F32)
    x1_ref[...] = x1
    hf_ref[...] = _rms(x1, gf_ref[...]).astype(BF16)


def _post(x2d, og, zA, od, w_gla, w_dsa, w_o, g_ffn):
    n = x2d.shape[0]
    tm = min(n, 512)
    row = lambda i: (i, 0)
    const = lambda i: (0, 0)
    return pl.pallas_call(
        _post_body,
        grid=(n // tm,),
        in_specs=[pl.BlockSpec((tm, D_MODEL), row),
                  pl.BlockSpec((tm, GLA_DV), row),
                  pl.BlockSpec((tm, GLA_DV), lambda i: (i, COL_RG)),
                  pl.BlockSpec((tm, D_MODEL), lambda i: (i, COL_GG)),
                  pl.BlockSpec((tm, D_MODEL), lambda i: (i, COL_GD)),
                  pl.BlockSpec((tm, DSA_WIDTH), row),
                  pl.BlockSpec((GLA_DV, D_MODEL), const),
                  pl.BlockSpec((DSA_WIDTH, D_MODEL), const),
                  pl.BlockSpec((D_MODEL, D_MODEL), const),
                  pl.BlockSpec((1, D_MODEL), const)],
        out_specs=[pl.BlockSpec((tm, D_MODEL), row), pl.BlockSpec((tm, D_MODEL), row)],
        out_shape=[jax.ShapeDtypeStruct((n, D_MODEL), F32), jax.ShapeDtypeStruct((n, D_MODEL), BF16)],
        compiler_params=pltpu.CompilerParams(dimension_semantics=("arbitrary",),
                                             vmem_limit_bytes=VMEM_LIMIT),
        name="post_mix",
    )(x2d, og, zA, zA, zA, od, w_gla, w_dsa, w_o, g_ffn.reshape(1, -1))


FFN_TILE = 256
FFN_ROWS = 512


def _ffn_body(hf_ref, x1_ref, wg_ref, wu_ref, wd_ref, gfin_ref, y_ref, acc_scr):
    hf = hf_ref[...]
    n_chunks = wg_ref.shape[1] // FFN_TILE
    cols = lambda k: slice(k * FFN_TILE, (k + 1) * FFN_TILE)

    def gate_up(k):
        return (jnp.dot(hf, wg_ref[:, cols(k)], preferred_element_type=F32),
                jnp.dot(hf, wu_ref[:, cols(k)], preferred_element_type=F32))

    acc_scr[...] = x1_ref[...]
    nxt = gate_up(0)
    for k in range(n_chunks):
        gate, up = nxt
        if k + 1 < n_chunks:
            nxt = gate_up(k + 1)
        act = (gate * _sigmoid(gate) * up).astype(BF16)
        acc_scr[...] = acc_scr[...] + jnp.dot(act, wd_ref[cols(k), :], preferred_element_type=F32)
    y_ref[...] = _rms(acc_scr[...], gfin_ref[...])


def _ffn(hf, x1, w_gate, w_up, w_down, g_final):
    n = hf.shape[0]
    d_ff = w_gate.shape[1]
    assert d_ff % FFN_TILE == 0
    tm = min(n, FFN_ROWS)
    row = lambda i: (i, 0)
    const = lambda i: (0, 0)
    resident = lambda shape: pl.BlockSpec(shape, const, pipeline_mode=pl.Buffered(1))
    return pl.pallas_call(
        _ffn_body,
        grid=(n // tm,),
        in_specs=[pl.BlockSpec((tm, D_MODEL), row),
                  pl.BlockSpec((tm, D_MODEL), row),
                  resident((D_MODEL, d_ff)),
                  resident((D_MODEL, d_ff)),
                  resident((d_ff, D_MODEL)),
                  pl.BlockSpec((1, D_MODEL), const)],
        out_specs=pl.BlockSpec((tm, D_MODEL), row),
        out_shape=jax.ShapeDtypeStruct((n, D_MODEL), F32),
        scratch_shapes=[pltpu.VMEM((tm, D_MODEL), F32)],
        compiler_params=pltpu.CompilerParams(dimension_semantics=("arbitrary",),
                                             vmem_limit_bytes=VMEM_LIMIT),
        name="ffn",
    )(hf, x1, w_gate, w_up, w_down, g_final.reshape(1, -1))


def _gate_weight_pad(w_gate_up):
    pad = jnp.zeros((LANES, GLA_DK), w_gate_up.dtype)
    return pad.at[MISC_ALOW:MISC_ALOW + GLA_GATE_RANK].set(w_gate_up).astype(BF16)


def kernel(x_prompt, x_sample, cache_k, cache_v, cache_kidx, state_gla, page_table, g_mix, w_in,
           w_gate_up, b_gate, gla_norm_g, w_gla_branch, idx_k_g, idx_k_b, w_dsa_branch, w_o, g_ffn,
           w_ffn_gate, w_ffn_up, w_ffn_down, rel_bias, g_final):
    depth = w_in.shape[0]
    assert depth == 1, "the final RMSNorm is fused into the FFN kernel of the single layer"
    nb, l, _ = x_prompt.shape
    db, t_new, _ = x_sample.shape
    layer = 0
    w_cat = _pack_w_in(w_in[layer])
    wgu_pad = _gate_weight_pad(w_gate_up[layer])
    w_gla = w_gla_branch[layer].astype(BF16)
    w_dsa = w_dsa_branch[layer].astype(BF16)
    w_out = w_o[layer].astype(BF16)
    w_fg, w_fu, w_fd = (w.astype(BF16) for w in (w_ffn_gate[layer], w_ffn_up[layer], w_ffn_down[layer]))

    xp = x_prompt.reshape(nb * l, D_MODEL)
    zA, kd, vd, misc, kin = _in_proj(xp, g_mix[layer], w_cat, idx_k_g[layer], idx_k_b[layer])
    zA3 = zA.reshape(nb, l, ZA_WIDTH)
    s0 = jnp.zeros((nb, GLA_HEADS, GLA_HEAD_DK, GLA_HEAD_DV), state_gla.dtype)
    og, s_p = _gla(zA3, zA3, zA3, misc.reshape(nb, l, PROJ_TILE), (COL_QG, COL_KG, COL_VG), wgu_pad,
                   b_gate[layer], gla_norm_g[layer], s0, GLA_CHUNK, GLA_TILE, GLA_TILE, nb)
    od = _dsa_prompt(zA, kd, vd, misc, kin, rel_bias, nb, l)
    x1, hf = _post(xp, og.reshape(nb * l, GLA_DV), zA, od.reshape(nb * l, DSA_WIDTH), w_gla, w_dsa,
                   w_out, g_ffn[layer])
    y_p = _ffn(hf, x1, w_fg, w_fu, w_fd, g_final).reshape(nb, l, D_MODEL)

    xs = x_sample.reshape(db * t_new, D_MODEL)
    zA_s, kd_s, vd_s, misc_s, kin_s = _in_proj(xs, g_mix[layer], w_cat, idx_k_g[layer], idx_k_b[layer])
    pad_t = lambda a: jnp.pad(a.reshape(db, t_new, -1), ((0, 0), (0, SAMPLE_CHUNK - t_new), (0, 0)))
    og_s, s_s = _gla(pad_t(zA_s[:, 0:GLA_DK]), pad_t(zA_s[:, GLA_DK:2 * GLA_DK]),
                     pad_t(zA_s[:, 2 * GLA_DK:2 * GLA_DK + GLA_DV]), pad_t(misc_s[:, 0:LANES]), (0, 0, 0),
                     wgu_pad, b_gate[layer], gla_norm_g[layer], state_gla[layer], SAMPLE_CHUNK, SAMPLE_CHUNK, t_new, SAMPLE_GLA_SEQS)
    og_s = og_s[:, :t_new, :].reshape(db * t_new, GLA_DV)
    od_s = _dsa_sample(zA_s, kd_s, vd_s, misc_s, kin_s, rel_bias, cache_k[layer], cache_v[layer],
                       cache_kidx[layer], page_table, db, t_new)
    x1_s, hf_s = _post(xs, og_s, zA_s, od_s, w_gla, w_dsa, w_out, g_ffn[layer])
    y_s = _ffn(hf_s, x1_s, w_fg, w_fu, w_fd, g_final).reshape(db, t_new, D_MODEL)

    heads = lambda a, n, t: a.reshape(1, n, t, DSA_HEADS, DSA_HEAD_DIM)
    return (y_p, y_s,
            heads(kd, nb, l), heads(vd, nb, l), kin.reshape(1, nb, l, IDX_DIM), s_p[None],
            heads(kd_s, db, t_new), heads(vd_s, db, t_new), kin_s.reshape(1, db, t_new, IDX_DIM), s_s[None])
```

```python
import functools
import math

import numpy as np
import jax
import jax.numpy as jnp
from jax import lax
from jax.experimental import pallas as pl
from jax.experimental.pallas import tpu as pltpu

F32, BF16, I32 = jnp.float32, jnp.bfloat16, jnp.int32

D_MODEL = 1024
GLA_HEADS = 4
GLA_HEAD_DK = 128
GLA_HEAD_DV = 256
GLA_DK = GLA_HEADS * GLA_HEAD_DK
GLA_DV = GLA_HEADS * GLA_HEAD_DV
GLA_GATE_RANK = 16
GLA_GATE_TAU = 16.0
DSA_HEADS = 8
DSA_HEAD_DIM = 64
DSA_WIDTH = DSA_HEADS * DSA_HEAD_DIM
IDX_HEADS = 4
IDX_DIM = 64
TOPK_MAX = 256
QUERY_BLOCK = 128
PAGE_SIZE = 128
REL_BUCKETS = 32
REL_MAX_DIST = 128
RMS_EPS = 1e-6
LN_EPS = 1e-6
SPLIT_SIZES = (GLA_DK, GLA_DK, GLA_DV, GLA_DV, GLA_GATE_RANK, DSA_WIDTH, DSA_WIDTH, DSA_WIDTH,
               IDX_HEADS * IDX_DIM, IDX_DIM, IDX_HEADS, D_MODEL, D_MODEL)

LANES = 128
SUBLANES = 8
VMEM_LIMIT = 56 * 1024 * 1024

PROJ_TILE = 512
ZA_TILES = 11
ZA_WIDTH = ZA_TILES * PROJ_TILE
COL_QG, COL_KG = 0, 1
COL_VG, COL_RG, COL_GG, COL_GD = 1, 2, 3, 4
ZA_QD = 10 * PROJ_TILE
MISC_ALOW, MISC_WI, MISC_QI = 64, 80, 128

GLA_CHUNK = 64
GLA_TILE = 256
SAMPLE_CHUNK = 16
SAMPLE_GLA_SEQS = 4
DSA_TK = 512
BIAS_TAB_ROWS = 1408
BIAS_TAB_OFF = 896
V_ROWS = 80
NEG = -1e30
LOG2E = 1.4426950408889634
BISECT_STEPS = 15


def _sigmoid(x):
    return 1.0 / (1.0 + jnp.exp(-x))


def _rms(x, g):
    return x * lax.rsqrt(jnp.mean(x * x, axis=-1, keepdims=True) + RMS_EPS) * g


def _t5_bucket_np(n):
    n = np.maximum(np.asarray(n, np.int64), 0)
    max_exact = REL_BUCKETS // 2
    large = max_exact + (np.log(np.maximum(n, 1).astype(np.float32) / np.float32(max_exact))
                         / np.float32(math.log(REL_MAX_DIST / max_exact))
                         * np.float32(REL_BUCKETS - max_exact)).astype(np.int32)
    large = np.minimum(large, REL_BUCKETS - 1)
    return np.where(n < max_exact, n, large).astype(np.int32)


def _inproj_wide_body(x_ref, g_ref, w_ref, za_ref):
    h = _rms(x_ref[...], g_ref[...]).astype(BF16)
    for j in range(ZA_TILES):
        cols = slice(j * PROJ_TILE, (j + 1) * PROJ_TILE)
        za_ref[:, cols] = jnp.dot(h, w_ref[:, cols], preferred_element_type=F32)


def _idx_key_norm(misc, ikg_ref, ikb_ref):
    ki = misc[:, 0:IDX_DIM]
    mu = jnp.mean(ki, axis=-1, keepdims=True)
    var = jnp.mean(jnp.square(ki - mu), axis=-1, keepdims=True)
    return (ki - mu) * lax.rsqrt(var + LN_EPS) * ikg_ref[...] + ikb_ref[...]


def _inproj_tail_body(x_ref, g_ref, w_ref, ikg_ref, ikb_ref, kd_ref, vd_ref, misc_ref, kin_ref):
    h = _rms(x_ref[...], g_ref[...]).astype(BF16)
    tile = lambda j: jnp.dot(h, w_ref[:, j * PROJ_TILE:(j + 1) * PROJ_TILE], preferred_element_type=F32)
    kd_ref[...] = tile(0)
    vd_ref[...] = tile(1)
    res = tile(2)
    misc_ref[...] = res
    kin_ref[...] = _idx_key_norm(res, ikg_ref, ikb_ref)


def _inproj_wide_prompt_body(x_ref, g_ref, w_ref, wqt_ref, za_ref, qdt_ref):
    h = _rms(x_ref[...], g_ref[...]).astype(BF16)
    for j in range(ZA_TILES - 1):
        cols = slice(j * PROJ_TILE, (j + 1) * PROJ_TILE)
        za_ref[:, cols] = jnp.dot(h, w_ref[:, cols], preferred_element_type=F32)
    qdt = lax.dot_general(wqt_ref[...], h, _NT, preferred_element_type=F32)
    qdt_ref[...] = (qdt * ((DSA_HEAD_DIM ** -0.5) * LOG2E)).astype(BF16)


def _inproj_tail_prompt_body(x_ref, g_ref, w_ref, wkt_ref, wvt_ref, ikg_ref, ikb_ref,
                             kdb_ref, kt_ref, vt_ref, vslab_ref, misc_ref, kin_ref):
    h = _rms(x_ref[...], g_ref[...]).astype(BF16)
    kdb_ref[...] = jnp.dot(h, w_ref[:, 0:PROJ_TILE], preferred_element_type=F32).astype(BF16)
    kt_ref[...] = lax.dot_general(wkt_ref[...], h, _NT, preferred_element_type=F32)
    vt = lax.dot_general(wvt_ref[...], h, _NT, preferred_element_type=F32)
    vt_ref[...] = vt
    pad_rows = V_ROWS - DSA_HEAD_DIM
    ones_row = jnp.where(lax.broadcasted_iota(I32, (pad_rows, DSA_TK), 0) == 0, 1.0, 0.0).astype(BF16)
    for s in range(vslab_ref.shape[0]):
        for hd in range(DSA_HEADS):
            r0 = hd * V_ROWS
            vslab_ref[s, r0:r0 + DSA_HEAD_DIM, :] = vt[hd * DSA_HEAD_DIM:(hd + 1) * DSA_HEAD_DIM,
                                                       s * DSA_TK:(s + 1) * DSA_TK].astype(BF16)
            vslab_ref[s, r0 + DSA_HEAD_DIM:r0 + V_ROWS, :] = ones_row
    res = jnp.dot(h, w_ref[:, PROJ_TILE:2 * PROJ_TILE], preferred_element_type=F32)
    misc_ref[...] = res
    kin_ref[...] = _idx_key_norm(res, ikg_ref, ikb_ref)


def _split_w_in(w_in):
    w = w_in.astype(BF16)
    pts = np.cumsum((0,) + SPLIT_SIZES)
    seg = [w[:, int(pts[i]):int(pts[i + 1])] for i in range(len(SPLIT_SIZES))]
    q_g, k_g, v_g, r_g, a_low, q_d, k_d, v_d, q_i, k_i, w_i, gate_g, gate_d = seg
    z = lambda n: jnp.zeros((w.shape[0], n), w.dtype)
    misc = jnp.concatenate([k_i, a_low, w_i, z(LANES - MISC_WI - IDX_HEADS), q_i,
                            z(PROJ_TILE - MISC_QI - IDX_HEADS * IDX_DIM)], axis=1)
    wide = jnp.concatenate([q_g, k_g, v_g, r_g, gate_g, gate_d, q_d], axis=1)
    return wide, q_d, k_d, v_d, misc


def _in_proj_specs():
    row = lambda i: (i, 0)
    const = lambda i: (0, 0)
    resident = lambda shape: pl.BlockSpec(shape, const, pipeline_mode=pl.Buffered(1))
    params = pltpu.CompilerParams(dimension_semantics=("arbitrary",), vmem_limit_bytes=VMEM_LIMIT)
    return row, const, resident, params


def _in_proj(x2d, g_mix, w_parts, idx_k_g, idx_k_b):
    wide, q_d, k_d, v_d, misc_w = w_parts
    n = x2d.shape[0]
    row, const, resident, params = _in_proj_specs()
    g2 = g_mix.reshape(1, -1)
    tm = min(n, 512)
    zA = pl.pallas_call(
        _inproj_wide_body,
        grid=(n // tm,),
        in_specs=[pl.BlockSpec((tm, D_MODEL), row), pl.BlockSpec((1, D_MODEL), const),
                  resident((D_MODEL, ZA_WIDTH))],
        out_specs=pl.BlockSpec((tm, ZA_WIDTH), row),
        out_shape=jax.ShapeDtypeStruct((n, ZA_WIDTH), F32),
        compiler_params=params,
        name="in_proj_wide",
    )(x2d, g2, wide)
    tm = min(n, 1024)
    w_tail = jnp.concatenate([k_d, v_d, misc_w], axis=1)
    kd, vd, misc, kin = pl.pallas_call(
        _inproj_tail_body,
        grid=(n // tm,),
        in_specs=[pl.BlockSpec((tm, D_MODEL), row), pl.BlockSpec((1, D_MODEL), const),
                  resident((D_MODEL, w_tail.shape[1])),
                  pl.BlockSpec((1, IDX_DIM), const), pl.BlockSpec((1, IDX_DIM), const)],
        out_specs=[pl.BlockSpec((tm, PROJ_TILE), row)] * 3 + [pl.BlockSpec((tm, IDX_DIM), row)],
        out_shape=[jax.ShapeDtypeStruct((n, PROJ_TILE), F32)] * 3 + [jax.ShapeDtypeStruct((n, IDX_DIM), F32)],
        compiler_params=params,
        name="in_proj_tail",
    )(x2d, g2, w_tail, idx_k_g.reshape(1, -1), idx_k_b.reshape(1, -1))
    return zA, kd, vd, misc, kin


def _in_proj_prompt(x2d, g_mix, w_parts, idx_k_g, idx_k_b, nb, l):
    wide, q_d, k_d, v_d, misc_w = w_parts
    n = x2d.shape[0]
    row, const, resident, params = _in_proj_specs()
    g2 = g_mix.reshape(1, -1)
    za_w = ZA_WIDTH - PROJ_TILE
    tm = 512
    per_b = l // tm
    zA, qdt = pl.pallas_call(
        _inproj_wide_prompt_body,
        grid=(n // tm,),
        in_specs=[pl.BlockSpec((tm, D_MODEL), row), pl.BlockSpec((1, D_MODEL), const),
                  resident((D_MODEL, za_w)), resident((DSA_WIDTH, D_MODEL))],
        out_specs=[pl.BlockSpec((tm, za_w), row),
                   pl.BlockSpec((None, DSA_WIDTH, tm), lambda i: (i // per_b, 0, i % per_b))],
        out_shape=[jax.ShapeDtypeStruct((n, za_w), F32), jax.ShapeDtypeStruct((nb, DSA_WIDTH, l), BF16)],
        compiler_params=params,
        name="in_proj_wide",
    )(x2d, g2, wide[:, :za_w], q_d.T)
    tm = 1024
    per_t = l // tm
    slabs = tm // DSA_TK
    tok = lambda i: (i // per_t, 0, i % per_t)
    w_tail = jnp.concatenate([k_d, misc_w], axis=1)
    kdb, kt, vt, vslab, misc, kin = pl.pallas_call(
        _inproj_tail_prompt_body,
        grid=(n // tm,),
        in_specs=[pl.BlockSpec((tm, D_MODEL), row), pl.BlockSpec((1, D_MODEL), const),
                  resident((D_MODEL, w_tail.shape[1])), resident((DSA_WIDTH, D_MODEL)),
                  resident((DSA_WIDTH, D_MODEL)),
                  pl.BlockSpec((1, IDX_DIM), const), pl.BlockSpec((1, IDX_DIM), const)],
        out_specs=[pl.BlockSpec((tm, DSA_WIDTH), row),
                   pl.BlockSpec((None, DSA_WIDTH, tm), tok),
                   pl.BlockSpec((None, DSA_WIDTH, tm), tok),
                   pl.BlockSpec((None, slabs, DSA_HEADS * V_ROWS, DSA_TK), lambda i: (i // per_t, i % per_t, 0, 0)),
                   pl.BlockSpec((tm, PROJ_TILE), row),
                   pl.BlockSpec((tm, IDX_DIM), row)],
        out_shape=[jax.ShapeDtypeStruct((n, DSA_WIDTH), BF16),
                   jax.ShapeDtypeStruct((nb, DSA_WIDTH, l), F32),
                   jax.ShapeDtypeStruct((nb, DSA_WIDTH, l), F32),
                   jax.ShapeDtypeStruct((nb, l // DSA_TK, DSA_HEADS * V_ROWS, DSA_TK), BF16),
                   jax.ShapeDtypeStruct((n, PROJ_TILE), F32),
                   jax.ShapeDtypeStruct((n, IDX_DIM), F32)],
        compiler_params=params,
        name="in_proj_tail",
    )(x2d, g2, w_tail, k_d.T, v_d.T, idx_k_g.reshape(1, -1), idx_k_b.reshape(1, -1))
    return zA, qdt, kdb, kt, vt, vslab, misc, kin


def _gla_consts(c):
    nlev = int(math.log2(c))
    t = np.arange(c)[:, None]
    s = np.arange(c)[None, :]
    mats = [(s <= t), np.ones((c, c), bool)]
    masks = []
    for l in range(nlev):
        mid = ((t >> (l + 1)) << (l + 1)) + (1 << l) - 1
        mats.append(s <= mid)
        masks.append(((t >> (l + 1)) == (s >> (l + 1))) & (((t >> l) & 1) == 1) & (((s >> l) & 1) == 0))
    masks.append(t == s)
    return (jnp.asarray(np.concatenate(mats, 0).astype(np.float32), BF16),
            jnp.asarray(np.stack(masks).astype(np.float32), F32), nlev)


_NT = (((1,), (1,)), ((), ()))
_TN = (((0,), (0,)), ((), ()))


def _gla_body(q_ref, k_ref, v_ref, misc_ref, wgu_ref, bg_ref, gn_ref, mst_ref, lmask_ref, s0_ref,
              o_ref, sout_ref, s_scr, la_scr, *, c, nc, nlev, valid_rows, bb):
    step = pl.program_id(1)
    tile = c * nc

    @pl.when(step == 0)
    def _():
        for bi in range(bb):
            for h in range(GLA_HEADS):
                s_scr[bi, h] = s0_ref[bi, h].T

    for bi in range(bb):
        x = jnp.dot(misc_ref[bi].astype(BF16), wgu_ref[...], preferred_element_type=F32) + bg_ref[...]
        log_a = (jnp.minimum(x, 0.0) - jnp.log1p(jnp.exp(-jnp.abs(x)))) * (1.0 / GLA_GATE_TAU)
        if valid_rows < tile:
            log_a = jnp.where(lax.broadcasted_iota(I32, log_a.shape, 0) < valid_rows, log_a, 0.0)
        la_scr[bi] = log_a
    scale = GLA_HEAD_DK ** -0.5

    def chunk(ci, carry):
        r0 = pl.multiple_of(ci * c, c)
        rows = pl.ds(r0, c)
        chains = [(bi, h) for bi in range(bb) for h in range(GLA_HEADS)]
        cs = []
        for bi in range(bb):
            g_all = la_scr[bi, rows, :]
            g_hi = g_all.astype(BF16)
            g_lo = (g_all - g_hi.astype(F32)).astype(BF16)
            t = jnp.dot(mst_ref[...], jnp.concatenate([g_hi, g_lo], axis=1), preferred_element_type=F32)
            cs.append(t[:, :GLA_DK] + t[:, GLA_DK:])
        ksl = lambda h: slice(h * GLA_HEAD_DK, (h + 1) * GLA_HEAD_DK)
        vsl = lambda h: slice(h * GLA_HEAD_DV, (h + 1) * GLA_HEAD_DV)
        qs, ks, bs, els, o_inter, att = [], [], [], [], [], []
        for bi, h in chains:
            q = q_ref[bi, rows, ksl(h)] * scale
            k = k_ref[bi, rows, ksl(h)]
            b = cs[bi][0:c, ksl(h)]
            qs.append(q), ks.append(k), bs.append(b), els.append(cs[bi][c:2 * c, ksl(h)])
            o_inter.append(lax.dot_general((q * jnp.exp(b)).astype(BF16), s_scr[bi, h].astype(BF16), _NT,
                                           preferred_element_type=F32))
            a = lmask_ref[nlev] * lax.dot_general(q.astype(BF16), k.astype(BF16), _NT,
                                                  preferred_element_type=F32)
            for l in range(nlev):
                e = cs[bi][(2 + l) * c:(3 + l) * c, ksl(h)]
                ql = (q * jnp.exp(jnp.minimum(b - e, 0.0))).astype(BF16)
                kl = (k * jnp.exp(jnp.minimum(e - b, 0.0))).astype(BF16)
                a = a + lmask_ref[l] * lax.dot_general(ql, kl, _NT, preferred_element_type=F32)
            att.append(a)
        vs = [v_ref[bi, rows, vsl(h)].astype(BF16) for bi, h in chains]
        outs = [o_inter[n] + jnp.dot(att[n].astype(BF16), vs[n], preferred_element_type=F32)
                for n in range(len(chains))]
        for n, (bi, h) in enumerate(chains):
            k_st = (ks[n] * jnp.exp(els[n] - bs[n])).astype(BF16)
            s_scr[bi, h] = s_scr[bi, h] * jnp.exp(els[n][0:1, :]) + lax.dot_general(
                vs[n], k_st, _TN, preferred_element_type=F32)
            o_ref[bi, rows, vsl(h)] = _rms(outs[n], gn_ref[...])
        return carry

    lax.fori_loop(0, nc, chunk, 0)

    @pl.when(step == pl.num_programs(1) - 1)
    def _():
        for bi in range(bb):
            for h in range(GLA_HEADS):
                sout_ref[bi, h] = s_scr[bi, h].T


def _gla(q_arr, k_arr, v_arr, misc_arr, cols, wgu_pad, b_gate, gla_norm_g, s0, c, tile, valid_rows, bb):
    nb, l = q_arr.shape[0], q_arr.shape[1]
    assert nb % bb == 0 and l % tile == 0
    mst, lmask, nlev = _gla_consts(c)
    const2 = lambda b, s: (0, 0)
    state_spec = pl.BlockSpec((bb, GLA_HEADS, GLA_HEAD_DK, GLA_HEAD_DV), lambda b, s: (b, 0, 0, 0))
    body = functools.partial(_gla_body, c=c, nc=tile // c, nlev=nlev, valid_rows=valid_rows, bb=bb)
    return pl.pallas_call(
        body,
        grid=(nb // bb, l // tile),
        in_specs=[pl.BlockSpec((bb, tile, GLA_DK), lambda b, s: (b, s, cols[0])),
                  pl.BlockSpec((bb, tile, GLA_DK), lambda b, s: (b, s, cols[1])),
                  pl.BlockSpec((bb, tile, GLA_DV), lambda b, s: (b, s, cols[2])),
                  pl.BlockSpec((bb, tile, LANES), lambda b, s: (b, s, 0)),
                  pl.BlockSpec((LANES, GLA_DK), const2),
                  pl.BlockSpec((1, GLA_DK), const2),
                  pl.BlockSpec((1, GLA_HEAD_DV), const2),
                  pl.BlockSpec(mst.shape, const2),
                  pl.BlockSpec(lmask.shape, lambda b, s: (0, 0, 0)),
                  state_spec],
        out_specs=[pl.BlockSpec((bb, tile, GLA_DV), lambda b, s: (b, s, 0)), state_spec],
        out_shape=[jax.ShapeDtypeStruct((nb, l, GLA_DV), F32),
                   jax.ShapeDtypeStruct((nb, GLA_HEADS, GLA_HEAD_DK, GLA_HEAD_DV), F32)],
        scratch_shapes=[pltpu.VMEM((bb, GLA_HEADS, GLA_HEAD_DV, GLA_HEAD_DK), F32),
                        pltpu.VMEM((bb, tile, GLA_DK), F32)],
        compiler_params=pltpu.CompilerParams(dimension_semantics=("arbitrary", "arbitrary"),
                                             vmem_limit_bytes=VMEM_LIMIT),
        name="gla",
    )(q_arr, k_arr, v_arr, misc_arr, wgu_pad, b_gate.reshape(1, -1), gla_norm_g.reshape(1, -1),
      mst, lmask, s0)


def _score_stats_init(w):
    inf = jnp.full((SUBLANES, w), jnp.inf, F32)
    zero = jnp.zeros((SUBLANES, w), I32)
    return -inf, inf, zero, zero


def _score_stats_update(carry, blk):
    mx, mn, c_pos, c_nn = carry
    b3 = blk.reshape(blk.shape[0] // SUBLANES, SUBLANES, blk.shape[1])
    return (jnp.maximum(mx, jnp.max(b3, axis=0)),
            jnp.minimum(mn, jnp.min(jnp.where(b3 == -jnp.inf, jnp.inf, b3), axis=0)),
            c_pos + jnp.sum((b3 > 0.0).astype(I32), axis=0),
            c_nn + jnp.sum((b3 >= 0.0).astype(I32), axis=0))


def _select_threshold(sc_ref, tri_ref, nt, tr, topk, small, lane_ok, stats=None):
    w = sc_ref.shape[1]
    inf = jnp.float32(jnp.inf)

    def over_tiles(fn, init):
        def body(i, carry):
            r0 = pl.multiple_of(i * tr, tr)
            return fn(carry, sc_ref[pl.ds(r0, tr), :], r0)
        return lax.fori_loop(0, nt, body, init)

    fold = lambda x: x.reshape(tr // SUBLANES, SUBLANES, w)
    zeros8 = jnp.zeros((SUBLANES, w), I32)
    pinf8 = jnp.full((SUBLANES, w), inf, F32)

    def count(pred):
        acc = over_tiles(lambda a, blk, r0: a + jnp.sum(fold(pred(blk, r0).astype(I32)), axis=0), zeros8)
        return jnp.sum(acc, axis=0, keepdims=True)

    def min_where(pred):
        acc = over_tiles(lambda a, blk, r0: jnp.minimum(
            a, jnp.min(fold(jnp.where(pred(blk, r0), blk, inf)), axis=0)), pinf8)
        return jnp.min(acc, axis=0, keepdims=True)

    if stats is None:
        stats = over_tiles(lambda carry, blk, r0: _score_stats_update(carry, blk), _score_stats_init(w))
    mx8, mn8, cp8, cn8 = stats
    c_pos = jnp.sum(cp8, axis=0, keepdims=True)
    c_nn = jnp.sum(cn8, axis=0, keepdims=True)
    hi = jnp.where(c_pos >= topk, jnp.max(mx8, axis=0, keepdims=True), 0.0)
    lo = jnp.where(c_nn >= topk, 0.0, jnp.min(mn8, axis=0, keepdims=True))

    live = lane_ok & jnp.logical_not(small)

    def bisect(_, carry):
        lo, hi = carry
        mid = 0.5 * lo + 0.5 * hi
        ge = count(lambda blk, r0: blk >= mid) >= topk
        return jnp.where(ge, mid, lo), jnp.where(ge, hi, mid)

    lo, hi = lax.fori_loop(0, BISECT_STEPS, bisect, (lo, hi))
    v0 = min_where(lambda blk, r0: blk >= lo)

    def gt_next(v):
        def f(carry, blk, r0):
            cg, nx = carry
            gt = blk > v
            return (cg + jnp.sum(fold(gt.astype(I32)), axis=0),
                    jnp.minimum(nx, jnp.min(fold(jnp.where(gt, blk, inf)), axis=0)))
        cg8, nx8 = over_tiles(f, (zeros8, pinf8))
        return jnp.sum(cg8, axis=0, keepdims=True), jnp.min(nx8, axis=0, keepdims=True)

    def peel(state):
        v, _, _ = state
        cg, nx = gt_next(v)
        move = (cg >= topk) & live
        return jnp.where(move, nx, v), cg, jnp.max(move.astype(I32))

    v, cnt_gt, _ = lax.while_loop(lambda s: s[2] > 0, peel,
                                  (v0, jnp.zeros((1, w), I32), jnp.int32(1)))
    cnt_ge = count(lambda blk, r0: blk >= v)
    excess = (cnt_ge > topk) & live

    @pl.when(jnp.max(excess.astype(I32)) > 0)
    def _():
        need = jnp.where(excess, (topk - cnt_gt).astype(F32), jnp.float32(2 ** 30))

        tb = tri_ref.shape[0]

        def drop_surplus(i, seen):
            r0 = pl.multiple_of(i * tr, tr)
            blk = sc_ref[pl.ds(r0, tr), :]
            eq = blk == v
            ones = jnp.where(eq, 1.0, 0.0).astype(BF16)
            ranks = [jnp.dot(tri_ref[...], ones[s * tb:(s + 1) * tb, :], preferred_element_type=F32)
                     for s in range(tr // tb)]
            for s in range(tr // tb):
                rows = slice(s * tb, (s + 1) * tb)
                rank = ranks[s] + seen
                sc_ref[pl.ds(r0 + s * tb, tb), :] = jnp.where(eq[rows], jnp.where(rank > need, -inf, blk[rows]),
                                                              blk[rows])
                seen = rank[tb - 1:tb, :]
            return seen

        lax.fori_loop(0, nt, drop_surplus, jnp.zeros((1, w), F32))

    return jnp.where(small, -inf, v)


def _dsa_prompt_body(relb_ref, btab_ref, tri_ref, ki_ref, qi_ref, wi_ref, kd_ref, vt_ref, qd_ref, o_ref,
                     sc_scr, tbl_scr, q2_scr, acc_scr, m_scr, o_scr, lg_scr, pr_scr, *,
                     topk, far_bucket):
    b = pl.program_id(0)
    j = pl.program_id(1)
    tk = DSA_TK
    hd = DSA_HEAD_DIM

    @pl.when((b == 0) & (j == 0))
    def _():
        q2_scr[...] = jnp.zeros(q2_scr.shape, BF16)

        def build(ci, carry):
            r0 = pl.multiple_of(ci * LANES, LANES)
            bt = btab_ref[pl.ds(r0, LANES), :]
            for h in range(DSA_HEADS):
                t = jnp.zeros(bt.shape, F32)
                for bk in range(REL_BUCKETS):
                    t = jnp.where(bt == bk, (relb_ref[bk, h] - relb_ref[far_bucket, h]) * LOG2E, t)
                tbl_scr[h, pl.ds(r0, LANES), :] = t
            return carry

        lax.fori_loop(0, BIAS_TAB_ROWS // LANES, build, 0)

    nt = j // (tk // QUERY_BLOCK) + 1
    qpos = j * QUERY_BLOCK + lax.broadcasted_iota(I32, (1, LANES), 1)
    for p in range(DSA_HEADS // 2):
        q2_scr[p, 0:hd, 0:LANES] = qd_ref[2 * p * hd:(2 * p + 1) * hd, :]
        q2_scr[p, hd:2 * hd, LANES:2 * LANES] = qd_ref[(2 * p + 1) * hd:(2 * p + 2) * hd, :]

    wi = wi_ref[...]

    def score_tile(i, carry):
        r0 = pl.multiple_of(i * tk, tk)
        s4 = jnp.dot(ki_ref[pl.ds(r0, tk), :], qi_ref[...], preferred_element_type=F32)
        sc = jnp.zeros((tk, LANES), F32)
        for h in range(IDX_HEADS):
            sc = sc + jnp.maximum(s4[:, h * LANES:(h + 1) * LANES], 0.0) * wi[h:h + 1, :]
        kpos = r0 + lax.broadcasted_iota(I32, (tk, LANES), 0)
        sc = jnp.where(kpos <= qpos, sc, -jnp.inf)
        sc_scr[pl.ds(r0, tk), :] = sc
        return _score_stats_update(carry, sc)

    stats = lax.fori_loop(0, nt, score_tile, _score_stats_init(LANES))

    small = (qpos + 1) < topk
    kstar = _select_threshold(sc_scr, tri_ref, nt, tk, topk, small, jnp.full((1, LANES), True), stats)

    m_scr[...] = jnp.full(m_scr.shape, NEG, F32)
    acc_scr[...] = jnp.zeros(acc_scr.shape, F32)
    vrows = acc_scr.shape[0] // DSA_HEADS

    def logits_stage(i, buf, near):
        r0 = pl.multiple_of(i * tk, tk)
        blk = sc_scr[pl.ds(r0, tk), :]
        if near:
            kpos = r0 + lax.broadcasted_iota(I32, (tk, LANES), 0)
            addm = jnp.where(blk >= kstar, jnp.where(kpos <= qpos, 0.0, NEG), NEG)
            off = pl.multiple_of(i * tk - j * QUERY_BLOCK + BIAS_TAB_OFF, LANES)
        else:
            addm = jnp.where(blk >= kstar, 0.0, NEG)
        tile_max = []
        for p in range(DSA_HEADS // 2):
            lg2 = jnp.dot(kd_ref[pl.ds(r0, tk), p * LANES:(p + 1) * LANES], q2_scr[p],
                          preferred_element_type=F32)
            for hh in range(2):
                h = 2 * p + hh
                lg = lg2[:, hh * LANES:(hh + 1) * LANES] + addm
                if near:
                    lg = lg + tbl_scr[h, pl.ds(off, tk), :]
                lg_scr[buf, h] = lg
                tile_max.append(jnp.max(lg, axis=0, keepdims=True))
        return tuple(tile_max)

    def softmax_pv_stage(i, buf, tile_max):
        alpha = []
        for h in range(DSA_HEADS):
            m_old = m_scr[h:h + 1, :]
            m_new = jnp.maximum(m_old, tile_max[h])
            m_scr[h:h + 1, :] = m_new
            alpha.append(jnp.exp2(m_old - m_new))
            pr_scr[h] = jnp.exp2((lg_scr[buf, h] - m_new).astype(BF16))
        for h in range(DSA_HEADS):
            rows = slice(h * vrows, (h + 1) * vrows)
            acc_scr[rows, :] = alpha[h] * acc_scr[rows, :] + jnp.dot(
                vt_ref[i, rows, :], pr_scr[h], preferred_element_type=F32)

    n_far = jnp.maximum(nt - 2, 0)

    @pl.when(n_far > 0)
    def _():
        n_pairs = (n_far - 1) // 2

        def pair(it, tile_max):
            i0 = 2 * it
            max1 = logits_stage(i0 + 1, 1, False)
            softmax_pv_stage(i0, 0, tile_max)
            max2 = logits_stage(i0 + 2, 0, False)
            softmax_pv_stage(i0 + 1, 1, max1)
            return max2

        tile_max = lax.fori_loop(0, n_pairs, pair, logits_stage(0, 0, False))
        last = 2 * n_pairs

        @pl.when(n_far - last == 1)
        def _():
            softmax_pv_stage(last, 0, tile_max)

        @pl.when(n_far - last == 2)
        def _():
            max1 = logits_stage(last + 1, 1, False)
            softmax_pv_stage(last, 0, tile_max)
            softmax_pv_stage(last + 1, 1, max1)

    def near_tile(i, carry):
        softmax_pv_stage(i, 0, logits_stage(i, 0, True))
        return carry

    lax.fori_loop(n_far, nt, near_tile, 0)

    for h in range(DSA_HEADS):
        o_scr[h * hd:(h + 1) * hd, :] = (acc_scr[h * vrows:h * vrows + hd, :]
                                         / acc_scr[h * vrows + hd:h * vrows + hd + 1, :])
    o_ref[...] = o_scr[...].T.astype(BF16)


def _tri_ones(n):
    return jnp.asarray(np.tril(np.ones((n, n), np.float32)), BF16)


def _bias_bucket_table():
    u = np.arange(BIAS_TAB_ROWS)[:, None]
    r = np.arange(LANES)[None, :]
    return jnp.asarray(_t5_bucket_np(r + BIAS_TAB_OFF - u), I32)


def _dsa_prompt(qd, kd_bf, vt, misc, kin, rel_bias, nb, l):
    nq = l // QUERY_BLOCK
    topk = min(TOPK_MAX, l // 4)
    assert l % DSA_TK == 0
    far = _t5_bucket_np(np.arange(REL_MAX_DIST, max(l, REL_MAX_DIST + 1)))
    assert (far == far[0]).all()
    ki = kin.reshape(nb, l, IDX_DIM).astype(BF16)
    qi = misc[:, MISC_QI:MISC_QI + IDX_HEADS * IDX_DIM].reshape(nb, nq, QUERY_BLOCK, IDX_HEADS, IDX_DIM)
    qi = qi.transpose(0, 4, 1, 3, 2).reshape(nb, IDX_DIM, nq * IDX_HEADS * QUERY_BLOCK).astype(BF16)
    wi = misc[:, MISC_WI:MISC_WI + IDX_HEADS] * ((IDX_DIM ** -0.5) * (IDX_HEADS ** -0.5))
    wi = wi.reshape(nb, nq, QUERY_BLOCK, IDX_HEADS).transpose(0, 1, 3, 2)
    wi = jnp.pad(wi, ((0, 0), (0, 0), (0, SUBLANES - IDX_HEADS), (0, 0))).reshape(nb, nq * SUBLANES, QUERY_BLOCK)
    kd_bf = kd_bf.reshape(nb, l, DSA_WIDTH)
    body = functools.partial(_dsa_prompt_body, topk=topk, far_bucket=int(far[0]))
    whole = lambda b, j: (b, 0, 0)
    return pl.pallas_call(
        body,
        grid=(nb, nq),
        in_specs=[pl.BlockSpec(memory_space=pltpu.SMEM),
                  pl.BlockSpec((BIAS_TAB_ROWS, LANES), lambda b, j: (0, 0), pipeline_mode=pl.Buffered(1)),
                  pl.BlockSpec((LANES, LANES), lambda b, j: (0, 0), pipeline_mode=pl.Buffered(1)),
                  pl.BlockSpec((None, l, IDX_DIM), whole, pipeline_mode=pl.Buffered(1)),
                  pl.BlockSpec((None, IDX_DIM, IDX_HEADS * QUERY_BLOCK), lambda b, j: (b, 0, j)),
                  pl.BlockSpec((None, SUBLANES, QUERY_BLOCK), lambda b, j: (b, j, 0)),
                  pl.BlockSpec((None, l, DSA_WIDTH), whole, pipeline_mode=pl.Buffered(1)),
                  pl.BlockSpec((None, l // DSA_TK, DSA_HEADS * V_ROWS, DSA_TK), lambda b, j: (b, 0, 0, 0),
                               pipeline_mode=pl.Buffered(1)),
                  pl.BlockSpec((None, DSA_WIDTH, QUERY_BLOCK), lambda b, j: (b, 0, j))],
        out_specs=pl.BlockSpec((None, QUERY_BLOCK, DSA_WIDTH), lambda b, j: (b, j, 0)),
        out_shape=jax.ShapeDtypeStruct((nb, l, DSA_WIDTH), BF16),
        scratch_shapes=[pltpu.VMEM((l, LANES), F32),
                        pltpu.VMEM((DSA_HEADS, BIAS_TAB_ROWS, LANES), F32),
                        pltpu.VMEM((DSA_HEADS // 2, LANES, 2 * LANES), BF16),
                        pltpu.VMEM((DSA_HEADS * V_ROWS, LANES), F32),
                        pltpu.VMEM((DSA_HEADS, LANES), F32),
                        pltpu.VMEM((DSA_WIDTH, LANES), F32),
                        pltpu.VMEM((2, DSA_HEADS, DSA_TK, LANES), F32),
                        pltpu.VMEM((DSA_HEADS, DSA_TK, LANES), BF16)],
        compiler_params=pltpu.CompilerParams(dimension_semantics=("arbitrary", "arbitrary"),
                                             vmem_limit_bytes=VMEM_LIMIT),
        name="dsa_prompt",
    )(rel_bias, _bias_bucket_table(), _tri_ones(LANES), ki, qi, wi, kd_bf, vt, qd)


def _select_threshold_lanes(sc_ref, triu_ref, nt, topk, small, row_ok):
    r = sc_ref.shape[0]
    inf = jnp.float32(jnp.inf)
    tiles = [slice(i * LANES, (i + 1) * LANES) for i in range(nt)]
    rowsum = lambda x: jnp.sum(x, axis=1, keepdims=True)
    rowmin = lambda x: jnp.min(x, axis=1, keepdims=True)

    def count(pred):
        acc = jnp.zeros((r, LANES), I32)
        for t in tiles:
            acc = acc + pred(sc_ref[:, t]).astype(I32)
        return rowsum(acc)

    def min_where(pred):
        acc = jnp.full((r, LANES), inf, F32)
        for t in tiles:
            blk = sc_ref[:, t]
            acc = jnp.minimum(acc, jnp.where(pred(blk), blk, inf))
        return rowmin(acc)

    mx = jnp.full((r, LANES), -inf, F32)
    mn = jnp.full((r, LANES), inf, F32)
    c_pos = jnp.zeros((r, LANES), I32)
    c_nn = jnp.zeros((r, LANES), I32)
    for t in tiles:
        blk = sc_ref[:, t]
        mx = jnp.maximum(mx, blk)
        mn = jnp.minimum(mn, jnp.where(blk == -inf, inf, blk))
        c_pos = c_pos + (blk > 0.0).astype(I32)
        c_nn = c_nn + (blk >= 0.0).astype(I32)
    hi = jnp.where(rowsum(c_pos) >= topk, jnp.max(mx, axis=1, keepdims=True), 0.0)
    lo = jnp.where(rowsum(c_nn) >= topk, 0.0, rowmin(mn))
    live = row_ok & jnp.logical_not(small)

    def bisect(_, carry):
        lo, hi = carry
        mid = 0.5 * lo + 0.5 * hi
        ge = count(lambda blk: blk >= mid) >= topk
        return jnp.where(ge, mid, lo), jnp.where(ge, hi, mid)

    lo, hi = lax.fori_loop(0, BISECT_STEPS, bisect, (lo, hi))
    v0 = min_where(lambda blk: blk >= lo)

    def peel(state):
        v, _, _ = state
        cg = jnp.zeros((r, LANES), I32)
        nx = jnp.full((r, LANES), inf, F32)
        for t in tiles:
            blk = sc_ref[:, t]
            gt = blk > v
            cg = cg + gt.astype(I32)
            nx = jnp.minimum(nx, jnp.where(gt, blk, inf))
        cg = rowsum(cg)
        move = (cg >= topk) & live
        return jnp.where(move, rowmin(nx), v), cg, jnp.max(move.astype(I32))

    v, cnt_gt, _ = lax.while_loop(lambda s: s[2] > 0, peel,
                                  (v0, jnp.zeros((r, 1), I32), jnp.int32(1)))
    excess = (count(lambda blk: blk >= v) > topk) & live

    @pl.when(jnp.max(excess.astype(I32)) > 0)
    def _():
        need = (topk - cnt_gt).astype(F32)
        seen = jnp.zeros((r, 1), F32)
        for t in tiles:
            blk = sc_ref[:, t]
            eq = blk == v
            rank = seen + jnp.dot(jnp.where(eq, 1.0, 0.0).astype(BF16), triu_ref[...],
                                  preferred_element_type=F32)
            sc_ref[:, t] = jnp.where(eq & (rank > need) & excess, -inf, blk)
            seen = rank[:, LANES - 1:LANES]

    return jnp.where(small, -inf, v)


SEQ_GROUP = LANES // SUBLANES
NEW_ROWS = 16


def _dsa_sample_select_body(pt_ref, *refs, n_pages, t_new, topk):
    page_refs = refs[:n_pages]
    knew_ref, qi_ref, wi_ref, triu_ref, mask_ref, sc_scr = refs[n_pages:]
    g = pl.program_id(1)
    past = n_pages * PAGE_SIZE
    rows = pl.ds(pl.multiple_of(g * SUBLANES, SUBLANES), SUBLANES)
    for p in range(n_pages + 1):
        keys_t = (page_refs[p][...] if p < n_pages else knew_ref[...]).astype(BF16)
        s = jnp.dot(qi_ref[...], keys_t, preferred_element_type=F32)
        sc = jnp.zeros((SUBLANES, PAGE_SIZE), F32)
        for h in range(IDX_HEADS):
            hs = slice(h * SUBLANES, (h + 1) * SUBLANES)
            sc = sc + jnp.maximum(s[hs, :], 0.0) * wi_ref[hs, :]
        sc_scr[rows, p * PAGE_SIZE:(p + 1) * PAGE_SIZE] = sc

    @pl.when(g == SEQ_GROUP - 1)
    def _():
        q_of_row = lax.broadcasted_iota(I32, (LANES, 1), 0) % SUBLANES
        new = slice(past, past + PAGE_SIZE)
        cpos = lax.broadcasted_iota(I32, (LANES, PAGE_SIZE), 1)
        sc_scr[:, new] = jnp.where(cpos <= q_of_row, sc_scr[:, new], -jnp.inf)
        row_ok = q_of_row < t_new
        small = (past + q_of_row + 1) < topk
        kstar = _select_threshold_lanes(sc_scr, triu_ref, n_pages + 1, topk, small, row_ok)
        for p in range(n_pages + 1):
            t = slice(p * PAGE_SIZE, (p + 1) * PAGE_SIZE)
            mask_ref[:, t] = jnp.where(sc_scr[:, t] >= kstar, 1.0, 0.0)


def _dsa_sample_select(page_table, kidx_t, kin_new_t, qi_rows, wi_rows, t_new, topk):
    db, n_pages = page_table.shape
    ng = db // SEQ_GROUP
    width = (n_pages + 1) * PAGE_SIZE
    seq = lambda gi, g, pt: (gi * SEQ_GROUP + g, 0, 0)
    page_specs = [pl.BlockSpec((None, IDX_DIM, PAGE_SIZE),
                               functools.partial(lambda gi, g, pt, p: (pt[gi * SEQ_GROUP + g, p], 0, 0), p=p))
                  for p in range(n_pages)]
    body = functools.partial(_dsa_sample_select_body, n_pages=n_pages, t_new=t_new, topk=topk)
    grid_spec = pltpu.PrefetchScalarGridSpec(
        num_scalar_prefetch=1,
        grid=(ng, SEQ_GROUP),
        in_specs=page_specs + [pl.BlockSpec((None, IDX_DIM, PAGE_SIZE), seq),
                               pl.BlockSpec((None, IDX_HEADS * SUBLANES, IDX_DIM), seq),
                               pl.BlockSpec((None, IDX_HEADS * SUBLANES, PAGE_SIZE), seq),
                               pl.BlockSpec((LANES, LANES), lambda gi, g, pt: (0, 0))],
        out_specs=pl.BlockSpec((None, LANES, width), lambda gi, g, pt: (gi, 0, 0)),
        scratch_shapes=[pltpu.VMEM((LANES, width), F32)],
    )
    triu = jnp.asarray(np.triu(np.ones((LANES, LANES), np.float32)), BF16)
    return pl.pallas_call(
        body, grid_spec=grid_spec,
        out_shape=jax.ShapeDtypeStruct((ng, LANES, width), F32),
        compiler_params=pltpu.CompilerParams(dimension_semantics=("arbitrary", "arbitrary"),
                                             vmem_limit_bytes=VMEM_LIMIT),
        name="dsa_sample_select",
    )(page_table, *([kidx_t] * n_pages), kin_new_t, qi_rows, wi_rows, triu)


def _dsa_sample_attend_body(pt_ref, *refs, n_pages):
    k_refs = refs[:n_pages]
    v_refs = refs[n_pages:2 * n_pages]
    (knew_ref, vnew_ref, mask_ref, qbd_ref, btab_ref, rb_ref, o_ref, lg_scr, tbl_scr) = refs[2 * n_pages:]
    b = pl.program_id(0)
    hq = DSA_HEADS * SUBLANES

    @pl.when(b == 0)
    def _():
        for half in range(2):
            cols = slice(half * PAGE_SIZE, (half + 1) * PAGE_SIZE)
            bt = btab_ref[:, cols]
            t = jnp.zeros(bt.shape, F32)
            for bk in range(REL_BUCKETS):
                t = jnp.where(bt == bk, rb_ref[bk], t)
            tbl_scr[:, cols] = t

    far_bias = rb_ref[REL_BUCKETS - 1]
    page = lambda ref: ref[...].reshape(DSA_WIDTH, PAGE_SIZE).astype(BF16)

    def masked(lg, sel):
        return lg + jnp.where(jnp.concatenate([sel] * DSA_HEADS, axis=0) > 0.5, 0.0, NEG)

    m = jnp.full((hq, LANES), NEG, F32)
    for p in range(n_pages):
        cols = slice(p * PAGE_SIZE, (p + 1) * PAGE_SIZE)
        lg = jnp.dot(qbd_ref[...], page(k_refs[p]), preferred_element_type=F32)
        lg = lg + (tbl_scr[:, 0:PAGE_SIZE] if p == n_pages - 1 else far_bias)
        lg = masked(lg, mask_ref[:, cols])
        lg_scr[:, cols] = lg
        m = jnp.maximum(m, lg)
    past = n_pages * PAGE_SIZE
    lg_new = lax.dot_general(qbd_ref[...], knew_ref[...].astype(BF16), _NT, preferred_element_type=F32)
    lg_new = masked(lg_new + tbl_scr[:, PAGE_SIZE:PAGE_SIZE + NEW_ROWS], mask_ref[:, past:past + NEW_ROWS])
    m = jnp.maximum(jnp.max(m, axis=1, keepdims=True), jnp.max(lg_new, axis=1, keepdims=True))

    pr_new = jnp.exp(lg_new - m)
    acc = jnp.dot(pr_new.astype(BF16), vnew_ref[...].astype(BF16), preferred_element_type=F32)
    lsum = jnp.zeros((hq, LANES), F32)
    for p in range(n_pages):
        cols = slice(p * PAGE_SIZE, (p + 1) * PAGE_SIZE)
        pr = jnp.exp(lg_scr[:, cols] - m)
        lsum = lsum + pr
        acc = acc + lax.dot_general(pr.astype(BF16), page(v_refs[p]), _NT, preferred_element_type=F32)
    lsum = jnp.sum(lsum, axis=1, keepdims=True) + jnp.sum(pr_new, axis=1, keepdims=True)
    acc = acc / lsum
    head_of_lane = lax.broadcasted_iota(I32, (SUBLANES, DSA_WIDTH), 1) // DSA_HEAD_DIM
    out = jnp.zeros((SUBLANES, DSA_WIDTH), F32)
    for h in range(DSA_HEADS):
        out = jnp.where(head_of_lane == h, acc[h * SUBLANES:(h + 1) * SUBLANES, :], out)
    o_ref[...] = out


def _dsa_sample_attend(page_table, k_t, v_t, k_new, v_new, mask, qbd, rel_bias):
    db, n_pages = page_table.shape
    hq = DSA_HEADS * SUBLANES
    width = (n_pages + 1) * PAGE_SIZE
    seq = lambda b, pt: (b, 0, 0)
    page = lambda p: functools.partial(lambda b, pt, p: (pt[b, p], 0, 0, 0), p=p)
    kv_spec = lambda p: pl.BlockSpec((None, DSA_HEADS, DSA_HEAD_DIM, PAGE_SIZE), page(p))
    u = np.arange(2 * PAGE_SIZE)[None, :]
    q = (np.arange(hq) % SUBLANES)[:, None]
    btab = jnp.asarray(_t5_bucket_np(PAGE_SIZE + q - u), I32)
    rb = jnp.broadcast_to(jnp.repeat(rel_bias, SUBLANES, axis=1)[:, :, None], (REL_BUCKETS, hq, PAGE_SIZE))
    grid_spec = pltpu.PrefetchScalarGridSpec(
        num_scalar_prefetch=1,
        grid=(db,),
        in_specs=[kv_spec(p) for p in range(n_pages)] + [kv_spec(p) for p in range(n_pages)] + [
            pl.BlockSpec((None, NEW_ROWS, DSA_WIDTH), seq),
            pl.BlockSpec((None, NEW_ROWS, DSA_WIDTH), seq),
            pl.BlockSpec((None, SUBLANES, width), lambda b, pt: (b // SEQ_GROUP, b % SEQ_GROUP, 0)),
            pl.BlockSpec((None, hq, DSA_WIDTH), seq),
            pl.BlockSpec(btab.shape, lambda b, pt: (0, 0)),
            pl.BlockSpec(rb.shape, lambda b, pt: (0, 0, 0))],
        out_specs=pl.BlockSpec((None, SUBLANES, DSA_WIDTH), seq),
        scratch_shapes=[pltpu.VMEM((hq, width - PAGE_SIZE), F32), pltpu.VMEM((hq, 2 * PAGE_SIZE), F32)],
    )
    return pl.pallas_call(
        functools.partial(_dsa_sample_attend_body, n_pages=n_pages), grid_spec=grid_spec,
        out_shape=jax.ShapeDtypeStruct((db, SUBLANES, DSA_WIDTH), F32),
        compiler_params=pltpu.CompilerParams(dimension_semantics=("arbitrary",),
                                             vmem_limit_bytes=VMEM_LIMIT),
        name="dsa_sample_attend",
    )(page_table, *([k_t] * n_pages), *([v_t] * n_pages), k_new, v_new, mask, qbd, btab, rb)


def _dsa_sample(zA, kd, vd, misc, kin, rel_bias, cache_k, cache_v, cache_kidx, page_table, db, t_new):
    n_pages = page_table.shape[1]
    past = n_pages * PAGE_SIZE
    topk = min(TOPK_MAX, (past + t_new) // 4)
    assert db % SEQ_GROUP == 0 and t_new <= SUBLANES
    new_t = lambda a, w: jnp.pad(a.reshape(db, t_new, w).transpose(0, 2, 1),
                                 ((0, 0), (0, 0), (0, PAGE_SIZE - t_new)))
    pad_q = lambda a: jnp.pad(a, ((0, 0), (0, 0), (0, SUBLANES - t_new), (0, 0)))
    new_rows = lambda a: jnp.pad(a.reshape(db, t_new, DSA_WIDTH), ((0, 0), (0, NEW_ROWS - t_new), (0, 0)))
    qi = misc[:, MISC_QI:MISC_QI + IDX_HEADS * IDX_DIM].reshape(db, t_new, IDX_HEADS, IDX_DIM)
    qi_rows = pad_q(qi.transpose(0, 2, 1, 3)).reshape(db, IDX_HEADS * SUBLANES, IDX_DIM).astype(BF16)
    wi = misc[:, MISC_WI:MISC_WI + IDX_HEADS] * ((IDX_DIM ** -0.5) * (IDX_HEADS ** -0.5))
    wi = pad_q(wi.reshape(db, t_new, IDX_HEADS, 1).transpose(0, 2, 1, 3)).reshape(db, IDX_HEADS * SUBLANES, 1)
    wi_rows = jnp.broadcast_to(wi, (db, IDX_HEADS * SUBLANES, PAGE_SIZE))
    mask = _dsa_sample_select(page_table, cache_kidx.transpose(0, 2, 1), new_t(kin, IDX_DIM),
                              qi_rows, wi_rows, t_new, topk)
    qd = (zA[:, ZA_QD:ZA_QD + DSA_WIDTH] * (DSA_HEAD_DIM ** -0.5)).reshape(db, t_new, DSA_HEADS, DSA_HEAD_DIM)
    qd = pad_q(qd.transpose(0, 2, 1, 3))
    eye = jnp.eye(DSA_HEADS, dtype=qd.dtype)
    qbd = (qd[:, :, :, None, :] * eye[None, :, None, :, None]).reshape(db, DSA_HEADS * SUBLANES, DSA_WIDTH)
    o = _dsa_sample_attend(page_table, cache_k.transpose(0, 2, 3, 1), cache_v.transpose(0, 2, 3, 1),
                           new_rows(kd), new_rows(vd), mask, qbd.astype(BF16), rel_bias)
    return o[:, :t_new, :].reshape(db * t_new, DSA_WIDTH).astype(BF16)


def _post_body(x_ref, og_ref, rg_ref, gg_ref, gd_ref, od_ref, wg_ref, wd_ref, wo_ref, gf_ref,
               x1_ref, hf_ref):
    half = x_ref.shape[0] // 2
    parts = [slice(0, half), slice(half, 2 * half)]
    branch = []
    for r in parts:
        rg = rg_ref[r, :]
        a = (og_ref[r, :] * (rg * _sigmoid(rg))).astype(BF16)
        branch.append((jnp.dot(a, wg_ref[...], preferred_element_type=F32),
                       jnp.dot(od_ref[r, :], wd_ref[...], preferred_element_type=F32)))
    for r, (y_g, y_d) in zip(parts, branch):
        mix = (_sigmoid(gg_ref[r, :]) * y_g + _sigmoid(gd_ref[r, :]) * y_d).astype(BF16)
        x1 = x_ref[r, :] + jnp.dot(mix, wo_ref[...], preferred_element_type=F32)
        x1_ref[r, :] = x1
        hf_ref[r, :] = _rms(x1, gf_ref[...]).astype(BF16)


def _post(x2d, og, zA, od, w_gla, w_dsa, w_o, g_ffn):
    n = x2d.shape[0]
    tm = min(n, 512)
    row = lambda i: (i, 0)
    const = lambda i: (0, 0)
    return pl.pallas_call(
        _post_body,
        grid=(n // tm,),
        in_specs=[pl.BlockSpec((tm, D_MODEL), row),
                  pl.BlockSpec((tm, GLA_DV), row),
                  pl.BlockSpec((tm, GLA_DV), lambda i: (i, COL_RG)),
                  pl.BlockSpec((tm, D_MODEL), lambda i: (i, COL_GG)),
                  pl.BlockSpec((tm, D_MODEL), lambda i: (i, COL_GD)),
                  pl.BlockSpec((tm, DSA_WIDTH), row),
                  pl.BlockSpec((GLA_DV, D_MODEL), const),
                  pl.BlockSpec((DSA_WIDTH, D_MODEL), const),
                  pl.BlockSpec((D_MODEL, D_MODEL), const),
                  pl.BlockSpec((1, D_MODEL), const)],
        out_specs=[pl.BlockSpec((tm, D_MODEL), row), pl.BlockSpec((tm, D_MODEL), row)],
        out_shape=[jax.ShapeDtypeStruct((n, D_MODEL), F32), jax.ShapeDtypeStruct((n, D_MODEL), BF16)],
        compiler_params=pltpu.CompilerParams(dimension_semantics=("arbitrary",),
                                             vmem_limit_bytes=VMEM_LIMIT),
        name="post_mix",
    )(x2d, og, zA, zA, zA, od, w_gla, w_dsa, w_o, g_ffn.reshape(1, -1))


FFN_TILE = 256
FFN_ROWS = 512


def _ffn_body(hf_ref, x1_ref, wg_ref, wu_ref, wd_ref, gfin_ref, y_ref, acc_scr):
    hf = hf_ref[...]
    n_chunks = wg_ref.shape[1] // FFN_TILE
    cols = lambda k: slice(k * FFN_TILE, (k + 1) * FFN_TILE)

    def gate_up(k):
        return (jnp.dot(hf, wg_ref[:, cols(k)], preferred_element_type=F32),
                jnp.dot(hf, wu_ref[:, cols(k)], preferred_element_type=F32))

    acc_scr[...] = x1_ref[...]
    nxt = gate_up(0)
    for k in range(n_chunks):
        gate, up = nxt
        if k + 1 < n_chunks:
            nxt = gate_up(k + 1)
        act = (gate * _sigmoid(gate) * up).astype(BF16)
        acc_scr[...] = acc_scr[...] + jnp.dot(act, wd_ref[cols(k), :], preferred_element_type=F32)
    y_ref[...] = _rms(acc_scr[...], gfin_ref[...])


def _ffn(hf, x1, w_gate, w_up, w_down, g_final):
    n = hf.shape[0]
    d_ff = w_gate.shape[1]
    assert d_ff % FFN_TILE == 0
    tm = min(n, FFN_ROWS)
    row = lambda i: (i, 0)
    const = lambda i: (0, 0)
    resident = lambda shape: pl.BlockSpec(shape, const, pipeline_mode=pl.Buffered(1))
    return pl.pallas_call(
        _ffn_body,
        grid=(n // tm,),
        in_specs=[pl.BlockSpec((tm, D_MODEL), row),
                  pl.BlockSpec((tm, D_MODEL), row),
                  resident((D_MODEL, d_ff)),
                  resident((D_MODEL, d_ff)),
                  resident((d_ff, D_MODEL)),
                  pl.BlockSpec((1, D_MODEL), const)],
        out_specs=pl.BlockSpec((tm, D_MODEL), row),
        out_shape=jax.ShapeDtypeStruct((n, D_MODEL), F32),
        scratch_shapes=[pltpu.VMEM((tm, D_MODEL), F32)],
        compiler_params=pltpu.CompilerParams(dimension_semantics=("arbitrary",),
                                             vmem_limit_bytes=VMEM_LIMIT),
        name="ffn",
    )(hf, x1, w_gate, w_up, w_down, g_final.reshape(1, -1))


def _gate_weight_pad(w_gate_up):
    pad = jnp.zeros((LANES, GLA_DK), w_gate_up.dtype)
    return pad.at[MISC_ALOW:MISC_ALOW + GLA_GATE_RANK].set(w_gate_up).astype(BF16)


def kernel(x_prompt, x_sample, cache_k, cache_v, cache_kidx, state_gla, page_table, g_mix, w_in,
           w_gate_up, b_gate, gla_norm_g, w_gla_branch, idx_k_g, idx_k_b, w_dsa_branch, w_o, g_ffn,
           w_ffn_gate, w_ffn_up, w_ffn_down, rel_bias, g_final):
    depth = w_in.shape[0]
    assert depth == 1, "the final RMSNorm is fused into the FFN kernel of the single layer"
    nb, l, _ = x_prompt.shape
    db, t_new, _ = x_sample.shape
    layer = 0
    w_parts = _split_w_in(w_in[layer])
    wgu_pad = _gate_weight_pad(w_gate_up[layer])
    w_gla = w_gla_branch[layer].astype(BF16)
    w_dsa = w_dsa_branch[layer].astype(BF16)
    w_out = w_o[layer].astype(BF16)
    w_fg, w_fu, w_fd = (w.astype(BF16) for w in (w_ffn_gate[layer], w_ffn_up[layer], w_ffn_down[layer]))

    xp = x_prompt.reshape(nb * l, D_MODEL)
    zA, qdt, kdb, kt, vt, vslab, misc, kin = _in_proj_prompt(xp, g_mix[layer], w_parts, idx_k_g[layer],
                                                             idx_k_b[layer], nb, l)
    zA3 = zA.reshape(nb, l, zA.shape[1])
    s0 = jnp.zeros((nb, GLA_HEADS, GLA_HEAD_DK, GLA_HEAD_DV), state_gla.dtype)
    og, s_p = _gla(zA3, zA3, zA3, misc.reshape(nb, l, PROJ_TILE), (COL_QG, COL_KG, COL_VG), wgu_pad,
                   b_gate[layer], gla_norm_g[layer], s0, GLA_CHUNK, GLA_TILE, GLA_TILE, nb)
    od = _dsa_prompt(qdt, kdb, vslab, misc, kin, rel_bias, nb, l)
    x1, hf = _post(xp, og.reshape(nb * l, GLA_DV), zA, od.reshape(nb * l, DSA_WIDTH), w_gla, w_dsa,
                   w_out, g_ffn[layer])
    y_p = _ffn(hf, x1, w_fg, w_fu, w_fd, g_final).reshape(nb, l, D_MODEL)

    xs = x_sample.reshape(db * t_new, D_MODEL)
    zA_s, kd_s, vd_s, misc_s, kin_s = _in_proj(xs, g_mix[layer], w_parts, idx_k_g[layer], idx_k_b[layer])
    pad_t = lambda a: jnp.pad(a.reshape(db, t_new, -1), ((0, 0), (0, SAMPLE_CHUNK - t_new), (0, 0)))
    og_s, s_s = _gla(pad_t(zA_s[:, 0:GLA_DK]), pad_t(zA_s[:, GLA_DK:2 * GLA_DK]),
                     pad_t(zA_s[:, 2 * GLA_DK:2 * GLA_DK + GLA_DV]), pad_t(misc_s[:, 0:LANES]), (0, 0, 0),
                     wgu_pad, b_gate[layer], gla_norm_g[layer], state_gla[layer], SAMPLE_CHUNK, SAMPLE_CHUNK, t_new, SAMPLE_GLA_SEQS)
    og_s = og_s[:, :t_new, :].reshape(db * t_new, GLA_DV)
    od_s = _dsa_sample(zA_s, kd_s, vd_s, misc_s, kin_s, rel_bias, cache_k[layer], cache_v[layer],
                       cache_kidx[layer], page_table, db, t_new)
    x1_s, hf_s = _post(xs, og_s, zA_s, od_s, w_gla, w_dsa, w_out, g_ffn[layer])
    y_s = _ffn(hf_s, x1_s, w_fg, w_fu, w_fd, g_final).reshape(db, t_new, D_MODEL)

    heads = lambda a, n, t: a.reshape(1, n, t, DSA_HEADS, DSA_HEAD_DIM)
    heads_t = lambda a: a.reshape(nb, DSA_HEADS, DSA_HEAD_DIM, l).transpose(0, 3, 1, 2)[None]
    return (y_p, y_s,
            heads_t(kt), heads_t(vt), kin.reshape(1, nb, l, IDX_DIM), s_p[None],
            heads(kd_s, db, t_new), heads(vd_s, db, t_new), kin_s.reshape(1, db, t_new, IDX_DIM), s_s[None])
```

```python
import functools
import math

import numpy as np
import jax
import jax.numpy as jnp
from jax import lax
from jax.experimental import pallas as pl
from jax.experimental.pallas import tpu as pltpu

F32, BF16, I32 = jnp.float32, jnp.bfloat16, jnp.int32

D_MODEL = 1024
GLA_HEADS = 4
GLA_HEAD_DK = 128
GLA_HEAD_DV = 256
GLA_DK = GLA_HEADS * GLA_HEAD_DK
GLA_DV = GLA_HEADS * GLA_HEAD_DV
GLA_GATE_RANK = 16
GLA_GATE_TAU = 16.0
DSA_HEADS = 8
DSA_HEAD_DIM = 64
DSA_WIDTH = DSA_HEADS * DSA_HEAD_DIM
IDX_HEADS = 4
IDX_DIM = 64
TOPK_MAX = 256
QUERY_BLOCK = 128
PAGE_SIZE = 128
REL_BUCKETS = 32
REL_MAX_DIST = 128
RMS_EPS = 1e-6
LN_EPS = 1e-6
SPLIT_SIZES = (GLA_DK, GLA_DK, GLA_DV, GLA_DV, GLA_GATE_RANK, DSA_WIDTH, DSA_WIDTH, DSA_WIDTH,
               IDX_HEADS * IDX_DIM, IDX_DIM, IDX_HEADS, D_MODEL, D_MODEL)

LANES = 128
SUBLANES = 8
VMEM_LIMIT = 56 * 1024 * 1024

PROJ_TILE = 512
ZA_TILES = 11
ZA_WIDTH = ZA_TILES * PROJ_TILE
COL_QG, COL_KG = 0, 1
COL_VG, COL_RG, COL_GG, COL_GD = 1, 2, 3, 4
ZA_QD = 10 * PROJ_TILE
MISC_ALOW, MISC_WI, MISC_QI = 64, 80, 128

GLA_CHUNK = 64
GLA_TILE = 256
SAMPLE_CHUNK = 16
SAMPLE_GLA_SEQS = 4
DSA_TK = 512
BIAS_TAB_ROWS = 1408
BIAS_TAB_OFF = 896
V_ROWS = 80
NEG = -1e30
LOG2E = 1.4426950408889634
BISECT_STEPS = 20


def _sigmoid(x):
    return 1.0 / (1.0 + jnp.exp(-x))


def _rms(x, g):
    return x * lax.rsqrt(jnp.mean(x * x, axis=-1, keepdims=True) + RMS_EPS) * g


def _t5_bucket_np(n):
    n = np.maximum(np.asarray(n, np.int64), 0)
    max_exact = REL_BUCKETS // 2
    large = max_exact + (np.log(np.maximum(n, 1).astype(np.float32) / np.float32(max_exact))
                         / np.float32(math.log(REL_MAX_DIST / max_exact))
                         * np.float32(REL_BUCKETS - max_exact)).astype(np.int32)
    large = np.minimum(large, REL_BUCKETS - 1)
    return np.where(n < max_exact, n, large).astype(np.int32)


def _inproj_wide_body(x_ref, g_ref, w_ref, za_ref):
    h = _rms(x_ref[...], g_ref[...]).astype(BF16)
    for j in range(ZA_TILES):
        cols = slice(j * PROJ_TILE, (j + 1) * PROJ_TILE)
        za_ref[:, cols] = jnp.dot(h, w_ref[:, cols], preferred_element_type=F32)


def _idx_key_norm(misc, ikg_ref, ikb_ref):
    ki = misc[:, 0:IDX_DIM]
    mu = jnp.mean(ki, axis=-1, keepdims=True)
    var = jnp.mean(jnp.square(ki - mu), axis=-1, keepdims=True)
    return (ki - mu) * lax.rsqrt(var + LN_EPS) * ikg_ref[...] + ikb_ref[...]


def _inproj_tail_body(x_ref, g_ref, w_ref, ikg_ref, ikb_ref, kd_ref, vd_ref, misc_ref, kin_ref):
    h = _rms(x_ref[...], g_ref[...]).astype(BF16)
    tile = lambda j: jnp.dot(h, w_ref[:, j * PROJ_TILE:(j + 1) * PROJ_TILE], preferred_element_type=F32)
    kd_ref[...] = tile(0)
    vd_ref[...] = tile(1)
    res = tile(2)
    misc_ref[...] = res
    kin_ref[...] = _idx_key_norm(res, ikg_ref, ikb_ref)


def _inproj_wide_prompt_body(x_ref, g_ref, w_ref, wqt_ref, za_ref, qdt_ref):
    h = _rms(x_ref[...], g_ref[...]).astype(BF16)
    for j in range(ZA_TILES - 1):
        cols = slice(j * PROJ_TILE, (j + 1) * PROJ_TILE)
        za_ref[:, cols] = jnp.dot(h, w_ref[:, cols], preferred_element_type=F32)
    qdt = lax.dot_general(wqt_ref[...], h, _NT, preferred_element_type=F32)
    qdt_ref[...] = (qdt * ((DSA_HEAD_DIM ** -0.5) * LOG2E)).astype(BF16)


def _inproj_tail_prompt_body(x_ref, g_ref, w_ref, wkt_ref, wvt_ref, ikg_ref, ikb_ref,
                             kdb_ref, kt_ref, vt_ref, vslab_ref, misc_ref, kin_ref):
    h = _rms(x_ref[...], g_ref[...]).astype(BF16)
    kdb_ref[...] = jnp.dot(h, w_ref[:, 0:PROJ_TILE], preferred_element_type=F32).astype(BF16)
    kt_ref[...] = lax.dot_general(wkt_ref[...], h, _NT, preferred_element_type=F32)
    vt = lax.dot_general(wvt_ref[...], h, _NT, preferred_element_type=F32)
    vt_ref[...] = vt
    pad_rows = V_ROWS - DSA_HEAD_DIM
    ones_row = jnp.where(lax.broadcasted_iota(I32, (pad_rows, DSA_TK), 0) == 0, 1.0, 0.0).astype(BF16)
    for s in range(vslab_ref.shape[0]):
        for hd in range(DSA_HEADS):
            r0 = hd * V_ROWS
            vslab_ref[s, r0:r0 + DSA_HEAD_DIM, :] = vt[hd * DSA_HEAD_DIM:(hd + 1) * DSA_HEAD_DIM,
                                                       s * DSA_TK:(s + 1) * DSA_TK].astype(BF16)
            vslab_ref[s, r0 + DSA_HEAD_DIM:r0 + V_ROWS, :] = ones_row
    res = jnp.dot(h, w_ref[:, PROJ_TILE:2 * PROJ_TILE], preferred_element_type=F32)
    misc_ref[...] = res
    kin_ref[...] = _idx_key_norm(res, ikg_ref, ikb_ref)


def _split_w_in(w_in):
    w = w_in.astype(BF16)
    pts = np.cumsum((0,) + SPLIT_SIZES)
    seg = [w[:, int(pts[i]):int(pts[i + 1])] for i in range(len(SPLIT_SIZES))]
    q_g, k_g, v_g, r_g, a_low, q_d, k_d, v_d, q_i, k_i, w_i, gate_g, gate_d = seg
    z = lambda n: jnp.zeros((w.shape[0], n), w.dtype)
    misc = jnp.concatenate([k_i, a_low, w_i, z(LANES - MISC_WI - IDX_HEADS), q_i,
                            z(PROJ_TILE - MISC_QI - IDX_HEADS * IDX_DIM)], axis=1)
    wide = jnp.concatenate([q_g, k_g, v_g, r_g, gate_g, gate_d, q_d], axis=1)
    return wide, q_d, k_d, v_d, misc


def _in_proj_specs():
    row = lambda i: (i, 0)
    const = lambda i: (0, 0)
    resident = lambda shape: pl.BlockSpec(shape, const, pipeline_mode=pl.Buffered(1))
    params = pltpu.CompilerParams(dimension_semantics=("arbitrary",), vmem_limit_bytes=VMEM_LIMIT)
    return row, const, resident, params


def _in_proj(x2d, g_mix, w_parts, idx_k_g, idx_k_b):
    wide, q_d, k_d, v_d, misc_w = w_parts
    n = x2d.shape[0]
    row, const, resident, params = _in_proj_specs()
    g2 = g_mix.reshape(1, -1)
    tm = min(n, 512)
    zA = pl.pallas_call(
        _inproj_wide_body,
        grid=(n // tm,),
        in_specs=[pl.BlockSpec((tm, D_MODEL), row), pl.BlockSpec((1, D_MODEL), const),
                  resident((D_MODEL, ZA_WIDTH))],
        out_specs=pl.BlockSpec((tm, ZA_WIDTH), row),
        out_shape=jax.ShapeDtypeStruct((n, ZA_WIDTH), F32),
        compiler_params=params,
        name="in_proj_wide",
    )(x2d, g2, wide)
    tm = min(n, 1024)
    w_tail = jnp.concatenate([k_d, v_d, misc_w], axis=1)
    kd, vd, misc, kin = pl.pallas_call(
        _inproj_tail_body,
        grid=(n // tm,),
        in_specs=[pl.BlockSpec((tm, D_MODEL), row), pl.BlockSpec((1, D_MODEL), const),
                  resident((D_MODEL, w_tail.shape[1])),
                  pl.BlockSpec((1, IDX_DIM), const), pl.BlockSpec((1, IDX_DIM), const)],
        out_specs=[pl.BlockSpec((tm, PROJ_TILE), row)] * 3 + [pl.BlockSpec((tm, IDX_DIM), row)],
        out_shape=[jax.ShapeDtypeStruct((n, PROJ_TILE), F32)] * 3 + [jax.ShapeDtypeStruct((n, IDX_DIM), F32)],
        compiler_params=params,
        name="in_proj_tail",
    )(x2d, g2, w_tail, idx_k_g.reshape(1, -1), idx_k_b.reshape(1, -1))
    return zA, kd, vd, misc, kin


def _in_proj_prompt(x2d, g_mix, w_parts, idx_k_g, idx_k_b, nb, l):
    wide, q_d, k_d, v_d, misc_w = w_parts
    n = x2d.shape[0]
    row, const, resident, params = _in_proj_specs()
    g2 = g_mix.reshape(1, -1)
    za_w = ZA_WIDTH - PROJ_TILE
    tm = 512
    per_b = l // tm
    zA, qdt = pl.pallas_call(
        _inproj_wide_prompt_body,
        grid=(n // tm,),
        in_specs=[pl.BlockSpec((tm, D_MODEL), row), pl.BlockSpec((1, D_MODEL), const),
                  resident((D_MODEL, za_w)), resident((DSA_WIDTH, D_MODEL))],
        out_specs=[pl.BlockSpec((tm, za_w), row),
                   pl.BlockSpec((None, DSA_WIDTH, tm), lambda i: (i // per_b, 0, i % per_b))],
        out_shape=[jax.ShapeDtypeStruct((n, za_w), F32), jax.ShapeDtypeStruct((nb, DSA_WIDTH, l), BF16)],
        compiler_params=params,
        name="in_proj_wide",
    )(x2d, g2, wide[:, :za_w], q_d.T)
    tm = 1024
    per_t = l // tm
    slabs = tm // DSA_TK
    tok = lambda i: (i // per_t, 0, i % per_t)
    w_tail = jnp.concatenate([k_d, misc_w], axis=1)
    kdb, kt, vt, vslab, misc, kin = pl.pallas_call(
        _inproj_tail_prompt_body,
        grid=(n // tm,),
        in_specs=[pl.BlockSpec((tm, D_MODEL), row), pl.BlockSpec((1, D_MODEL), const),
                  resident((D_MODEL, w_tail.shape[1])), resident((DSA_WIDTH, D_MODEL)),
                  resident((DSA_WIDTH, D_MODEL)),
                  pl.BlockSpec((1, IDX_DIM), const), pl.BlockSpec((1, IDX_DIM), const)],
        out_specs=[pl.BlockSpec((tm, DSA_WIDTH), row),
                   pl.BlockSpec((None, DSA_WIDTH, tm), tok),
                   pl.BlockSpec((None, DSA_WIDTH, tm), tok),
                   pl.BlockSpec((None, slabs, DSA_HEADS * V_ROWS, DSA_TK), lambda i: (i // per_t, i % per_t, 0, 0)),
                   pl.BlockSpec((tm, PROJ_TILE), row),
                   pl.BlockSpec((tm, IDX_DIM), row)],
        out_shape=[jax.ShapeDtypeStruct((n, DSA_WIDTH), BF16),
                   jax.ShapeDtypeStruct((nb, DSA_WIDTH, l), F32),
                   jax.ShapeDtypeStruct((nb, DSA_WIDTH, l), F32),
                   jax.ShapeDtypeStruct((nb, l // DSA_TK, DSA_HEADS * V_ROWS, DSA_TK), BF16),
                   jax.ShapeDtypeStruct((n, PROJ_TILE), F32),
                   jax.ShapeDtypeStruct((n, IDX_DIM), F32)],
        compiler_params=params,
        name="in_proj_tail",
    )(x2d, g2, w_tail, k_d.T, v_d.T, idx_k_g.reshape(1, -1), idx_k_b.reshape(1, -1))
    return zA, qdt, kdb, kt, vt, vslab, misc, kin


def _gla_consts(c):
    nlev = int(math.log2(c))
    t = np.arange(c)[:, None]
    s = np.arange(c)[None, :]
    mats = [(s <= t), np.ones((c, c), bool)]
    masks = []
    for l in range(nlev):
        mid = ((t >> (l + 1)) << (l + 1)) + (1 << l) - 1
        mats.append(s <= mid)
        masks.append(((t >> (l + 1)) == (s >> (l + 1))) & (((t >> l) & 1) == 1) & (((s >> l) & 1) == 0))
    masks.append(t == s)
    return (jnp.asarray(np.concatenate(mats, 0).astype(np.float32), BF16),
            jnp.asarray(np.stack(masks).astype(np.float32), F32), nlev)


_NT = (((1,), (1,)), ((), ()))
_TN = (((0,), (0,)), ((), ()))


def _gla_body(q_ref, k_ref, v_ref, misc_ref, wgu_ref, bg_ref, gn_ref, mst_ref, lmask_ref, s0_ref,
              o_ref, sout_ref, s_scr, la_scr, *, c, nc, nlev, valid_rows, bb):
    step = pl.program_id(1)
    tile = c * nc

    @pl.when(step == 0)
    def _():
        for bi in range(bb):
            for h in range(GLA_HEADS):
                s_scr[bi, h] = s0_ref[bi, h].T

    for bi in range(bb):
        x = jnp.dot(misc_ref[bi].astype(BF16), wgu_ref[...], preferred_element_type=F32) + bg_ref[...]
        log_a = (jnp.minimum(x, 0.0) - jnp.log1p(jnp.exp(-jnp.abs(x)))) * (1.0 / GLA_GATE_TAU)
        if valid_rows < tile:
            log_a = jnp.where(lax.broadcasted_iota(I32, log_a.shape, 0) < valid_rows, log_a, 0.0)
        la_scr[bi] = log_a
    scale = GLA_HEAD_DK ** -0.5

    def chunk(ci, carry):
        r0 = pl.multiple_of(ci * c, c)
        rows = pl.ds(r0, c)
        chains = [(bi, h) for bi in range(bb) for h in range(GLA_HEADS)]
        cs = []
        for bi in range(bb):
            g_all = la_scr[bi, rows, :]
            g_hi = g_all.astype(BF16)
            g_lo = (g_all - g_hi.astype(F32)).astype(BF16)
            t = jnp.dot(mst_ref[...], jnp.concatenate([g_hi, g_lo], axis=1), preferred_element_type=F32)
            cs.append(t[:, :GLA_DK] + t[:, GLA_DK:])
        ksl = lambda h: slice(h * GLA_HEAD_DK, (h + 1) * GLA_HEAD_DK)
        vsl = lambda h: slice(h * GLA_HEAD_DV, (h + 1) * GLA_HEAD_DV)
        qs, ks, bs, els, o_inter, att = [], [], [], [], [], []
        for bi, h in chains:
            q = q_ref[bi, rows, ksl(h)] * scale
            k = k_ref[bi, rows, ksl(h)]
            b = cs[bi][0:c, ksl(h)]
            qs.append(q), ks.append(k), bs.append(b), els.append(cs[bi][c:2 * c, ksl(h)])
            o_inter.append(lax.dot_general((q * jnp.exp(b)).astype(BF16), s_scr[bi, h].astype(BF16), _NT,
                                           preferred_element_type=F32))
            a = lmask_ref[nlev] * lax.dot_general(q.astype(BF16), k.astype(BF16), _NT,
                                                  preferred_element_type=F32)
            for l in range(nlev):
                e = cs[bi][(2 + l) * c:(3 + l) * c, ksl(h)]
                ql = (q * jnp.exp(jnp.minimum(b - e, 0.0))).astype(BF16)
                kl = (k * jnp.exp(jnp.minimum(e - b, 0.0))).astype(BF16)
                a = a + lmask_ref[l] * lax.dot_general(ql, kl, _NT, preferred_element_type=F32)
            att.append(a)
        vs = [v_ref[bi, rows, vsl(h)].astype(BF16) for bi, h in chains]
        outs = [o_inter[n] + jnp.dot(att[n].astype(BF16), vs[n], preferred_element_type=F32)
                for n in range(len(chains))]
        for n, (bi, h) in enumerate(chains):
            k_st = (ks[n] * jnp.exp(els[n] - bs[n])).astype(BF16)
            s_scr[bi, h] = s_scr[bi, h] * jnp.exp(els[n][0:1, :]) + lax.dot_general(
                vs[n], k_st, _TN, preferred_element_type=F32)
            o_ref[bi, rows, vsl(h)] = _rms(outs[n], gn_ref[...])
        return carry

    lax.fori_loop(0, nc, chunk, 0)

    @pl.when(step == pl.num_programs(1) - 1)
    def _():
        for bi in range(bb):
            for h in range(GLA_HEADS):
                sout_ref[bi, h] = s_scr[bi, h].T


def _gla(q_arr, k_arr, v_arr, misc_arr, cols, wgu_pad, b_gate, gla_norm_g, s0, c, tile, valid_rows, bb):
    nb, l = q_arr.shape[0], q_arr.shape[1]
    assert nb % bb == 0 and l % tile == 0
    mst, lmask, nlev = _gla_consts(c)
    const2 = lambda b, s: (0, 0)
    state_spec = pl.BlockSpec((bb, GLA_HEADS, GLA_HEAD_DK, GLA_HEAD_DV), lambda b, s: (b, 0, 0, 0))
    body = functools.partial(_gla_body, c=c, nc=tile // c, nlev=nlev, valid_rows=valid_rows, bb=bb)
    return pl.pallas_call(
        body,
        grid=(nb // bb, l // tile),
        in_specs=[pl.BlockSpec((bb, tile, GLA_DK), lambda b, s: (b, s, cols[0])),
                  pl.BlockSpec((bb, tile, GLA_DK), lambda b, s: (b, s, cols[1])),
                  pl.BlockSpec((bb, tile, GLA_DV), lambda b, s: (b, s, cols[2])),
                  pl.BlockSpec((bb, tile, LANES), lambda b, s: (b, s, 0)),
                  pl.BlockSpec((LANES, GLA_DK), const2),
                  pl.BlockSpec((1, GLA_DK), const2),
                  pl.BlockSpec((1, GLA_HEAD_DV), const2),
                  pl.BlockSpec(mst.shape, const2),
                  pl.BlockSpec(lmask.shape, lambda b, s: (0, 0, 0)),
                  state_spec],
        out_specs=[pl.BlockSpec((bb, tile, GLA_DV), lambda b, s: (b, s, 0)), state_spec],
        out_shape=[jax.ShapeDtypeStruct((nb, l, GLA_DV), F32),
                   jax.ShapeDtypeStruct((nb, GLA_HEADS, GLA_HEAD_DK, GLA_HEAD_DV), F32)],
        scratch_shapes=[pltpu.VMEM((bb, GLA_HEADS, GLA_HEAD_DV, GLA_HEAD_DK), F32),
                        pltpu.VMEM((bb, tile, GLA_DK), F32)],
        compiler_params=pltpu.CompilerParams(dimension_semantics=("arbitrary", "arbitrary"),
                                             vmem_limit_bytes=VMEM_LIMIT),
        name="gla",
    )(q_arr, k_arr, v_arr, misc_arr, wgu_pad, b_gate.reshape(1, -1), gla_norm_g.reshape(1, -1),
      mst, lmask, s0)


def _score_stats_init(w):
    inf = jnp.full((SUBLANES, w), jnp.inf, F32)
    zero = jnp.zeros((SUBLANES, w), I32)
    return -inf, inf, zero, zero


def _score_stats_update(carry, blk, finite=False):
    mx, mn, c_pos, c_nn = carry
    b3 = blk.reshape(blk.shape[0] // SUBLANES, SUBLANES, blk.shape[1])
    lows = b3 if finite else jnp.where(b3 == -jnp.inf, jnp.inf, b3)
    return (jnp.maximum(mx, jnp.max(b3, axis=0)),
            jnp.minimum(mn, jnp.min(lows, axis=0)),
            c_pos + jnp.sum((b3 > 0.0).astype(I32), axis=0),
            c_nn + jnp.sum((b3 >= 0.0).astype(I32), axis=0))


def _select_threshold(sc_ref, tri_ref, nt, tr, topk, small, lane_ok, stats=None):
    w = sc_ref.shape[1]
    inf = jnp.float32(jnp.inf)

    def over_tiles(fn, init):
        def body(i, carry):
            r0 = pl.multiple_of(i * tr, tr)
            return fn(carry, sc_ref[pl.ds(r0, tr), :], r0)
        return lax.fori_loop(0, nt, body, init)

    fold = lambda x: x.reshape(tr // SUBLANES, SUBLANES, w)
    zeros8 = jnp.zeros((SUBLANES, w), I32)
    pinf8 = jnp.full((SUBLANES, w), inf, F32)

    def count(pred):
        acc = over_tiles(lambda a, blk, r0: a + jnp.sum(fold(pred(blk, r0).astype(I32)), axis=0), zeros8)
        return jnp.sum(acc, axis=0, keepdims=True)

    def min_where(pred):
        acc = over_tiles(lambda a, blk, r0: jnp.minimum(
            a, jnp.min(fold(jnp.where(pred(blk, r0), blk, inf)), axis=0)), pinf8)
        return jnp.min(acc, axis=0, keepdims=True)

    if stats is None:
        stats = over_tiles(lambda carry, blk, r0: _score_stats_update(carry, blk), _score_stats_init(w))
    mx8, mn8, cp8, cn8 = stats
    c_pos = jnp.sum(cp8, axis=0, keepdims=True)
    c_nn = jnp.sum(cn8, axis=0, keepdims=True)
    hi = jnp.where(c_pos >= topk, jnp.max(mx8, axis=0, keepdims=True), 0.0)
    lo = jnp.where(c_nn >= topk, 0.0, jnp.min(mn8, axis=0, keepdims=True))

    live = lane_ok & jnp.logical_not(small)

    def bisect(_, carry):
        lo, hi = carry
        mid = 0.5 * lo + 0.5 * hi
        ge = count(lambda blk, r0: blk >= mid) >= topk
        return jnp.where(ge, mid, lo), jnp.where(ge, hi, mid)

    lo, hi = lax.fori_loop(0, BISECT_STEPS, bisect, (lo, hi))
    v0 = min_where(lambda blk, r0: blk >= lo)

    def gt_next(v):
        def f(carry, blk, r0):
            cg, nx = carry
            gt = blk > v
            return (cg + jnp.sum(fold(gt.astype(I32)), axis=0),
                    jnp.minimum(nx, jnp.min(fold(jnp.where(gt, blk, inf)), axis=0)))
        cg8, nx8 = over_tiles(f, (zeros8, pinf8))
        return jnp.sum(cg8, axis=0, keepdims=True), jnp.min(nx8, axis=0, keepdims=True)

    def peel(state):
        v, _, _ = state
        cg, nx = gt_next(v)
        move = (cg >= topk) & live
        return jnp.where(move, nx, v), cg, jnp.max(move.astype(I32))

    v, cnt_gt, _ = lax.while_loop(lambda s: s[2] > 0, peel,
                                  (v0, jnp.zeros((1, w), I32), jnp.int32(1)))
    cnt_ge = count(lambda blk, r0: blk >= v)
    excess = (cnt_ge > topk) & live

    @pl.when(jnp.max(excess.astype(I32)) > 0)
    def _():
        need = jnp.where(excess, (topk - cnt_gt).astype(F32), jnp.float32(2 ** 30))

        tb = tri_ref.shape[0]

        def drop_surplus(i, seen):
            r0 = pl.multiple_of(i * tr, tr)
            blk = sc_ref[pl.ds(r0, tr), :]
            eq = blk == v
            ones = jnp.where(eq, 1.0, 0.0).astype(BF16)
            ranks = [jnp.dot(tri_ref[...], ones[s * tb:(s + 1) * tb, :], preferred_element_type=F32)
                     for s in range(tr // tb)]
            for s in range(tr // tb):
                rows = slice(s * tb, (s + 1) * tb)
                rank = ranks[s] + seen
                sc_ref[pl.ds(r0 + s * tb, tb), :] = jnp.where(eq[rows], jnp.where(rank > need, -inf, blk[rows]),
                                                              blk[rows])
                seen = rank[tb - 1:tb, :]
            return seen

        lax.fori_loop(0, nt, drop_surplus, jnp.zeros((1, w), F32))

    return jnp.where(small, -inf, v)


def _dsa_prompt_body(relb_ref, btab_ref, tri_ref, ki_ref, qi_ref, wi_ref, kd_ref, vt_ref, qd_ref, o_ref,
                     sc_scr, tbl_scr, q2_scr, acc_scr, m_scr, o_scr, lg_scr, pr_scr, *,
                     topk, far_bucket):
    b = pl.program_id(0)
    j = pl.program_id(1)
    tk = DSA_TK
    hd = DSA_HEAD_DIM

    @pl.when((b == 0) & (j == 0))
    def _():
        q2_scr[...] = jnp.zeros(q2_scr.shape, BF16)

        def build(ci, carry):
            r0 = pl.multiple_of(ci * LANES, LANES)
            bt = btab_ref[pl.ds(r0, LANES), :]
            for h in range(DSA_HEADS):
                t = jnp.zeros(bt.shape, F32)
                for bk in range(REL_BUCKETS):
                    t = jnp.where(bt == bk, (relb_ref[bk, h] - relb_ref[far_bucket, h]) * LOG2E, t)
                tbl_scr[h, pl.ds(r0, LANES), :] = t
            return carry

        lax.fori_loop(0, BIAS_TAB_ROWS // LANES, build, 0)

    nt = j // (tk // QUERY_BLOCK) + 1
    qpos = j * QUERY_BLOCK + lax.broadcasted_iota(I32, (1, LANES), 1)
    for p in range(DSA_HEADS // 2):
        q2_scr[p, 0:hd, 0:LANES] = qd_ref[2 * p * hd:(2 * p + 1) * hd, :]
        q2_scr[p, hd:2 * hd, LANES:2 * LANES] = qd_ref[(2 * p + 1) * hd:(2 * p + 2) * hd, :]

    wi = wi_ref[...]

    def score_tile(i, carry, last):
        r0 = pl.multiple_of(i * tk, tk)
        s4 = jnp.dot(ki_ref[pl.ds(r0, tk), :], qi_ref[...], preferred_element_type=F32)
        sc = jnp.zeros((tk, LANES), F32)
        for h in range(IDX_HEADS):
            sc = sc + jnp.maximum(s4[:, h * LANES:(h + 1) * LANES], 0.0) * wi[h:h + 1, :]
        if last:
            kpos = r0 + lax.broadcasted_iota(I32, (tk, LANES), 0)
            sc = jnp.where(kpos <= qpos, sc, -jnp.inf)
        sc_scr[pl.ds(r0, tk), :] = sc
        return _score_stats_update(carry, sc, finite=not last)

    stats = lax.fori_loop(0, nt - 1, lambda i, c: score_tile(i, c, False), _score_stats_init(LANES))
    stats = score_tile(nt - 1, stats, True)

    small = (qpos + 1) < topk
    kstar = _select_threshold(sc_scr, tri_ref, nt, tk, topk, small, jnp.full((1, LANES), True), stats)

    m_scr[...] = jnp.full(m_scr.shape, NEG, F32)
    acc_scr[...] = jnp.zeros(acc_scr.shape, F32)
    vrows = acc_scr.shape[0] // DSA_HEADS

    def logits_stage(i, buf, near):
        r0 = pl.multiple_of(i * tk, tk)
        blk = sc_scr[pl.ds(r0, tk), :]
        if near:
            kpos = r0 + lax.broadcasted_iota(I32, (tk, LANES), 0)
            addm = jnp.where(blk >= kstar, jnp.where(kpos <= qpos, 0.0, NEG), NEG)
            off = pl.multiple_of(i * tk - j * QUERY_BLOCK + BIAS_TAB_OFF, LANES)
        else:
            addm = jnp.where(blk >= kstar, 0.0, NEG)
        tile_max = []
        for p in range(DSA_HEADS // 2):
            lg2 = jnp.dot(kd_ref[pl.ds(r0, tk), p * LANES:(p + 1) * LANES], q2_scr[p],
                          preferred_element_type=F32)
            for hh in range(2):
                h = 2 * p + hh
                lg = lg2[:, hh * LANES:(hh + 1) * LANES] + addm
                if near:
                    lg = lg + tbl_scr[h, pl.ds(off, tk), :]
                lg_scr[buf, h] = lg
                tile_max.append(jnp.max(lg, axis=0, keepdims=True))
        return tuple(tile_max)

    def softmax_pv_stage(i, buf, tile_max):
        alpha = []
        for h in range(DSA_HEADS):
            m_old = m_scr[h:h + 1, :]
            m_new = jnp.maximum(m_old, tile_max[h])
            m_scr[h:h + 1, :] = m_new
            alpha.append(jnp.exp2(m_old - m_new))
            pr_scr[h] = jnp.exp2((lg_scr[buf, h] - m_new).astype(BF16))
        for h in range(DSA_HEADS):
            rows = slice(h * vrows, (h + 1) * vrows)
            acc_scr[rows, :] = alpha[h] * acc_scr[rows, :] + jnp.dot(
                vt_ref[i, rows, :], pr_scr[h], preferred_element_type=F32)

    near_tiles = jnp.where(j % (tk // QUERY_BLOCK) == 0, 2, 1)
    n_far = jnp.maximum(nt - near_tiles, 0)

    @pl.when(n_far > 0)
    def _():
        n_pairs = (n_far - 1) // 2

        def pair(it, tile_max):
            i0 = 2 * it
            max1 = logits_stage(i0 + 1, 1, False)
            softmax_pv_stage(i0, 0, tile_max)
            max2 = logits_stage(i0 + 2, 0, False)
            softmax_pv_stage(i0 + 1, 1, max1)
            return max2

        tile_max = lax.fori_loop(0, n_pairs, pair, logits_stage(0, 0, False))
        last = 2 * n_pairs

        @pl.when(n_far - last == 1)
        def _():
            softmax_pv_stage(last, 0, tile_max)

        @pl.when(n_far - last == 2)
        def _():
            max1 = logits_stage(last + 1, 1, False)
            softmax_pv_stage(last, 0, tile_max)
            softmax_pv_stage(last + 1, 1, max1)

    def near_tile(i, carry):
        softmax_pv_stage(i, 0, logits_stage(i, 0, True))
        return carry

    lax.fori_loop(n_far, nt, near_tile, 0)

    for h in range(DSA_HEADS):
        o_scr[h * hd:(h + 1) * hd, :] = (acc_scr[h * vrows:h * vrows + hd, :]
                                         / acc_scr[h * vrows + hd:h * vrows + hd + 1, :])
    o_ref[...] = o_scr[...].T.astype(BF16)


def _tri_ones(n):
    return jnp.asarray(np.tril(np.ones((n, n), np.float32)), BF16)


def _bias_bucket_table():
    u = np.arange(BIAS_TAB_ROWS)[:, None]
    r = np.arange(LANES)[None, :]
    return jnp.asarray(_t5_bucket_np(r + BIAS_TAB_OFF - u), I32)


def _dsa_prompt(qd, kd_bf, vt, misc, kin, rel_bias, nb, l):
    nq = l // QUERY_BLOCK
    topk = min(TOPK_MAX, l // 4)
    assert l % DSA_TK == 0
    far = _t5_bucket_np(np.arange(REL_MAX_DIST, max(l, REL_MAX_DIST + 1)))
    assert (far == far[0]).all()
    ki = kin.reshape(nb, l, IDX_DIM).astype(BF16)
    qi = misc[:, MISC_QI:MISC_QI + IDX_HEADS * IDX_DIM].reshape(nb, nq, QUERY_BLOCK, IDX_HEADS, IDX_DIM)
    qi = qi.transpose(0, 4, 1, 3, 2).reshape(nb, IDX_DIM, nq * IDX_HEADS * QUERY_BLOCK).astype(BF16)
    wi = misc[:, MISC_WI:MISC_WI + IDX_HEADS] * ((IDX_DIM ** -0.5) * (IDX_HEADS ** -0.5))
    wi = wi.reshape(nb, nq, QUERY_BLOCK, IDX_HEADS).transpose(0, 1, 3, 2)
    wi = jnp.pad(wi, ((0, 0), (0, 0), (0, SUBLANES - IDX_HEADS), (0, 0))).reshape(nb, nq * SUBLANES, QUERY_BLOCK)
    kd_bf = kd_bf.reshape(nb, l, DSA_WIDTH)
    body = functools.partial(_dsa_prompt_body, topk=topk, far_bucket=int(far[0]))
    whole = lambda b, j: (b, 0, 0)
    return pl.pallas_call(
        body,
        grid=(nb, nq),
        in_specs=[pl.BlockSpec(memory_space=pltpu.SMEM),
                  pl.BlockSpec((BIAS_TAB_ROWS, LANES), lambda b, j: (0, 0), pipeline_mode=pl.Buffered(1)),
                  pl.BlockSpec((LANES, LANES), lambda b, j: (0, 0), pipeline_mode=pl.Buffered(1)),
                  pl.BlockSpec((None, l, IDX_DIM), whole, pipeline_mode=pl.Buffered(1)),
                  pl.BlockSpec((None, IDX_DIM, IDX_HEADS * QUERY_BLOCK), lambda b, j: (b, 0, j)),
                  pl.BlockSpec((None, SUBLANES, QUERY_BLOCK), lambda b, j: (b, j, 0)),
                  pl.BlockSpec((None, l, DSA_WIDTH), whole, pipeline_mode=pl.Buffered(1)),
                  pl.BlockSpec((None, l // DSA_TK, DSA_HEADS * V_ROWS, DSA_TK), lambda b, j: (b, 0, 0, 0),
                               pipeline_mode=pl.Buffered(1)),
                  pl.BlockSpec((None, DSA_WIDTH, QUERY_BLOCK), lambda b, j: (b, 0, j))],
        out_specs=pl.BlockSpec((None, QUERY_BLOCK, DSA_WIDTH), lambda b, j: (b, j, 0)),
        out_shape=jax.ShapeDtypeStruct((nb, l, DSA_WIDTH), BF16),
        scratch_shapes=[pltpu.VMEM((l, LANES), F32),
                        pltpu.VMEM((DSA_HEADS, BIAS_TAB_ROWS, LANES), F32),
                        pltpu.VMEM((DSA_HEADS // 2, LANES, 2 * LANES), BF16),
                        pltpu.VMEM((DSA_HEADS * V_ROWS, LANES), F32),
                        pltpu.VMEM((DSA_HEADS, LANES), F32),
                        pltpu.VMEM((DSA_WIDTH, LANES), F32),
                        pltpu.VMEM((2, DSA_HEADS, DSA_TK, LANES), F32),
                        pltpu.VMEM((DSA_HEADS, DSA_TK, LANES), BF16)],
        compiler_params=pltpu.CompilerParams(dimension_semantics=("arbitrary", "arbitrary"),
                                             vmem_limit_bytes=VMEM_LIMIT),
        name="dsa_prompt",
    )(rel_bias, _bias_bucket_table(), _tri_ones(LANES), ki, qi, wi, kd_bf, vt, qd)


def _select_threshold_lanes(sc_ref, triu_ref, nt, topk, small, row_ok):
    r = sc_ref.shape[0]
    inf = jnp.float32(jnp.inf)
    tiles = [slice(i * LANES, (i + 1) * LANES) for i in range(nt)]
    rowsum = lambda x: jnp.sum(x, axis=1, keepdims=True)
    rowmin = lambda x: jnp.min(x, axis=1, keepdims=True)

    def count(pred):
        acc = jnp.zeros((r, LANES), I32)
        for t in tiles:
            acc = acc + pred(sc_ref[:, t]).astype(I32)
        return rowsum(acc)

    def min_where(pred):
        acc = jnp.full((r, LANES), inf, F32)
        for t in tiles:
            blk = sc_ref[:, t]
            acc = jnp.minimum(acc, jnp.where(pred(blk), blk, inf))
        return rowmin(acc)

    mx = jnp.full((r, LANES), -inf, F32)
    mn = jnp.full((r, LANES), inf, F32)
    c_pos = jnp.zeros((r, LANES), I32)
    c_nn = jnp.zeros((r, LANES), I32)
    for t in tiles:
        blk = sc_ref[:, t]
        mx = jnp.maximum(mx, blk)
        mn = jnp.minimum(mn, jnp.where(blk == -inf, inf, blk))
        c_pos = c_pos + (blk > 0.0).astype(I32)
        c_nn = c_nn + (blk >= 0.0).astype(I32)
    hi = jnp.where(rowsum(c_pos) >= topk, jnp.max(mx, axis=1, keepdims=True), 0.0)
    lo = jnp.where(rowsum(c_nn) >= topk, 0.0, rowmin(mn))
    live = row_ok & jnp.logical_not(small)

    def bisect(_, carry):
        lo, hi = carry
        mid = 0.5 * lo + 0.5 * hi
        ge = count(lambda blk: blk >= mid) >= topk
        return jnp.where(ge, mid, lo), jnp.where(ge, hi, mid)

    lo, hi = lax.fori_loop(0, BISECT_STEPS, bisect, (lo, hi))
    v0 = min_where(lambda blk: blk >= lo)

    def peel(state):
        v, _, _ = state
        cg = jnp.zeros((r, LANES), I32)
        nx = jnp.full((r, LANES), inf, F32)
        for t in tiles:
            blk = sc_ref[:, t]
            gt = blk > v
            cg = cg + gt.astype(I32)
            nx = jnp.minimum(nx, jnp.where(gt, blk, inf))
        cg = rowsum(cg)
        move = (cg >= topk) & live
        return jnp.where(move, rowmin(nx), v), cg, jnp.max(move.astype(I32))

    v, cnt_gt, _ = lax.while_loop(lambda s: s[2] > 0, peel,
                                  (v0, jnp.zeros((r, 1), I32), jnp.int32(1)))
    excess = (count(lambda blk: blk >= v) > topk) & live

    @pl.when(jnp.max(excess.astype(I32)) > 0)
    def _():
        need = (topk - cnt_gt).astype(F32)
        seen = jnp.zeros((r, 1), F32)
        for t in tiles:
            blk = sc_ref[:, t]
            eq = blk == v
            rank = seen + jnp.dot(jnp.where(eq, 1.0, 0.0).astype(BF16), triu_ref[...],
                                  preferred_element_type=F32)
            sc_ref[:, t] = jnp.where(eq & (rank > need) & excess, -inf, blk)
            seen = rank[:, LANES - 1:LANES]

    return jnp.where(small, -inf, v)


SEQ_GROUP = LANES // SUBLANES
NEW_ROWS = 16


def _dsa_sample_select_body(pt_ref, *refs, n_pages, t_new, topk):
    page_refs = refs[:n_pages]
    knew_ref, qi_ref, wi_ref, triu_ref, mask_ref, sc_scr = refs[n_pages:]
    g = pl.program_id(1)
    past = n_pages * PAGE_SIZE
    rows = pl.ds(pl.multiple_of(g * SUBLANES, SUBLANES), SUBLANES)
    for p in range(n_pages + 1):
        keys_t = (page_refs[p][...] if p < n_pages else knew_ref[...]).astype(BF16)
        s = jnp.dot(qi_ref[...], keys_t, preferred_element_type=F32)
        sc = jnp.zeros((SUBLANES, PAGE_SIZE), F32)
        for h in range(IDX_HEADS):
            hs = slice(h * SUBLANES, (h + 1) * SUBLANES)
            sc = sc + jnp.maximum(s[hs, :], 0.0) * wi_ref[hs, :]
        sc_scr[rows, p * PAGE_SIZE:(p + 1) * PAGE_SIZE] = sc

    @pl.when(g == SEQ_GROUP - 1)
    def _():
        q_of_row = lax.broadcasted_iota(I32, (LANES, 1), 0) % SUBLANES
        new = slice(past, past + PAGE_SIZE)
        cpos = lax.broadcasted_iota(I32, (LANES, PAGE_SIZE), 1)
        sc_scr[:, new] = jnp.where(cpos <= q_of_row, sc_scr[:, new], -jnp.inf)
        row_ok = q_of_row < t_new
        small = (past + q_of_row + 1) < topk
        kstar = _select_threshold_lanes(sc_scr, triu_ref, n_pages + 1, topk, small, row_ok)
        for p in range(n_pages + 1):
            t = slice(p * PAGE_SIZE, (p + 1) * PAGE_SIZE)
            mask_ref[:, t] = jnp.where(sc_scr[:, t] >= kstar, 1.0, 0.0)


def _dsa_sample_select(page_table, kidx_t, kin_new_t, qi_rows, wi_rows, t_new, topk):
    db, n_pages = page_table.shape
    ng = db // SEQ_GROUP
    width = (n_pages + 1) * PAGE_SIZE
    seq = lambda gi, g, pt: (gi * SEQ_GROUP + g, 0, 0)
    page_specs = [pl.BlockSpec((None, IDX_DIM, PAGE_SIZE),
                               functools.partial(lambda gi, g, pt, p: (pt[gi * SEQ_GROUP + g, p], 0, 0), p=p))
                  for p in range(n_pages)]
    body = functools.partial(_dsa_sample_select_body, n_pages=n_pages, t_new=t_new, topk=topk)
    grid_spec = pltpu.PrefetchScalarGridSpec(
        num_scalar_prefetch=1,
        grid=(ng, SEQ_GROUP),
        in_specs=page_specs + [pl.BlockSpec((None, IDX_DIM, PAGE_SIZE), seq),
                               pl.BlockSpec((None, IDX_HEADS * SUBLANES, IDX_DIM), seq),
                               pl.BlockSpec((None, IDX_HEADS * SUBLANES, PAGE_SIZE), seq),
                               pl.BlockSpec((LANES, LANES), lambda gi, g, pt: (0, 0))],
        out_specs=pl.BlockSpec((None, LANES, width), lambda gi, g, pt: (gi, 0, 0)),
        scratch_shapes=[pltpu.VMEM((LANES, width), F32)],
    )
    triu = jnp.asarray(np.triu(np.ones((LANES, LANES), np.float32)), BF16)
    return pl.pallas_call(
        body, grid_spec=grid_spec,
        out_shape=jax.ShapeDtypeStruct((ng, LANES, width), F32),
        compiler_params=pltpu.CompilerParams(dimension_semantics=("arbitrary", "arbitrary"),
                                             vmem_limit_bytes=VMEM_LIMIT),
        name="dsa_sample_select",
    )(page_table, *([kidx_t] * n_pages), kin_new_t, qi_rows, wi_rows, triu)


def _dsa_sample_attend_body(pt_ref, *refs, n_pages):
    k_refs = refs[:n_pages]
    v_refs = refs[n_pages:2 * n_pages]
    (knew_ref, vnew_ref, mask_ref, qbd_ref, btab_ref, rb_ref, o_ref, lg_scr, tbl_scr) = refs[2 * n_pages:]
    b = pl.program_id(0)
    hq = DSA_HEADS * SUBLANES

    @pl.when(b == 0)
    def _():
        for half in range(2):
            cols = slice(half * PAGE_SIZE, (half + 1) * PAGE_SIZE)
            bt = btab_ref[:, cols]
            t = jnp.zeros(bt.shape, F32)
            for bk in range(REL_BUCKETS):
                t = jnp.where(bt == bk, rb_ref[bk], t)
            tbl_scr[:, cols] = t

    far_bias = rb_ref[REL_BUCKETS - 1]
    page = lambda ref: ref[...].reshape(DSA_WIDTH, PAGE_SIZE).astype(BF16)

    def masked(lg, sel):
        return lg + jnp.where(jnp.concatenate([sel] * DSA_HEADS, axis=0) > 0.5, 0.0, NEG)

    m = jnp.full((hq, LANES), NEG, F32)
    for p in range(n_pages):
        cols = slice(p * PAGE_SIZE, (p + 1) * PAGE_SIZE)
        lg = jnp.dot(qbd_ref[...], page(k_refs[p]), preferred_element_type=F32)
        lg = lg + (tbl_scr[:, 0:PAGE_SIZE] if p == n_pages - 1 else far_bias)
        lg = masked(lg, mask_ref[:, cols])
        lg_scr[:, cols] = lg
        m = jnp.maximum(m, lg)
    past = n_pages * PAGE_SIZE
    lg_new = lax.dot_general(qbd_ref[...], knew_ref[...].astype(BF16), _NT, preferred_element_type=F32)
    lg_new = masked(lg_new + tbl_scr[:, PAGE_SIZE:PAGE_SIZE + NEW_ROWS], mask_ref[:, past:past + NEW_ROWS])
    m = jnp.maximum(jnp.max(m, axis=1, keepdims=True), jnp.max(lg_new, axis=1, keepdims=True))

    pr_new = jnp.exp(lg_new - m)
    acc = jnp.dot(pr_new.astype(BF16), vnew_ref[...].astype(BF16), preferred_element_type=F32)
    lsum = jnp.zeros((hq, LANES), F32)
    for p in range(n_pages):
        cols = slice(p * PAGE_SIZE, (p + 1) * PAGE_SIZE)
        pr = jnp.exp(lg_scr[:, cols] - m)
        lsum = lsum + pr
        acc = acc + lax.dot_general(pr.astype(BF16), page(v_refs[p]), _NT, preferred_element_type=F32)
    lsum = jnp.sum(lsum, axis=1, keepdims=True) + jnp.sum(pr_new, axis=1, keepdims=True)
    acc = acc / lsum
    head_of_lane = lax.broadcasted_iota(I32, (SUBLANES, DSA_WIDTH), 1) // DSA_HEAD_DIM
    out = jnp.zeros((SUBLANES, DSA_WIDTH), F32)
    for h in range(DSA_HEADS):
        out = jnp.where(head_of_lane == h, acc[h * SUBLANES:(h + 1) * SUBLANES, :], out)
    o_ref[...] = out


def _dsa_sample_attend(page_table, k_t, v_t, k_new, v_new, mask, qbd, rel_bias):
    db, n_pages = page_table.shape
    hq = DSA_HEADS * SUBLANES
    width = (n_pages + 1) * PAGE_SIZE
    seq = lambda b, pt: (b, 0, 0)
    page = lambda p: functools.partial(lambda b, pt, p: (pt[b, p], 0, 0, 0), p=p)
    kv_spec = lambda p: pl.BlockSpec((None, DSA_HEADS, DSA_HEAD_DIM, PAGE_SIZE), page(p))
    u = np.arange(2 * PAGE_SIZE)[None, :]
    q = (np.arange(hq) % SUBLANES)[:, None]
    btab = jnp.asarray(_t5_bucket_np(PAGE_SIZE + q - u), I32)
    rb = jnp.broadcast_to(jnp.repeat(rel_bias, SUBLANES, axis=1)[:, :, None], (REL_BUCKETS, hq, PAGE_SIZE))
    grid_spec = pltpu.PrefetchScalarGridSpec(
        num_scalar_prefetch=1,
        grid=(db,),
        in_specs=[kv_spec(p) for p in range(n_pages)] + [kv_spec(p) for p in range(n_pages)] + [
            pl.BlockSpec((None, NEW_ROWS, DSA_WIDTH), seq),
            pl.BlockSpec((None, NEW_ROWS, DSA_WIDTH), seq),
            pl.BlockSpec((None, SUBLANES, width), lambda b, pt: (b // SEQ_GROUP, b % SEQ_GROUP, 0)),
            pl.BlockSpec((None, hq, DSA_WIDTH), seq),
            pl.BlockSpec(btab.shape, lambda b, pt: (0, 0)),
            pl.BlockSpec(rb.shape, lambda b, pt: (0, 0, 0))],
        out_specs=pl.BlockSpec((None, SUBLANES, DSA_WIDTH), seq),
        scratch_shapes=[pltpu.VMEM((hq, width - PAGE_SIZE), F32), pltpu.VMEM((hq, 2 * PAGE_SIZE), F32)],
    )
    return pl.pallas_call(
        functools.partial(_dsa_sample_attend_body, n_pages=n_pages), grid_spec=grid_spec,
        out_shape=jax.ShapeDtypeStruct((db, SUBLANES, DSA_WIDTH), F32),
        compiler_params=pltpu.CompilerParams(dimension_semantics=("arbitrary",),
                                             vmem_limit_bytes=VMEM_LIMIT),
        name="dsa_sample_attend",
    )(page_table, *([k_t] * n_pages), *([v_t] * n_pages), k_new, v_new, mask, qbd, btab, rb)


def _dsa_sample(zA, kd, vd, misc, kin, rel_bias, cache_k, cache_v, cache_kidx, page_table, db, t_new):
    n_pages = page_table.shape[1]
    past = n_pages * PAGE_SIZE
    topk = min(TOPK_MAX, (past + t_new) // 4)
    assert db % SEQ_GROUP == 0 and t_new <= SUBLANES
    new_t = lambda a, w: jnp.pad(a.reshape(db, t_new, w).transpose(0, 2, 1),
                                 ((0, 0), (0, 0), (0, PAGE_SIZE - t_new)))
    pad_q = lambda a: jnp.pad(a, ((0, 0), (0, 0), (0, SUBLANES - t_new), (0, 0)))
    new_rows = lambda a: jnp.pad(a.reshape(db, t_new, DSA_WIDTH), ((0, 0), (0, NEW_ROWS - t_new), (0, 0)))
    qi = misc[:, MISC_QI:MISC_QI + IDX_HEADS * IDX_DIM].reshape(db, t_new, IDX_HEADS, IDX_DIM)
    qi_rows = pad_q(qi.transpose(0, 2, 1, 3)).reshape(db, IDX_HEADS * SUBLANES, IDX_DIM).astype(BF16)
    wi = misc[:, MISC_WI:MISC_WI + IDX_HEADS] * ((IDX_DIM ** -0.5) * (IDX_HEADS ** -0.5))
    wi = pad_q(wi.reshape(db, t_new, IDX_HEADS, 1).transpose(0, 2, 1, 3)).reshape(db, IDX_HEADS * SUBLANES, 1)
    wi_rows = jnp.broadcast_to(wi, (db, IDX_HEADS * SUBLANES, PAGE_SIZE))
    mask = _dsa_sample_select(page_table, cache_kidx.transpose(0, 2, 1), new_t(kin, IDX_DIM),
                              qi_rows, wi_rows, t_new, topk)
    qd = (zA[:, ZA_QD:ZA_QD + DSA_WIDTH] * (DSA_HEAD_DIM ** -0.5)).reshape(db, t_new, DSA_HEADS, DSA_HEAD_DIM)
    qd = pad_q(qd.transpose(0, 2, 1, 3))
    eye = jnp.eye(DSA_HEADS, dtype=qd.dtype)
    qbd = (qd[:, :, :, None, :] * eye[None, :, None, :, None]).reshape(db, DSA_HEADS * SUBLANES, DSA_WIDTH)
    o = _dsa_sample_attend(page_table, cache_k.transpose(0, 2, 3, 1), cache_v.transpose(0, 2, 3, 1),
                           new_rows(kd), new_rows(vd), mask, qbd.astype(BF16), rel_bias)
    return o[:, :t_new, :].reshape(db * t_new, DSA_WIDTH).astype(BF16)


def _post_body(x_ref, og_ref, rg_ref, gg_ref, gd_ref, od_ref, wg_ref, wd_ref, wo_ref, gf_ref,
               x1_ref, hf_ref):
    half = x_ref.shape[0] // 2
    parts = [slice(0, half), slice(half, 2 * half)]
    branch = []
    for r in parts:
        rg = rg_ref[r, :]
        a = (og_ref[r, :] * (rg * _sigmoid(rg))).astype(BF16)
        branch.append((jnp.dot(a, wg_ref[...], preferred_element_type=F32),
                       jnp.dot(od_ref[r, :], wd_ref[...], preferred_element_type=F32)))
    for r, (y_g, y_d) in zip(parts, branch):
        mix = (_sigmoid(gg_ref[r, :]) * y_g + _sigmoid(gd_ref[r, :]) * y_d).astype(BF16)
        x1 = x_ref[r, :] + jnp.dot(mix, wo_ref[...], preferred_element_type=F32)
        x1_ref[r, :] = x1
        hf_ref[r, :] = _rms(x1, gf_ref[...]).astype(BF16)


def _post(x2d, og, zA, od, w_gla, w_dsa, w_o, g_ffn):
    n = x2d.shape[0]
    tm = min(n, 512)
    row = lambda i: (i, 0)
    const = lambda i: (0, 0)
    return pl.pallas_call(
        _post_body,
        grid=(n // tm,),
        in_specs=[pl.BlockSpec((tm, D_MODEL), row),
                  pl.BlockSpec((tm, GLA_DV), row),
                  pl.BlockSpec((tm, GLA_DV), lambda i: (i, COL_RG)),
                  pl.BlockSpec((tm, D_MODEL), lambda i: (i, COL_GG)),
                  pl.BlockSpec((tm, D_MODEL), lambda i: (i, COL_GD)),
                  pl.BlockSpec((tm, DSA_WIDTH), row),
                  pl.BlockSpec((GLA_DV, D_MODEL), const),
                  pl.BlockSpec((DSA_WIDTH, D_MODEL), const),
                  pl.BlockSpec((D_MODEL, D_MODEL), const),
                  pl.BlockSpec((1, D_MODEL), const)],
        out_specs=[pl.BlockSpec((tm, D_MODEL), row), pl.BlockSpec((tm, D_MODEL), row)],
        out_shape=[jax.ShapeDtypeStruct((n, D_MODEL), F32), jax.ShapeDtypeStruct((n, D_MODEL), BF16)],
        compiler_params=pltpu.CompilerParams(dimension_semantics=("arbitrary",),
                                             vmem_limit_bytes=VMEM_LIMIT),
        name="post_mix",
    )(x2d, og, zA, zA, zA, od, w_gla, w_dsa, w_o, g_ffn.reshape(1, -1))


FFN_TILE = 256
FFN_ROWS = 512


def _ffn_body(hf_ref, x1_ref, wg_ref, wu_ref, wd_ref, gfin_ref, y_ref, acc_scr):
    hf = hf_ref[...]
    n_chunks = wg_ref.shape[1] // FFN_TILE
    cols = lambda k: slice(k * FFN_TILE, (k + 1) * FFN_TILE)

    def gate_up(k):
        return (jnp.dot(hf, wg_ref[:, cols(k)], preferred_element_type=F32),
                jnp.dot(hf, wu_ref[:, cols(k)], preferred_element_type=F32))

    acc_scr[...] = x1_ref[...]
    nxt = gate_up(0)
    for k in range(n_chunks):
        gate, up = nxt
        if k + 1 < n_chunks:
            nxt = gate_up(k + 1)
        act = (gate * _sigmoid(gate) * up).astype(BF16)
        acc_scr[...] = acc_scr[...] + jnp.dot(act, wd_ref[cols(k), :], preferred_element_type=F32)
    y_ref[...] = _rms(acc_scr[...], gfin_ref[...])


def _ffn(hf, x1, w_gate, w_up, w_down, g_final):
    n = hf.shape[0]
    d_ff = w_gate.shape[1]
    assert d_ff % FFN_TILE == 0
    tm = min(n, FFN_ROWS)
    row = lambda i: (i, 0)
    const = lambda i: (0, 0)
    resident = lambda shape: pl.BlockSpec(shape, const, pipeline_mode=pl.Buffered(1))
    return pl.pallas_call(
        _ffn_body,
        grid=(n // tm,),
        in_specs=[pl.BlockSpec((tm, D_MODEL), row),
                  pl.BlockSpec((tm, D_MODEL), row),
                  resident((D_MODEL, d_ff)),
                  resident((D_MODEL, d_ff)),
                  resident((d_ff, D_MODEL)),
                  pl.BlockSpec((1, D_MODEL), const)],
        out_specs=pl.BlockSpec((tm, D_MODEL), row),
        out_shape=jax.ShapeDtypeStruct((n, D_MODEL), F32),
        scratch_shapes=[pltpu.VMEM((tm, D_MODEL), F32)],
        compiler_params=pltpu.CompilerParams(dimension_semantics=("arbitrary",),
                                             vmem_limit_bytes=VMEM_LIMIT),
        name="ffn",
    )(hf, x1, w_gate, w_up, w_down, g_final.reshape(1, -1))


def _gate_weight_pad(w_gate_up):
    pad = jnp.zeros((LANES, GLA_DK), w_gate_up.dtype)
    return pad.at[MISC_ALOW:MISC_ALOW + GLA_GATE_RANK].set(w_gate_up).astype(BF16)


def kernel(x_prompt, x_sample, cache_k, cache_v, cache_kidx, state_gla, page_table, g_mix, w_in,
           w_gate_up, b_gate, gla_norm_g, w_gla_branch, idx_k_g, idx_k_b, w_dsa_branch, w_o, g_ffn,
           w_ffn_gate, w_ffn_up, w_ffn_down, rel_bias, g_final):
    depth = w_in.shape[0]
    assert depth == 1, "the final RMSNorm is fused into the FFN kernel of the single layer"
    nb, l, _ = x_prompt.shape
    db, t_new, _ = x_sample.shape
    layer = 0
    w_parts = _split_w_in(w_in[layer])
    wgu_pad = _gate_weight_pad(w_gate_up[layer])
    w_gla = w_gla_branch[layer].astype(BF16)
    w_dsa = w_dsa_branch[layer].astype(BF16)
    w_out = w_o[layer].astype(BF16)
    w_fg, w_fu, w_fd = (w.astype(BF16) for w in (w_ffn_gate[layer], w_ffn_up[layer], w_ffn_down[layer]))

    xp = x_prompt.reshape(nb * l, D_MODEL)
    zA, qdt, kdb, kt, vt, vslab, misc, kin = _in_proj_prompt(xp, g_mix[layer], w_parts, idx_k_g[layer],
                                                             idx_k_b[layer], nb, l)
    zA3 = zA.reshape(nb, l, zA.shape[1])
    s0 = jnp.zeros((nb, GLA_HEADS, GLA_HEAD_DK, GLA_HEAD_DV), state_gla.dtype)
    og, s_p = _gla(zA3, zA3, zA3, misc.reshape(nb, l, PROJ_TILE), (COL_QG, COL_KG, COL_VG), wgu_pad,
                   b_gate[layer], gla_norm_g[layer], s0, GLA_CHUNK, GLA_TILE, GLA_TILE, nb)
    od = _dsa_prompt(qdt, kdb, vslab, misc, kin, rel_bias, nb, l)
    x1, hf = _post(xp, og.reshape(nb * l, GLA_DV), zA, od.reshape(nb * l, DSA_WIDTH), w_gla, w_dsa,
                   w_out, g_ffn[layer])
    y_p = _ffn(hf, x1, w_fg, w_fu, w_fd, g_final).reshape(nb, l, D_MODEL)

    xs = x_sample.reshape(db * t_new, D_MODEL)
    zA_s, kd_s, vd_s, misc_s, kin_s = _in_proj(xs, g_mix[layer], w_parts, idx_k_g[layer], idx_k_b[layer])
    pad_t = lambda a: jnp.pad(a.reshape(db, t_new, -1), ((0, 0), (0, SAMPLE_CHUNK - t_new), (0, 0)))
    og_s, s_s = _gla(pad_t(zA_s[:, 0:GLA_DK]), pad_t(zA_s[:, GLA_DK:2 * GLA_DK]),
                     pad_t(zA_s[:, 2 * GLA_DK:2 * GLA_DK + GLA_DV]), pad_t(misc_s[:, 0:LANES]), (0, 0, 0),
                     wgu_pad, b_gate[layer], gla_norm_g[layer], state_gla[layer], SAMPLE_CHUNK, SAMPLE_CHUNK, t_new, SAMPLE_GLA_SEQS)
    og_s = og_s[:, :t_new, :].reshape(db * t_new, GLA_DV)
    od_s = _dsa_sample(zA_s, kd_s, vd_s, misc_s, kin_s, rel_bias, cache_k[layer], cache_v[layer],
                       cache_kidx[layer], page_table, db, t_new)
    x1_s, hf_s = _post(xs, og_s, zA_s, od_s, w_gla, w_dsa, w_out, g_ffn[layer])
    y_s = _ffn(hf_s, x1_s, w_fg, w_fu, w_fd, g_final).reshape(db, t_new, D_MODEL)

    heads = lambda a, n, t: a.reshape(1, n, t, DSA_HEADS, DSA_HEAD_DIM)
    heads_t = lambda a: a.reshape(nb, DSA_HEADS, DSA_HEAD_DIM, l).transpose(0, 3, 1, 2)[None]
    return (y_p, y_s,
            heads_t(kt), heads_t(vt), kin.reshape(1, nb, l, IDX_DIM), s_p[None],
            heads(kd_s, db, t_new), heads(vd_s, db, t_new), kin_s.reshape(1, db, t_new, IDX_DIM), s_s[None])
```

```python
import functools
import math

import numpy as np
import jax
import jax.numpy as jnp
from jax import lax
from jax.experimental import pallas as pl
from jax.experimental.pallas import tpu as pltpu

F32, BF16, I32 = jnp.float32, jnp.bfloat16, jnp.int32

D_MODEL = 1024
GLA_HEADS = 4
GLA_HEAD_DK = 128
GLA_HEAD_DV = 256
GLA_DK = GLA_HEADS * GLA_HEAD_DK
GLA_DV = GLA_HEADS * GLA_HEAD_DV
GLA_GATE_RANK = 16
GLA_GATE_TAU = 16.0
DSA_HEADS = 8
DSA_HEAD_DIM = 64
DSA_WIDTH = DSA_HEADS * DSA_HEAD_DIM
IDX_HEADS = 4
IDX_DIM = 64
TOPK_MAX = 256
QUERY_BLOCK = 128
PAGE_SIZE = 128
REL_BUCKETS = 32
REL_MAX_DIST = 128
RMS_EPS = 1e-6
LN_EPS = 1e-6
SPLIT_SIZES = (GLA_DK, GLA_DK, GLA_DV, GLA_DV, GLA_GATE_RANK, DSA_WIDTH, DSA_WIDTH, DSA_WIDTH,
               IDX_HEADS * IDX_DIM, IDX_DIM, IDX_HEADS, D_MODEL, D_MODEL)

LANES = 128
SUBLANES = 8
VMEM_LIMIT = 56 * 1024 * 1024

PROJ_TILE = 512
ZA_TILES = 11
ZA_WIDTH = ZA_TILES * PROJ_TILE
COL_QG, COL_KG = 0, 1
COL_VG, COL_RG, COL_GG, COL_GD = 1, 2, 3, 4
ZA_QD = 10 * PROJ_TILE
MISC_ALOW, MISC_WI, MISC_QI = 64, 80, 128

GLA_CHUNK = 64
GLA_TILE = 256
SAMPLE_CHUNK = 16
SAMPLE_GLA_SEQS = 4
DSA_TK = 512
BIAS_TAB_OFF = 2 * DSA_TK - QUERY_BLOCK
BIAS_TAB_ROWS = BIAS_TAB_OFF + DSA_TK
V_ROWS = 80
NEG = -1e30
LOG2E = 1.4426950408889634
BISECT_STEPS = 20


def _sigmoid(x):
    return 1.0 / (1.0 + jnp.exp(-x))


def _rms(x, g):
    return x * lax.rsqrt(jnp.mean(x * x, axis=-1, keepdims=True) + RMS_EPS) * g


def _t5_bucket_np(n):
    n = np.maximum(np.asarray(n, np.int64), 0)
    max_exact = REL_BUCKETS // 2
    large = max_exact + (np.log(np.maximum(n, 1).astype(np.float32) / np.float32(max_exact))
                         / np.float32(math.log(REL_MAX_DIST / max_exact))
                         * np.float32(REL_BUCKETS - max_exact)).astype(np.int32)
    large = np.minimum(large, REL_BUCKETS - 1)
    return np.where(n < max_exact, n, large).astype(np.int32)


def _inproj_wide_body(x_ref, g_ref, w_ref, za_ref):
    h = _rms(x_ref[...], g_ref[...]).astype(BF16)
    for j in range(ZA_TILES):
        cols = slice(j * PROJ_TILE, (j + 1) * PROJ_TILE)
        za_ref[:, cols] = jnp.dot(h, w_ref[:, cols], preferred_element_type=F32)


def _idx_key_norm(misc, ikg_ref, ikb_ref):
    ki = misc[:, 0:IDX_DIM]
    mu = jnp.mean(ki, axis=-1, keepdims=True)
    var = jnp.mean(jnp.square(ki - mu), axis=-1, keepdims=True)
    return (ki - mu) * lax.rsqrt(var + LN_EPS) * ikg_ref[...] + ikb_ref[...]


def _inproj_tail_body(x_ref, g_ref, w_ref, ikg_ref, ikb_ref, kd_ref, vd_ref, misc_ref, kin_ref):
    h = _rms(x_ref[...], g_ref[...]).astype(BF16)
    tile = lambda j: jnp.dot(h, w_ref[:, j * PROJ_TILE:(j + 1) * PROJ_TILE], preferred_element_type=F32)
    kd_ref[...] = tile(0)
    vd_ref[...] = tile(1)
    res = tile(2)
    misc_ref[...] = res
    kin_ref[...] = _idx_key_norm(res, ikg_ref, ikb_ref)


def _inproj_wide_prompt_body(x_ref, g_ref, w_ref, wqt_ref, za_ref, qdt_ref):
    h = _rms(x_ref[...], g_ref[...]).astype(BF16)
    for j in range(ZA_TILES - 1):
        cols = slice(j * PROJ_TILE, (j + 1) * PROJ_TILE)
        za_ref[:, cols] = jnp.dot(h, w_ref[:, cols], preferred_element_type=F32)
    qdt = lax.dot_general(wqt_ref[...], h, _NT, preferred_element_type=F32)
    qdt_ref[...] = (qdt * ((DSA_HEAD_DIM ** -0.5) * LOG2E)).astype(BF16)


def _inproj_tail_prompt_body(x_ref, g_ref, w_ref, wkt_ref, wvt_ref, ikg_ref, ikb_ref,
                             kdb_ref, kt_ref, vt_ref, vslab_ref, misc_ref, kin_ref):
    h = _rms(x_ref[...], g_ref[...]).astype(BF16)
    kdb_ref[...] = jnp.dot(h, w_ref[:, 0:PROJ_TILE], preferred_element_type=F32).astype(BF16)
    kt_ref[...] = lax.dot_general(wkt_ref[...], h, _NT, preferred_element_type=F32)
    vt = lax.dot_general(wvt_ref[...], h, _NT, preferred_element_type=F32)
    vt_ref[...] = vt
    pad_rows = V_ROWS - DSA_HEAD_DIM
    ones_row = jnp.where(lax.broadcasted_iota(I32, (pad_rows, DSA_TK), 0) == 0, 1.0, 0.0).astype(BF16)
    for s in range(vslab_ref.shape[0]):
        for hd in range(DSA_HEADS):
            r0 = hd * V_ROWS
            vslab_ref[s, r0:r0 + DSA_HEAD_DIM, :] = vt[hd * DSA_HEAD_DIM:(hd + 1) * DSA_HEAD_DIM,
                                                       s * DSA_TK:(s + 1) * DSA_TK].astype(BF16)
            vslab_ref[s, r0 + DSA_HEAD_DIM:r0 + V_ROWS, :] = ones_row
    res = jnp.dot(h, w_ref[:, PROJ_TILE:2 * PROJ_TILE], preferred_element_type=F32)
    misc_ref[...] = res
    kin_ref[...] = _idx_key_norm(res, ikg_ref, ikb_ref)


def _split_w_in(w_in):
    w = w_in.astype(BF16)
    pts = np.cumsum((0,) + SPLIT_SIZES)
    seg = [w[:, int(pts[i]):int(pts[i + 1])] for i in range(len(SPLIT_SIZES))]
    q_g, k_g, v_g, r_g, a_low, q_d, k_d, v_d, q_i, k_i, w_i, gate_g, gate_d = seg
    z = lambda n: jnp.zeros((w.shape[0], n), w.dtype)
    misc = jnp.concatenate([k_i, a_low, w_i, z(LANES - MISC_WI - IDX_HEADS), q_i,
                            z(PROJ_TILE - MISC_QI - IDX_HEADS * IDX_DIM)], axis=1)
    wide = jnp.concatenate([q_g, k_g, v_g, r_g, gate_g, gate_d, q_d], axis=1)
    return wide, q_d, k_d, v_d, misc


def _in_proj_specs():
    row = lambda i: (i, 0)
    const = lambda i: (0, 0)
    resident = lambda shape: pl.BlockSpec(shape, const, pipeline_mode=pl.Buffered(1))
    params = pltpu.CompilerParams(dimension_semantics=("arbitrary",), vmem_limit_bytes=VMEM_LIMIT)
    return row, const, resident, params


def _in_proj(x2d, g_mix, w_parts, idx_k_g, idx_k_b):
    wide, q_d, k_d, v_d, misc_w = w_parts
    n = x2d.shape[0]
    row, const, resident, params = _in_proj_specs()
    g2 = g_mix.reshape(1, -1)
    tm = min(n, 512)
    zA = pl.pallas_call(
        _inproj_wide_body,
        grid=(n // tm,),
        in_specs=[pl.BlockSpec((tm, D_MODEL), row), pl.BlockSpec((1, D_MODEL), const),
                  resident((D_MODEL, ZA_WIDTH))],
        out_specs=pl.BlockSpec((tm, ZA_WIDTH), row),
        out_shape=jax.ShapeDtypeStruct((n, ZA_WIDTH), F32),
        compiler_params=params,
        name="in_proj_wide",
    )(x2d, g2, wide)
    tm = min(n, 1024)
    w_tail = jnp.concatenate([k_d, v_d, misc_w], axis=1)
    kd, vd, misc, kin = pl.pallas_call(
        _inproj_tail_body,
        grid=(n // tm,),
        in_specs=[pl.BlockSpec((tm, D_MODEL), row), pl.BlockSpec((1, D_MODEL), const),
                  resident((D_MODEL, w_tail.shape[1])),
                  pl.BlockSpec((1, IDX_DIM), const), pl.BlockSpec((1, IDX_DIM), const)],
        out_specs=[pl.BlockSpec((tm, PROJ_TILE), row)] * 3 + [pl.BlockSpec((tm, IDX_DIM), row)],
        out_shape=[jax.ShapeDtypeStruct((n, PROJ_TILE), F32)] * 3 + [jax.ShapeDtypeStruct((n, IDX_DIM), F32)],
        compiler_params=params,
        name="in_proj_tail",
    )(x2d, g2, w_tail, idx_k_g.reshape(1, -1), idx_k_b.reshape(1, -1))
    return zA, kd, vd, misc, kin


def _in_proj_prompt(x2d, g_mix, w_parts, idx_k_g, idx_k_b, nb, l):
    wide, q_d, k_d, v_d, misc_w = w_parts
    n = x2d.shape[0]
    row, const, resident, params = _in_proj_specs()
    g2 = g_mix.reshape(1, -1)
    za_w = ZA_WIDTH - PROJ_TILE
    tm = 512
    per_b = l // tm
    zA, qdt = pl.pallas_call(
        _inproj_wide_prompt_body,
        grid=(n // tm,),
        in_specs=[pl.BlockSpec((tm, D_MODEL), row), pl.BlockSpec((1, D_MODEL), const),
                  resident((D_MODEL, za_w)), resident((DSA_WIDTH, D_MODEL))],
        out_specs=[pl.BlockSpec((tm, za_w), row),
                   pl.BlockSpec((None, DSA_WIDTH, tm), lambda i: (i // per_b, 0, i % per_b))],
        out_shape=[jax.ShapeDtypeStruct((n, za_w), F32), jax.ShapeDtypeStruct((nb, DSA_WIDTH, l), BF16)],
        compiler_params=params,
        name="in_proj_wide",
    )(x2d, g2, wide[:, :za_w], q_d.T)
    tm = 1024
    per_t = l // tm
    slabs = tm // DSA_TK
    tok = lambda i: (i // per_t, 0, i % per_t)
    w_tail = jnp.concatenate([k_d, misc_w], axis=1)
    kdb, kt, vt, vslab, misc, kin = pl.pallas_call(
        _inproj_tail_prompt_body,
        grid=(n // tm,),
        in_specs=[pl.BlockSpec((tm, D_MODEL), row), pl.BlockSpec((1, D_MODEL), const),
                  resident((D_MODEL, w_tail.shape[1])), resident((DSA_WIDTH, D_MODEL)),
                  resident((DSA_WIDTH, D_MODEL)),
                  pl.BlockSpec((1, IDX_DIM), const), pl.BlockSpec((1, IDX_DIM), const)],
        out_specs=[pl.BlockSpec((tm, DSA_WIDTH), row),
                   pl.BlockSpec((None, DSA_WIDTH, tm), tok),
                   pl.BlockSpec((None, DSA_WIDTH, tm), tok),
                   pl.BlockSpec((None, slabs, DSA_HEADS * V_ROWS, DSA_TK), lambda i: (i // per_t, i % per_t, 0, 0)),
                   pl.BlockSpec((tm, PROJ_TILE), row),
                   pl.BlockSpec((tm, IDX_DIM), row)],
        out_shape=[jax.ShapeDtypeStruct((n, DSA_WIDTH), BF16),
                   jax.ShapeDtypeStruct((nb, DSA_WIDTH, l), F32),
                   jax.ShapeDtypeStruct((nb, DSA_WIDTH, l), F32),
                   jax.ShapeDtypeStruct((nb, l // DSA_TK, DSA_HEADS * V_ROWS, DSA_TK), BF16),
                   jax.ShapeDtypeStruct((n, PROJ_TILE), F32),
                   jax.ShapeDtypeStruct((n, IDX_DIM), F32)],
        compiler_params=params,
        name="in_proj_tail",
    )(x2d, g2, w_tail, k_d.T, v_d.T, idx_k_g.reshape(1, -1), idx_k_b.reshape(1, -1))
    return zA, qdt, kdb, kt, vt, vslab, misc, kin


def _gla_consts(c):
    nlev = int(math.log2(c))
    t = np.arange(c)[:, None]
    s = np.arange(c)[None, :]
    mats = [(s <= t), np.ones((c, c), bool)]
    masks = []
    for l in range(nlev):
        mid = ((t >> (l + 1)) << (l + 1)) + (1 << l) - 1
        mats.append(s <= mid)
        masks.append(((t >> (l + 1)) == (s >> (l + 1))) & (((t >> l) & 1) == 1) & (((s >> l) & 1) == 0))
    masks.append(t == s)
    return (jnp.asarray(np.concatenate(mats, 0).astype(np.float32), BF16),
            jnp.asarray(np.stack(masks).astype(np.float32), F32), nlev)


_NT = (((1,), (1,)), ((), ()))
_TN = (((0,), (0,)), ((), ()))


def _gla_body(q_ref, k_ref, v_ref, misc_ref, wgu_ref, bg_ref, gn_ref, mst_ref, lmask_ref, s0_ref,
              o_ref, sout_ref, s_scr, la_scr, *, c, nc, nlev, valid_rows, bb):
    step = pl.program_id(1)
    tile = c * nc

    @pl.when(step == 0)
    def _():
        for bi in range(bb):
            for h in range(GLA_HEADS):
                s_scr[bi, h] = s0_ref[bi, h].T

    for bi in range(bb):
        x = jnp.dot(misc_ref[bi].astype(BF16), wgu_ref[...], preferred_element_type=F32) + bg_ref[...]
        log_a = (jnp.minimum(x, 0.0) - jnp.log1p(jnp.exp(-jnp.abs(x)))) * (1.0 / GLA_GATE_TAU)
        if valid_rows < tile:
            log_a = jnp.where(lax.broadcasted_iota(I32, log_a.shape, 0) < valid_rows, log_a, 0.0)
        la_scr[bi] = log_a
    scale = GLA_HEAD_DK ** -0.5

    def chunk(ci, carry):
        r0 = pl.multiple_of(ci * c, c)
        rows = pl.ds(r0, c)
        chains = [(bi, h) for bi in range(bb) for h in range(GLA_HEADS)]
        cs = []
        for bi in range(bb):
            g_all = la_scr[bi, rows, :]
            g_hi = g_all.astype(BF16)
            g_lo = (g_all - g_hi.astype(F32)).astype(BF16)
            t = jnp.dot(mst_ref[...], jnp.concatenate([g_hi, g_lo], axis=1), preferred_element_type=F32)
            cs.append(t[:, :GLA_DK] + t[:, GLA_DK:])
        ksl = lambda h: slice(h * GLA_HEAD_DK, (h + 1) * GLA_HEAD_DK)
        vsl = lambda h: slice(h * GLA_HEAD_DV, (h + 1) * GLA_HEAD_DV)
        qs, ks, bs, els, o_inter, att = [], [], [], [], [], []
        for bi, h in chains:
            q = q_ref[bi, rows, ksl(h)] * scale
            k = k_ref[bi, rows, ksl(h)]
            b = cs[bi][0:c, ksl(h)]
            qs.append(q), ks.append(k), bs.append(b), els.append(cs[bi][c:2 * c, ksl(h)])
            o_inter.append(lax.dot_general((q * jnp.exp(b)).astype(BF16), s_scr[bi, h].astype(BF16), _NT,
                                           preferred_element_type=F32))
            a = lmask_ref[nlev] * lax.dot_general(q.astype(BF16), k.astype(BF16), _NT,
                                                  preferred_element_type=F32)
            for l in range(nlev):
                e = cs[bi][(2 + l) * c:(3 + l) * c, ksl(h)]
                ql = (q * jnp.exp(jnp.minimum(b - e, 0.0))).astype(BF16)
                kl = (k * jnp.exp(jnp.minimum(e - b, 0.0))).astype(BF16)
                a = a + lmask_ref[l] * lax.dot_general(ql, kl, _NT, preferred_element_type=F32)
            att.append(a)
        vs = [v_ref[bi, rows, vsl(h)].astype(BF16) for bi, h in chains]
        outs = [o_inter[n] + jnp.dot(att[n].astype(BF16), vs[n], preferred_element_type=F32)
                for n in range(len(chains))]
        for n, (bi, h) in enumerate(chains):
            k_st = (ks[n] * jnp.exp(els[n] - bs[n])).astype(BF16)
            s_scr[bi, h] = s_scr[bi, h] * jnp.exp(els[n][0:1, :]) + lax.dot_general(
                vs[n], k_st, _TN, preferred_element_type=F32)
            o_ref[bi, rows, vsl(h)] = _rms(outs[n], gn_ref[...])
        return carry

    lax.fori_loop(0, nc, chunk, 0)

    @pl.when(step == pl.num_programs(1) - 1)
    def _():
        for bi in range(bb):
            for h in range(GLA_HEADS):
                sout_ref[bi, h] = s_scr[bi, h].T


def _gla(q_arr, k_arr, v_arr, misc_arr, cols, wgu_pad, b_gate, gla_norm_g, s0, c, tile, valid_rows, bb):
    nb, l = q_arr.shape[0], q_arr.shape[1]
    assert nb % bb == 0 and l % tile == 0
    mst, lmask, nlev = _gla_consts(c)
    const2 = lambda b, s: (0, 0)
    state_spec = pl.BlockSpec((bb, GLA_HEADS, GLA_HEAD_DK, GLA_HEAD_DV), lambda b, s: (b, 0, 0, 0))
    body = functools.partial(_gla_body, c=c, nc=tile // c, nlev=nlev, valid_rows=valid_rows, bb=bb)
    return pl.pallas_call(
        body,
        grid=(nb // bb, l // tile),
        in_specs=[pl.BlockSpec((bb, tile, GLA_DK), lambda b, s: (b, s, cols[0])),
                  pl.BlockSpec((bb, tile, GLA_DK), lambda b, s: (b, s, cols[1])),
                  pl.BlockSpec((bb, tile, GLA_DV), lambda b, s: (b, s, cols[2])),
                  pl.BlockSpec((bb, tile, LANES), lambda b, s: (b, s, 0)),
                  pl.BlockSpec((LANES, GLA_DK), const2),
                  pl.BlockSpec((1, GLA_DK), const2),
                  pl.BlockSpec((1, GLA_HEAD_DV), const2),
                  pl.BlockSpec(mst.shape, const2),
                  pl.BlockSpec(lmask.shape, lambda b, s: (0, 0, 0)),
                  state_spec],
        out_specs=[pl.BlockSpec((bb, tile, GLA_DV), lambda b, s: (b, s, 0)), state_spec],
        out_shape=[jax.ShapeDtypeStruct((nb, l, GLA_DV), F32),
                   jax.ShapeDtypeStruct((nb, GLA_HEADS, GLA_HEAD_DK, GLA_HEAD_DV), F32)],
        scratch_shapes=[pltpu.VMEM((bb, GLA_HEADS, GLA_HEAD_DV, GLA_HEAD_DK), F32),
                        pltpu.VMEM((bb, tile, GLA_DK), F32)],
        compiler_params=pltpu.CompilerParams(dimension_semantics=("arbitrary", "arbitrary"),
                                             vmem_limit_bytes=VMEM_LIMIT),
        name="gla",
    )(q_arr, k_arr, v_arr, misc_arr, wgu_pad, b_gate.reshape(1, -1), gla_norm_g.reshape(1, -1),
      mst, lmask, s0)


def _score_stats_init(w):
    inf = jnp.full((SUBLANES, w), jnp.inf, F32)
    zero = jnp.zeros((SUBLANES, w), I32)
    return -inf, inf, zero, zero


def _score_stats_update(carry, blk, finite=False):
    mx, mn, c_pos, c_nn = carry
    b3 = blk.reshape(blk.shape[0] // SUBLANES, SUBLANES, blk.shape[1])
    lows = b3 if finite else jnp.where(b3 == -jnp.inf, jnp.inf, b3)
    return (jnp.maximum(mx, jnp.max(b3, axis=0)),
            jnp.minimum(mn, jnp.min(lows, axis=0)),
            c_pos + jnp.sum((b3 > 0.0).astype(I32), axis=0),
            c_nn + jnp.sum((b3 >= 0.0).astype(I32), axis=0))


def _select_threshold(sc_ref, tri_ref, nt, tr, topk, small, lane_ok, stats=None):
    w = sc_ref.shape[1]
    inf = jnp.float32(jnp.inf)

    def over_tiles(fn, init):
        def single(i, carry):
            r0 = pl.multiple_of(i * tr, tr)
            return fn(carry, sc_ref[pl.ds(r0, tr), :], r0)

        def pair(i, carry):
            return single(2 * i + 1, single(2 * i, carry))

        return lax.fori_loop(2 * (nt // 2), nt, single, lax.fori_loop(0, nt // 2, pair, init))

    fold = lambda x: x.reshape(tr // SUBLANES, SUBLANES, w)
    zeros8 = jnp.zeros((SUBLANES, w), I32)
    pinf8 = jnp.full((SUBLANES, w), inf, F32)

    def count(pred):
        acc = over_tiles(lambda a, blk, r0: a + jnp.sum(fold(pred(blk, r0).astype(I32)), axis=0), zeros8)
        return jnp.sum(acc, axis=0, keepdims=True)

    def min_where(pred):
        acc = over_tiles(lambda a, blk, r0: jnp.minimum(
            a, jnp.min(fold(jnp.where(pred(blk, r0), blk, inf)), axis=0)), pinf8)
        return jnp.min(acc, axis=0, keepdims=True)

    if stats is None:
        stats = over_tiles(lambda carry, blk, r0: _score_stats_update(carry, blk), _score_stats_init(w))
    mx8, mn8, cp8, cn8 = stats
    c_pos = jnp.sum(cp8, axis=0, keepdims=True)
    c_nn = jnp.sum(cn8, axis=0, keepdims=True)
    hi = jnp.where(c_pos >= topk, jnp.max(mx8, axis=0, keepdims=True), 0.0)
    lo = jnp.where(c_nn >= topk, 0.0, jnp.min(mn8, axis=0, keepdims=True))

    live = lane_ok & jnp.logical_not(small)

    def bisect(_, carry):
        lo, hi = carry
        mid = 0.5 * lo + 0.5 * hi
        ge = count(lambda blk, r0: blk >= mid) >= topk
        return jnp.where(ge, mid, lo), jnp.where(ge, hi, mid)

    lo, hi = lax.fori_loop(0, BISECT_STEPS, bisect, (lo, hi))
    v0 = min_where(lambda blk, r0: blk >= lo)

    def gt_next(v):
        def f(carry, blk, r0):
            cg, nx = carry
            gt = blk > v
            return (cg + jnp.sum(fold(gt.astype(I32)), axis=0),
                    jnp.minimum(nx, jnp.min(fold(jnp.where(gt, blk, inf)), axis=0)))
        cg8, nx8 = over_tiles(f, (zeros8, pinf8))
        return jnp.sum(cg8, axis=0, keepdims=True), jnp.min(nx8, axis=0, keepdims=True)

    def peel(state):
        v, _, _ = state
        cg, nx = gt_next(v)
        move = (cg >= topk) & live
        return jnp.where(move, nx, v), cg, jnp.max(move.astype(I32))

    v, cnt_gt, _ = lax.while_loop(lambda s: s[2] > 0, peel,
                                  (v0, jnp.zeros((1, w), I32), jnp.int32(1)))
    cnt_ge = count(lambda blk, r0: blk >= v)
    excess = (cnt_ge > topk) & live

    @pl.when(jnp.max(excess.astype(I32)) > 0)
    def _():
        need = jnp.where(excess, (topk - cnt_gt).astype(F32), jnp.float32(2 ** 30))

        tb = tri_ref.shape[0]

        def drop_surplus(rows_per_step):
            def step(i, seen):
                r0 = pl.multiple_of(i * rows_per_step, rows_per_step)
                blk = sc_ref[pl.ds(r0, rows_per_step), :]
                eq = blk == v
                ones = jnp.where(eq, 1.0, 0.0).astype(BF16)
                ranks = [jnp.dot(tri_ref[...], ones[s * tb:(s + 1) * tb, :], preferred_element_type=F32)
                         for s in range(rows_per_step // tb)]
                for s in range(rows_per_step // tb):
                    rows = slice(s * tb, (s + 1) * tb)
                    rank = ranks[s] + seen
                    sc_ref[pl.ds(r0 + s * tb, tb), :] = jnp.where(
                        eq[rows], jnp.where(rank > need, -inf, blk[rows]), blk[rows])
                    seen = rank[tb - 1:tb, :]
                return seen
            return step

        seen = lax.fori_loop(0, nt // 2, drop_surplus(2 * tr), jnp.zeros((1, w), F32))
        lax.fori_loop(2 * (nt // 2), nt, drop_surplus(tr), seen)

    return jnp.where(small, -inf, v)


def _dsa_prompt_body(relb_ref, btab_ref, tri_ref, ki_ref, qi_ref, wi_ref, kd_ref, vt_ref, qd_ref, o_ref,
                     sc_scr, tbl_scr, q2_scr, acc_scr, m_scr, o_scr, lg_scr, pr_scr, *,
                     topk, far_bucket):
    b = pl.program_id(0)
    j = pl.program_id(1)
    tk = DSA_TK
    hd = DSA_HEAD_DIM

    @pl.when((b == 0) & (j == 0))
    def _():
        q2_scr[...] = jnp.zeros(q2_scr.shape, BF16)

        def build(ci, carry):
            r0 = pl.multiple_of(ci * LANES, LANES)
            bt = btab_ref[pl.ds(r0, LANES), :]
            for h in range(DSA_HEADS):
                t = jnp.zeros(bt.shape, F32)
                for bk in range(REL_BUCKETS):
                    t = jnp.where(bt == bk, (relb_ref[bk, h] - relb_ref[far_bucket, h]) * LOG2E, t)
                tbl_scr[h, pl.ds(r0, LANES), :] = t
            return carry

        lax.fori_loop(0, BIAS_TAB_ROWS // LANES, build, 0)

    nt = j // (tk // QUERY_BLOCK) + 1
    qpos = j * QUERY_BLOCK + lax.broadcasted_iota(I32, (1, LANES), 1)
    for p in range(DSA_HEADS // 2):
        q2_scr[p, 0:hd, 0:LANES] = qd_ref[2 * p * hd:(2 * p + 1) * hd, :]
        q2_scr[p, hd:2 * hd, LANES:2 * LANES] = qd_ref[(2 * p + 1) * hd:(2 * p + 2) * hd, :]

    wi = wi_ref[...]

    def score_tile(i, carry, last):
        r0 = pl.multiple_of(i * tk, tk)
        s4 = jnp.dot(ki_ref[pl.ds(r0, tk), :], qi_ref[...], preferred_element_type=F32)
        sc = jnp.zeros((tk, LANES), F32)
        for h in range(IDX_HEADS):
            sc = sc + jnp.maximum(s4[:, h * LANES:(h + 1) * LANES], 0.0) * wi[h:h + 1, :]
        if last:
            kpos = r0 + lax.broadcasted_iota(I32, (tk, LANES), 0)
            sc = jnp.where(kpos <= qpos, sc, -jnp.inf)
        sc_scr[pl.ds(r0, tk), :] = sc
        return _score_stats_update(carry, sc, finite=not last)

    stats = lax.fori_loop(0, nt - 1, lambda i, c: score_tile(i, c, False), _score_stats_init(LANES))
    stats = score_tile(nt - 1, stats, True)

    small = (qpos + 1) < topk
    kstar = _select_threshold(sc_scr, tri_ref, nt, tk, topk, small, jnp.full((1, LANES), True), stats)

    m_scr[...] = jnp.full(m_scr.shape, NEG, F32)
    acc_scr[...] = jnp.zeros(acc_scr.shape, F32)
    vrows = acc_scr.shape[0] // DSA_HEADS

    def logits_stage(i, buf, near):
        r0 = pl.multiple_of(i * tk, tk)
        blk = sc_scr[pl.ds(r0, tk), :]
        if near:
            kpos = r0 + lax.broadcasted_iota(I32, (tk, LANES), 0)
            addm = jnp.where(blk >= kstar, jnp.where(kpos <= qpos, 0.0, NEG), NEG)
            off = pl.multiple_of(i * tk - j * QUERY_BLOCK + BIAS_TAB_OFF, LANES)
        else:
            addm = jnp.where(blk >= kstar, 0.0, NEG)
        tile_max = []
        for p in range(DSA_HEADS // 2):
            lg2 = jnp.dot(kd_ref[pl.ds(r0, tk), p * LANES:(p + 1) * LANES], q2_scr[p],
                          preferred_element_type=F32)
            for hh in range(2):
                h = 2 * p + hh
                lg = lg2[:, hh * LANES:(hh + 1) * LANES] + addm
                if near:
                    lg = lg + tbl_scr[h, pl.ds(off, tk), :]
                lg_scr[buf, h] = lg
                tile_max.append(jnp.max(lg, axis=0, keepdims=True))
        return tuple(tile_max)

    def softmax_pv_stage(i, buf, tile_max):
        alpha = []
        for h in range(DSA_HEADS):
            m_old = m_scr[h:h + 1, :]
            m_new = jnp.maximum(m_old, tile_max[h])
            m_scr[h:h + 1, :] = m_new
            alpha.append(jnp.exp2(m_old - m_new))
            pr_scr[h] = jnp.exp2((lg_scr[buf, h] - m_new).astype(BF16))
        for h in range(DSA_HEADS):
            rows = slice(h * vrows, (h + 1) * vrows)
            acc_scr[rows, :] = alpha[h] * acc_scr[rows, :] + jnp.dot(
                vt_ref[i, rows, :], pr_scr[h], preferred_element_type=F32)

    near_tiles = jnp.where(j % (tk // QUERY_BLOCK) == 0, 2, 1)
    n_far = jnp.maximum(nt - near_tiles, 0)

    @pl.when(n_far > 0)
    def _():
        n_pairs = (n_far - 1) // 2

        def pair(it, tile_max):
            i0 = 2 * it
            max1 = logits_stage(i0 + 1, 1, False)
            softmax_pv_stage(i0, 0, tile_max)
            max2 = logits_stage(i0 + 2, 0, False)
            softmax_pv_stage(i0 + 1, 1, max1)
            return max2

        tile_max = lax.fori_loop(0, n_pairs, pair, logits_stage(0, 0, False))
        last = 2 * n_pairs

        @pl.when(n_far - last == 1)
        def _():
            softmax_pv_stage(last, 0, tile_max)

        @pl.when(n_far - last == 2)
        def _():
            max1 = logits_stage(last + 1, 1, False)
            softmax_pv_stage(last, 0, tile_max)
            softmax_pv_stage(last + 1, 1, max1)

    def near_tile(i, carry):
        softmax_pv_stage(i, 0, logits_stage(i, 0, True))
        return carry

    lax.fori_loop(n_far, nt, near_tile, 0)

    for h in range(DSA_HEADS):
        o_scr[h * hd:(h + 1) * hd, :] = (acc_scr[h * vrows:h * vrows + hd, :]
                                         / acc_scr[h * vrows + hd:h * vrows + hd + 1, :])
    o_ref[...] = o_scr[...].T.astype(BF16)


def _tri_ones(n):
    return jnp.asarray(np.tril(np.ones((n, n), np.float32)), BF16)


def _bias_bucket_table():
    u = np.arange(BIAS_TAB_ROWS)[:, None]
    r = np.arange(LANES)[None, :]
    return jnp.asarray(_t5_bucket_np(r + BIAS_TAB_OFF - u), I32)


def _dsa_prompt(qd, kd_bf, vt, misc, kin, rel_bias, nb, l):
    nq = l // QUERY_BLOCK
    topk = min(TOPK_MAX, l // 4)
    assert l % DSA_TK == 0
    far = _t5_bucket_np(np.arange(REL_MAX_DIST, max(l, REL_MAX_DIST + 1)))
    assert (far == far[0]).all()
    ki = kin.reshape(nb, l, IDX_DIM).astype(BF16)
    qi = misc[:, MISC_QI:MISC_QI + IDX_HEADS * IDX_DIM].reshape(nb, nq, QUERY_BLOCK, IDX_HEADS, IDX_DIM)
    qi = qi.transpose(0, 4, 1, 3, 2).reshape(nb, IDX_DIM, nq * IDX_HEADS * QUERY_BLOCK).astype(BF16)
    wi = misc[:, MISC_WI:MISC_WI + IDX_HEADS] * ((IDX_DIM ** -0.5) * (IDX_HEADS ** -0.5))
    wi = wi.reshape(nb, nq, QUERY_BLOCK, IDX_HEADS).transpose(0, 1, 3, 2)
    wi = jnp.pad(wi, ((0, 0), (0, 0), (0, SUBLANES - IDX_HEADS), (0, 0))).reshape(nb, nq * SUBLANES, QUERY_BLOCK)
    kd_bf = kd_bf.reshape(nb, l, DSA_WIDTH)
    body = functools.partial(_dsa_prompt_body, topk=topk, far_bucket=int(far[0]))
    whole = lambda b, j: (b, 0, 0)
    return pl.pallas_call(
        body,
        grid=(nb, nq),
        in_specs=[pl.BlockSpec(memory_space=pltpu.SMEM),
                  pl.BlockSpec((BIAS_TAB_ROWS, LANES), lambda b, j: (0, 0), pipeline_mode=pl.Buffered(1)),
                  pl.BlockSpec((LANES, LANES), lambda b, j: (0, 0), pipeline_mode=pl.Buffered(1)),
                  pl.BlockSpec((None, l, IDX_DIM), whole, pipeline_mode=pl.Buffered(1)),
                  pl.BlockSpec((None, IDX_DIM, IDX_HEADS * QUERY_BLOCK), lambda b, j: (b, 0, j)),
                  pl.BlockSpec((None, SUBLANES, QUERY_BLOCK), lambda b, j: (b, j, 0)),
                  pl.BlockSpec((None, l, DSA_WIDTH), whole, pipeline_mode=pl.Buffered(1)),
                  pl.BlockSpec((None, l // DSA_TK, DSA_HEADS * V_ROWS, DSA_TK), lambda b, j: (b, 0, 0, 0),
                               pipeline_mode=pl.Buffered(1)),
                  pl.BlockSpec((None, DSA_WIDTH, QUERY_BLOCK), lambda b, j: (b, 0, j))],
        out_specs=pl.BlockSpec((None, QUERY_BLOCK, DSA_WIDTH), lambda b, j: (b, j, 0)),
        out_shape=jax.ShapeDtypeStruct((nb, l, DSA_WIDTH), BF16),
        scratch_shapes=[pltpu.VMEM((l, LANES), F32),
                        pltpu.VMEM((DSA_HEADS, BIAS_TAB_ROWS, LANES), F32),
                        pltpu.VMEM((DSA_HEADS // 2, LANES, 2 * LANES), BF16),
                        pltpu.VMEM((DSA_HEADS * V_ROWS, LANES), F32),
                        pltpu.VMEM((DSA_HEADS, LANES), F32),
                        pltpu.VMEM((DSA_WIDTH, LANES), F32),
                        pltpu.VMEM((2, DSA_HEADS, DSA_TK, LANES), F32),
                        pltpu.VMEM((DSA_HEADS, DSA_TK, LANES), BF16)],
        compiler_params=pltpu.CompilerParams(dimension_semantics=("arbitrary", "arbitrary"),
                                             vmem_limit_bytes=VMEM_LIMIT),
        name="dsa_prompt",
    )(rel_bias, _bias_bucket_table(), _tri_ones(LANES), ki, qi, wi, kd_bf, vt, qd)


def _select_threshold_lanes(sc_ref, triu_ref, nt, topk, small, row_ok):
    r = sc_ref.shape[0]
    inf = jnp.float32(jnp.inf)
    tiles = [slice(i * LANES, (i + 1) * LANES) for i in range(nt)]
    rowsum = lambda x: jnp.sum(x, axis=1, keepdims=True)
    rowmin = lambda x: jnp.min(x, axis=1, keepdims=True)

    def count(pred):
        acc = jnp.zeros((r, LANES), I32)
        for t in tiles:
            acc = acc + pred(sc_ref[:, t]).astype(I32)
        return rowsum(acc)

    def min_where(pred):
        acc = jnp.full((r, LANES), inf, F32)
        for t in tiles:
            blk = sc_ref[:, t]
            acc = jnp.minimum(acc, jnp.where(pred(blk), blk, inf))
        return rowmin(acc)

    mx = jnp.full((r, LANES), -inf, F32)
    mn = jnp.full((r, LANES), inf, F32)
    c_pos = jnp.zeros((r, LANES), I32)
    c_nn = jnp.zeros((r, LANES), I32)
    for t in tiles:
        blk = sc_ref[:, t]
        mx = jnp.maximum(mx, blk)
        mn = jnp.minimum(mn, jnp.where(blk == -inf, inf, blk))
        c_pos = c_pos + (blk > 0.0).astype(I32)
        c_nn = c_nn + (blk >= 0.0).astype(I32)
    hi = jnp.where(rowsum(c_pos) >= topk, jnp.max(mx, axis=1, keepdims=True), 0.0)
    lo = jnp.where(rowsum(c_nn) >= topk, 0.0, rowmin(mn))
    live = row_ok & jnp.logical_not(small)

    def bisect(_, carry):
        lo, hi = carry
        mid = 0.5 * lo + 0.5 * hi
        ge = count(lambda blk: blk >= mid) >= topk
        return jnp.where(ge, mid, lo), jnp.where(ge, hi, mid)

    lo, hi = lax.fori_loop(0, BISECT_STEPS, bisect, (lo, hi))
    v0 = min_where(lambda blk: blk >= lo)

    def peel(state):
        v, _, _ = state
        cg = jnp.zeros((r, LANES), I32)
        nx = jnp.full((r, LANES), inf, F32)
        for t in tiles:
            blk = sc_ref[:, t]
            gt = blk > v
            cg = cg + gt.astype(I32)
            nx = jnp.minimum(nx, jnp.where(gt, blk, inf))
        cg = rowsum(cg)
        move = (cg >= topk) & live
        return jnp.where(move, rowmin(nx), v), cg, jnp.max(move.astype(I32))

    v, cnt_gt, _ = lax.while_loop(lambda s: s[2] > 0, peel,
                                  (v0, jnp.zeros((r, 1), I32), jnp.int32(1)))
    excess = (count(lambda blk: blk >= v) > topk) & live

    @pl.when(jnp.max(excess.astype(I32)) > 0)
    def _():
        need = (topk - cnt_gt).astype(F32)
        seen = jnp.zeros((r, 1), F32)
        for t in tiles:
            blk = sc_ref[:, t]
            eq = blk == v
            rank = seen + jnp.dot(jnp.where(eq, 1.0, 0.0).astype(BF16), triu_ref[...],
                                  preferred_element_type=F32)
            sc_ref[:, t] = jnp.where(eq & (rank > need) & excess, -inf, blk)
            seen = rank[:, LANES - 1:LANES]

    return jnp.where(small, -inf, v)


SEQ_GROUP = LANES // SUBLANES
NEW_ROWS = 16


def _dsa_sample_select_body(pt_ref, *refs, n_pages, t_new, topk):
    page_refs = refs[:n_pages]
    knew_ref, qi_ref, wi_ref, triu_ref, mask_ref, sc_scr = refs[n_pages:]
    g = pl.program_id(1)
    past = n_pages * PAGE_SIZE
    rows = pl.ds(pl.multiple_of(g * SUBLANES, SUBLANES), SUBLANES)
    for p in range(n_pages + 1):
        keys_t = (page_refs[p][...] if p < n_pages else knew_ref[...]).astype(BF16)
        s = jnp.dot(qi_ref[...], keys_t, preferred_element_type=F32)
        sc = jnp.zeros((SUBLANES, PAGE_SIZE), F32)
        for h in range(IDX_HEADS):
            hs = slice(h * SUBLANES, (h + 1) * SUBLANES)
            sc = sc + jnp.maximum(s[hs, :], 0.0) * wi_ref[hs, :]
        sc_scr[rows, p * PAGE_SIZE:(p + 1) * PAGE_SIZE] = sc

    @pl.when(g == SEQ_GROUP - 1)
    def _():
        q_of_row = lax.broadcasted_iota(I32, (LANES, 1), 0) % SUBLANES
        new = slice(past, past + PAGE_SIZE)
        cpos = lax.broadcasted_iota(I32, (LANES, PAGE_SIZE), 1)
        sc_scr[:, new] = jnp.where(cpos <= q_of_row, sc_scr[:, new], -jnp.inf)
        row_ok = q_of_row < t_new
        small = (past + q_of_row + 1) < topk
        kstar = _select_threshold_lanes(sc_scr, triu_ref, n_pages + 1, topk, small, row_ok)
        for p in range(n_pages + 1):
            t = slice(p * PAGE_SIZE, (p + 1) * PAGE_SIZE)
            mask_ref[:, t] = jnp.where(sc_scr[:, t] >= kstar, 1.0, 0.0)


def _dsa_sample_select(page_table, kidx_t, kin_new_t, qi_rows, wi_rows, t_new, topk):
    db, n_pages = page_table.shape
    ng = db // SEQ_GROUP
    width = (n_pages + 1) * PAGE_SIZE
    seq = lambda gi, g, pt: (gi * SEQ_GROUP + g, 0, 0)
    page_specs = [pl.BlockSpec((None, IDX_DIM, PAGE_SIZE),
                               functools.partial(lambda gi, g, pt, p: (pt[gi * SEQ_GROUP + g, p], 0, 0), p=p))
                  for p in range(n_pages)]
    body = functools.partial(_dsa_sample_select_body, n_pages=n_pages, t_new=t_new, topk=topk)
    grid_spec = pltpu.PrefetchScalarGridSpec(
        num_scalar_prefetch=1,
        grid=(ng, SEQ_GROUP),
        in_specs=page_specs + [pl.BlockSpec((None, IDX_DIM, PAGE_SIZE), seq),
                               pl.BlockSpec((None, IDX_HEADS * SUBLANES, IDX_DIM), seq),
                               pl.BlockSpec((None, IDX_HEADS * SUBLANES, PAGE_SIZE), seq),
                               pl.BlockSpec((LANES, LANES), lambda gi, g, pt: (0, 0))],
        out_specs=pl.BlockSpec((None, LANES, width), lambda gi, g, pt: (gi, 0, 0)),
        scratch_shapes=[pltpu.VMEM((LANES, width), F32)],
    )
    triu = jnp.asarray(np.triu(np.ones((LANES, LANES), np.float32)), BF16)
    return pl.pallas_call(
        body, grid_spec=grid_spec,
        out_shape=jax.ShapeDtypeStruct((ng, LANES, width), F32),
        compiler_params=pltpu.CompilerParams(dimension_semantics=("arbitrary", "arbitrary"),
                                             vmem_limit_bytes=VMEM_LIMIT),
        name="dsa_sample_select",
    )(page_table, *([kidx_t] * n_pages), kin_new_t, qi_rows, wi_rows, triu)


def _dsa_sample_attend_body(pt_ref, *refs, n_pages):
    k_refs = refs[:n_pages]
    v_refs = refs[n_pages:2 * n_pages]
    (knew_ref, vnew_ref, mask_ref, qbd_ref, btab_ref, rb_ref, o_ref, lg_scr, tbl_scr) = refs[2 * n_pages:]
    b = pl.program_id(0)
    hq = DSA_HEADS * SUBLANES

    @pl.when(b == 0)
    def _():
        for half in range(2):
            cols = slice(half * PAGE_SIZE, (half + 1) * PAGE_SIZE)
            bt = btab_ref[:, cols]
            t = jnp.zeros(bt.shape, F32)
            for bk in range(REL_BUCKETS):
                t = jnp.where(bt == bk, rb_ref[bk], t)
            tbl_scr[:, cols] = t

    far_bias = rb_ref[REL_BUCKETS - 1]
    page = lambda ref: ref[...].reshape(DSA_WIDTH, PAGE_SIZE).astype(BF16)

    def masked(lg, sel):
        return lg + jnp.where(jnp.concatenate([sel] * DSA_HEADS, axis=0) > 0.5, 0.0, NEG)

    m = jnp.full((hq, LANES), NEG, F32)
    for p in range(n_pages):
        cols = slice(p * PAGE_SIZE, (p + 1) * PAGE_SIZE)
        lg = jnp.dot(qbd_ref[...], page(k_refs[p]), preferred_element_type=F32)
        lg = lg + (tbl_scr[:, 0:PAGE_SIZE] if p == n_pages - 1 else far_bias)
        lg = masked(lg, mask_ref[:, cols])
        lg_scr[:, cols] = lg
        m = jnp.maximum(m, lg)
    past = n_pages * PAGE_SIZE
    lg_new = lax.dot_general(qbd_ref[...], knew_ref[...].astype(BF16), _NT, preferred_element_type=F32)
    lg_new = masked(lg_new + tbl_scr[:, PAGE_SIZE:PAGE_SIZE + NEW_ROWS], mask_ref[:, past:past + NEW_ROWS])
    m = jnp.maximum(jnp.max(m, axis=1, keepdims=True), jnp.max(lg_new, axis=1, keepdims=True))

    pr_new = jnp.exp(lg_new - m)
    acc = jnp.dot(pr_new.astype(BF16), vnew_ref[...].astype(BF16), preferred_element_type=F32)
    lsum = jnp.zeros((hq, LANES), F32)
    for p in range(n_pages):
        cols = slice(p * PAGE_SIZE, (p + 1) * PAGE_SIZE)
        pr = jnp.exp(lg_scr[:, cols] - m)
        lsum = lsum + pr
        acc = acc + lax.dot_general(pr.astype(BF16), page(v_refs[p]), _NT, preferred_element_type=F32)
    lsum = jnp.sum(lsum, axis=1, keepdims=True) + jnp.sum(pr_new, axis=1, keepdims=True)
    acc = acc / lsum
    head_of_lane = lax.broadcasted_iota(I32, (SUBLANES, DSA_WIDTH), 1) // DSA_HEAD_DIM
    out = jnp.zeros((SUBLANES, DSA_WIDTH), F32)
    for h in range(DSA_HEADS):
        out = jnp.where(head_of_lane == h, acc[h * SUBLANES:(h + 1) * SUBLANES, :], out)
    o_ref[...] = out


def _dsa_sample_attend(page_table, k_t, v_t, k_new, v_new, mask, qbd, rel_bias):
    db, n_pages = page_table.shape
    hq = DSA_HEADS * SUBLANES
    width = (n_pages + 1) * PAGE_SIZE
    seq = lambda b, pt: (b, 0, 0)
    page = lambda p: functools.partial(lambda b, pt, p: (pt[b, p], 0, 0, 0), p=p)
    kv_spec = lambda p: pl.BlockSpec((None, DSA_HEADS, DSA_HEAD_DIM, PAGE_SIZE), page(p))
    u = np.arange(2 * PAGE_SIZE)[None, :]
    q = (np.arange(hq) % SUBLANES)[:, None]
    btab = jnp.asarray(_t5_bucket_np(PAGE_SIZE + q - u), I32)
    rb = jnp.broadcast_to(jnp.repeat(rel_bias, SUBLANES, axis=1)[:, :, None], (REL_BUCKETS, hq, PAGE_SIZE))
    grid_spec = pltpu.PrefetchScalarGridSpec(
        num_scalar_prefetch=1,
        grid=(db,),
        in_specs=[kv_spec(p) for p in range(n_pages)] + [kv_spec(p) for p in range(n_pages)] + [
            pl.BlockSpec((None, NEW_ROWS, DSA_WIDTH), seq),
            pl.BlockSpec((None, NEW_ROWS, DSA_WIDTH), seq),
            pl.BlockSpec((None, SUBLANES, width), lambda b, pt: (b // SEQ_GROUP, b % SEQ_GROUP, 0)),
            pl.BlockSpec((None, hq, DSA_WIDTH), seq),
            pl.BlockSpec(btab.shape, lambda b, pt: (0, 0)),
            pl.BlockSpec(rb.shape, lambda b, pt: (0, 0, 0))],
        out_specs=pl.BlockSpec((None, SUBLANES, DSA_WIDTH), seq),
        scratch_shapes=[pltpu.VMEM((hq, width - PAGE_SIZE), F32), pltpu.VMEM((hq, 2 * PAGE_SIZE), F32)],
    )
    return pl.pallas_call(
        functools.partial(_dsa_sample_attend_body, n_pages=n_pages), grid_spec=grid_spec,
        out_shape=jax.ShapeDtypeStruct((db, SUBLANES, DSA_WIDTH), F32),
        compiler_params=pltpu.CompilerParams(dimension_semantics=("arbitrary",),
                                             vmem_limit_bytes=VMEM_LIMIT),
        name="dsa_sample_attend",
    )(page_table, *([k_t] * n_pages), *([v_t] * n_pages), k_new, v_new, mask, qbd, btab, rb)


def _dsa_sample(zA, kd, vd, misc, kin, rel_bias, cache_k, cache_v, cache_kidx, page_table, db, t_new):
    n_pages = page_table.shape[1]
    past = n_pages * PAGE_SIZE
    topk = min(TOPK_MAX, (past + t_new) // 4)
    assert db % SEQ_GROUP == 0 and t_new <= SUBLANES
    new_t = lambda a, w: jnp.pad(a.reshape(db, t_new, w).transpose(0, 2, 1),
                                 ((0, 0), (0, 0), (0, PAGE_SIZE - t_new)))
    pad_q = lambda a: jnp.pad(a, ((0, 0), (0, 0), (0, SUBLANES - t_new), (0, 0)))
    new_rows = lambda a: jnp.pad(a.reshape(db, t_new, DSA_WIDTH), ((0, 0), (0, NEW_ROWS - t_new), (0, 0)))
    qi = misc[:, MISC_QI:MISC_QI + IDX_HEADS * IDX_DIM].reshape(db, t_new, IDX_HEADS, IDX_DIM)
    qi_rows = pad_q(qi.transpose(0, 2, 1, 3)).reshape(db, IDX_HEADS * SUBLANES, IDX_DIM).astype(BF16)
    wi = misc[:, MISC_WI:MISC_WI + IDX_HEADS] * ((IDX_DIM ** -0.5) * (IDX_HEADS ** -0.5))
    wi = pad_q(wi.reshape(db, t_new, IDX_HEADS, 1).transpose(0, 2, 1, 3)).reshape(db, IDX_HEADS * SUBLANES, 1)
    wi_rows = jnp.broadcast_to(wi, (db, IDX_HEADS * SUBLANES, PAGE_SIZE))
    mask = _dsa_sample_select(page_table, cache_kidx.transpose(0, 2, 1), new_t(kin, IDX_DIM),
                              qi_rows, wi_rows, t_new, topk)
    qd = (zA[:, ZA_QD:ZA_QD + DSA_WIDTH] * (DSA_HEAD_DIM ** -0.5)).reshape(db, t_new, DSA_HEADS, DSA_HEAD_DIM)
    qd = pad_q(qd.transpose(0, 2, 1, 3))
    eye = jnp.eye(DSA_HEADS, dtype=qd.dtype)
    qbd = (qd[:, :, :, None, :] * eye[None, :, None, :, None]).reshape(db, DSA_HEADS * SUBLANES, DSA_WIDTH)
    o = _dsa_sample_attend(page_table, cache_k.transpose(0, 2, 3, 1), cache_v.transpose(0, 2, 3, 1),
                           new_rows(kd), new_rows(vd), mask, qbd.astype(BF16), rel_bias)
    return o[:, :t_new, :].reshape(db * t_new, DSA_WIDTH).astype(BF16)


def _post_body(x_ref, og_ref, rg_ref, gg_ref, gd_ref, od_ref, wg_ref, wd_ref, wo_ref, gf_ref,
               x1_ref, hf_ref):
    half = x_ref.shape[0] // 2
    parts = [slice(0, half), slice(half, 2 * half)]
    branch = []
    for r in parts:
        rg = rg_ref[r, :]
        a = (og_ref[r, :] * (rg * _sigmoid(rg))).astype(BF16)
        branch.append((jnp.dot(a, wg_ref[...], preferred_element_type=F32),
                       jnp.dot(od_ref[r, :], wd_ref[...], preferred_element_type=F32)))
    for r, (y_g, y_d) in zip(parts, branch):
        mix = (_sigmoid(gg_ref[r, :]) * y_g + _sigmoid(gd_ref[r, :]) * y_d).astype(BF16)
        x1 = x_ref[r, :] + jnp.dot(mix, wo_ref[...], preferred_element_type=F32)
        x1_ref[r, :] = x1
        hf_ref[r, :] = _rms(x1, gf_ref[...]).astype(BF16)


def _post(x2d, og, zA, od, w_gla, w_dsa, w_o, g_ffn):
    n = x2d.shape[0]
    tm = min(n, 512)
    row = lambda i: (i, 0)
    const = lambda i: (0, 0)
    return pl.pallas_call(
        _post_body,
        grid=(n // tm,),
        in_specs=[pl.BlockSpec((tm, D_MODEL), row),
                  pl.BlockSpec((tm, GLA_DV), row),
                  pl.BlockSpec((tm, GLA_DV), lambda i: (i, COL_RG)),
                  pl.BlockSpec((tm, D_MODEL), lambda i: (i, COL_GG)),
                  pl.BlockSpec((tm, D_MODEL), lambda i: (i, COL_GD)),
                  pl.BlockSpec((tm, DSA_WIDTH), row),
                  pl.BlockSpec((GLA_DV, D_MODEL), const),
                  pl.BlockSpec((DSA_WIDTH, D_MODEL), const),
                  pl.BlockSpec((D_MODEL, D_MODEL), const),
                  pl.BlockSpec((1, D_MODEL), const)],
        out_specs=[pl.BlockSpec((tm, D_MODEL), row), pl.BlockSpec((tm, D_MODEL), row)],
        out_shape=[jax.ShapeDtypeStruct((n, D_MODEL), F32), jax.ShapeDtypeStruct((n, D_MODEL), BF16)],
        compiler_params=pltpu.CompilerParams(dimension_semantics=("arbitrary",),
                                             vmem_limit_bytes=VMEM_LIMIT),
        name="post_mix",
    )(x2d, og, zA, zA, zA, od, w_gla, w_dsa, w_o, g_ffn.reshape(1, -1))


FFN_TILE = 256
FFN_ROWS = 512


def _ffn_body(hf_ref, x1_ref, wg_ref, wu_ref, wd_ref, gfin_ref, y_ref, acc_scr):
    hf = hf_ref[...]
    n_chunks = wg_ref.shape[1] // FFN_TILE
    cols = lambda k: slice(k * FFN_TILE, (k + 1) * FFN_TILE)

    def gate_up(k):
        return (jnp.dot(hf, wg_ref[:, cols(k)], preferred_element_type=F32),
                jnp.dot(hf, wu_ref[:, cols(k)], preferred_element_type=F32))

    acc_scr[...] = x1_ref[...]
    nxt = gate_up(0)
    for k in range(n_chunks):
        gate, up = nxt
        if k + 1 < n_chunks:
            nxt = gate_up(k + 1)
        act = (gate * _sigmoid(gate) * up).astype(BF16)
        acc_scr[...] = acc_scr[...] + jnp.dot(act, wd_ref[cols(k), :], preferred_element_type=F32)
    y_ref[...] = _rms(acc_scr[...], gfin_ref[...])


def _ffn(hf, x1, w_gate, w_up, w_down, g_final):
    n = hf.shape[0]
    d_ff = w_gate.shape[1]
    assert d_ff % FFN_TILE == 0
    tm = min(n, FFN_ROWS)
    row = lambda i: (i, 0)
    const = lambda i: (0, 0)
    resident = lambda shape: pl.BlockSpec(shape, const, pipeline_mode=pl.Buffered(1))
    return pl.pallas_call(
        _ffn_body,
        grid=(n // tm,),
        in_specs=[pl.BlockSpec((tm, D_MODEL), row),
                  pl.BlockSpec((tm, D_MODEL), row),
                  resident((D_MODEL, d_ff)),
                  resident((D_MODEL, d_ff)),
                  resident((d_ff, D_MODEL)),
                  pl.BlockSpec((1, D_MODEL), const)],
        out_specs=pl.BlockSpec((tm, D_MODEL), row),
        out_shape=jax.ShapeDtypeStruct((n, D_MODEL), F32),
        scratch_shapes=[pltpu.VMEM((tm, D_MODEL), F32)],
        compiler_params=pltpu.CompilerParams(dimension_semantics=("arbitrary",),
                                             vmem_limit_bytes=VMEM_LIMIT),
        name="ffn",
    )(hf, x1, w_gate, w_up, w_down, g_final.reshape(1, -1))


def _gate_weight_pad(w_gate_up):
    pad = jnp.zeros((LANES, GLA_DK), w_gate_up.dtype)
    return pad.at[MISC_ALOW:MISC_ALOW + GLA_GATE_RANK].set(w_gate_up).astype(BF16)


def kernel(x_prompt, x_sample, cache_k, cache_v, cache_kidx, state_gla, page_table, g_mix, w_in,
           w_gate_up, b_gate, gla_norm_g, w_gla_branch, idx_k_g, idx_k_b, w_dsa_branch, w_o, g_ffn,
           w_ffn_gate, w_ffn_up, w_ffn_down, rel_bias, g_final):
    depth = w_in.shape[0]
    assert depth == 1, "the final RMSNorm is fused into the FFN kernel of the single layer"
    nb, l, _ = x_prompt.shape
    db, t_new, _ = x_sample.shape
    layer = 0
    w_parts = _split_w_in(w_in[layer])
    wgu_pad = _gate_weight_pad(w_gate_up[layer])
    w_gla = w_gla_branch[layer].astype(BF16)
    w_dsa = w_dsa_branch[layer].astype(BF16)
    w_out = w_o[layer].astype(BF16)
    w_fg, w_fu, w_fd = (w.astype(BF16) for w in (w_ffn_gate[layer], w_ffn_up[layer], w_ffn_down[layer]))

    xp = x_prompt.reshape(nb * l, D_MODEL)
    zA, qdt, kdb, kt, vt, vslab, misc, kin = _in_proj_prompt(xp, g_mix[layer], w_parts, idx_k_g[layer],
                                                             idx_k_b[layer], nb, l)
    zA3 = zA.reshape(nb, l, zA.shape[1])
    s0 = jnp.zeros((nb, GLA_HEADS, GLA_HEAD_DK, GLA_HEAD_DV), state_gla.dtype)
    og, s_p = _gla(zA3, zA3, zA3, misc.reshape(nb, l, PROJ_TILE), (COL_QG, COL_KG, COL_VG), wgu_pad,
                   b_gate[layer], gla_norm_g[layer], s0, GLA_CHUNK, GLA_TILE, GLA_TILE, nb)
    od = _dsa_prompt(qdt, kdb, vslab, misc, kin, rel_bias, nb, l)
    x1, hf = _post(xp, og.reshape(nb * l, GLA_DV), zA, od.reshape(nb * l, DSA_WIDTH), w_gla, w_dsa,
                   w_out, g_ffn[layer])
    y_p = _ffn(hf, x1, w_fg, w_fu, w_fd, g_final).reshape(nb, l, D_MODEL)

    xs = x_sample.reshape(db * t_new, D_MODEL)
    zA_s, kd_s, vd_s, misc_s, kin_s = _in_proj(xs, g_mix[layer], w_parts, idx_k_g[layer], idx_k_b[layer])
    pad_t = lambda a: jnp.pad(a.reshape(db, t_new, -1), ((0, 0), (0, SAMPLE_CHUNK - t_new), (0, 0)))
    og_s, s_s = _gla(pad_t(zA_s[:, 0:GLA_DK]), pad_t(zA_s[:, GLA_DK:2 * GLA_DK]),
                     pad_t(zA_s[:, 2 * GLA_DK:2 * GLA_DK + GLA_DV]), pad_t(misc_s[:, 0:LANES]), (0, 0, 0),
                     wgu_pad, b_gate[layer], gla_norm_g[layer], state_gla[layer], SAMPLE_CHUNK, SAMPLE_CHUNK, t_new, SAMPLE_GLA_SEQS)
    og_s = og_s[:, :t_new, :].reshape(db * t_new, GLA_DV)
    od_s = _dsa_sample(zA_s, kd_s, vd_s, misc_s, kin_s, rel_bias, cache_k[layer], cache_v[layer],
                       cache_kidx[layer], page_table, db, t_new)
    x1_s, hf_s = _post(xs, og_s, zA_s, od_s, w_gla, w_dsa, w_out, g_ffn[layer])
    y_s = _ffn(hf_s, x1_s, w_fg, w_fu, w_fd, g_final).reshape(db, t_new, D_MODEL)

    heads = lambda a, n, t: a.reshape(1, n, t, DSA_HEADS, DSA_HEAD_DIM)
    heads_t = lambda a: a.reshape(nb, DSA_HEADS, DSA_HEAD_DIM, l).transpose(0, 3, 1, 2)[None]
    return (y_p, y_s,
            heads_t(kt), heads_t(vt), kin.reshape(1, nb, l, IDX_DIM), s_p[None],
            heads(kd_s, db, t_new), heads(vd_s, db, t_new), kin_s.reshape(1, db, t_new, IDX_DIM), s_s[None])
```

```python
import functools
import math

import numpy as np
import jax
import jax.numpy as jnp
from jax import lax
from jax.experimental import pallas as pl
from jax.experimental.pallas import tpu as pltpu

F32, BF16, I32 = jnp.float32, jnp.bfloat16, jnp.int32

D_MODEL = 1024
GLA_HEADS = 4
GLA_HEAD_DK = 128
GLA_HEAD_DV = 256
GLA_DK = GLA_HEADS * GLA_HEAD_DK
GLA_DV = GLA_HEADS * GLA_HEAD_DV
GLA_GATE_RANK = 16
GLA_GATE_TAU = 16.0
DSA_HEADS = 8
DSA_HEAD_DIM = 64
DSA_WIDTH = DSA_HEADS * DSA_HEAD_DIM
IDX_HEADS = 4
IDX_DIM = 64
TOPK_MAX = 256
QUERY_BLOCK = 128
PAGE_SIZE = 128
REL_BUCKETS = 32
REL_MAX_DIST = 128
RMS_EPS = 1e-6
LN_EPS = 1e-6
SPLIT_SIZES = (GLA_DK, GLA_DK, GLA_DV, GLA_DV, GLA_GATE_RANK, DSA_WIDTH, DSA_WIDTH, DSA_WIDTH,
               IDX_HEADS * IDX_DIM, IDX_DIM, IDX_HEADS, D_MODEL, D_MODEL)

LANES = 128
SUBLANES = 8
VMEM_LIMIT = 56 * 1024 * 1024

PROJ_TILE = 512
ZA_TILES = 11
ZA_WIDTH = ZA_TILES * PROJ_TILE
COL_QG, COL_KG = 0, 1
COL_VG, COL_RG, COL_GG, COL_GD = 1, 2, 3, 4
ZA_QD = 10 * PROJ_TILE
MISC_ALOW, MISC_WI, MISC_QI = 64, 80, 128

GLA_CHUNK = 64
GLA_TILE = 256
SAMPLE_CHUNK = 16
SAMPLE_GLA_SEQS = 4
DSA_TK = 512
BIAS_TAB_OFF = 2 * DSA_TK - QUERY_BLOCK
BIAS_TAB_ROWS = BIAS_TAB_OFF + DSA_TK
V_ROWS = 80
NEG = -1e30
LOG2E = 1.4426950408889634
BISECT_STEPS = 20


def _sigmoid(x):
    return 1.0 / (1.0 + jnp.exp(-x))


def _rms(x, g):
    return x * lax.rsqrt(jnp.mean(x * x, axis=-1, keepdims=True) + RMS_EPS) * g


def _t5_bucket_np(n):
    n = np.maximum(np.asarray(n, np.int64), 0)
    max_exact = REL_BUCKETS // 2
    large = max_exact + (np.log(np.maximum(n, 1).astype(np.float32) / np.float32(max_exact))
                         / np.float32(math.log(REL_MAX_DIST / max_exact))
                         * np.float32(REL_BUCKETS - max_exact)).astype(np.int32)
    large = np.minimum(large, REL_BUCKETS - 1)
    return np.where(n < max_exact, n, large).astype(np.int32)


def _inproj_wide_body(x_ref, g_ref, w_ref, za_ref):
    h = _rms(x_ref[...], g_ref[...]).astype(BF16)
    for j in range(ZA_TILES):
        cols = slice(j * PROJ_TILE, (j + 1) * PROJ_TILE)
        za_ref[:, cols] = jnp.dot(h, w_ref[:, cols], preferred_element_type=F32)


def _idx_key_norm(misc, ikg_ref, ikb_ref):
    ki = misc[:, 0:IDX_DIM]
    mu = jnp.mean(ki, axis=-1, keepdims=True)
    var = jnp.mean(jnp.square(ki - mu), axis=-1, keepdims=True)
    return (ki - mu) * lax.rsqrt(var + LN_EPS) * ikg_ref[...] + ikb_ref[...]


def _inproj_tail_body(x_ref, g_ref, w_ref, ikg_ref, ikb_ref, kd_ref, vd_ref, misc_ref, kin_ref):
    h = _rms(x_ref[...], g_ref[...]).astype(BF16)
    tile = lambda j: jnp.dot(h, w_ref[:, j * PROJ_TILE:(j + 1) * PROJ_TILE], preferred_element_type=F32)
    kd_ref[...] = tile(0)
    vd_ref[...] = tile(1)
    res = tile(2)
    misc_ref[...] = res
    kin_ref[...] = _idx_key_norm(res, ikg_ref, ikb_ref)


def _inproj_wide_prompt_body(x_ref, g_ref, w_ref, wqt_ref, za_ref, qdt_ref):
    h = _rms(x_ref[...], g_ref[...]).astype(BF16)
    for j in range(ZA_TILES - 1):
        cols = slice(j * PROJ_TILE, (j + 1) * PROJ_TILE)
        za_ref[:, cols] = jnp.dot(h, w_ref[:, cols], preferred_element_type=F32)
    qdt = lax.dot_general(wqt_ref[...], h, _NT, preferred_element_type=F32)
    qdt_ref[...] = (qdt * ((DSA_HEAD_DIM ** -0.5) * LOG2E)).astype(BF16)


def _inproj_tail_prompt_body(x_ref, g_ref, w_ref, wkt_ref, wvt_ref, ikg_ref, ikb_ref,
                             kdb_ref, kt_ref, vt_ref, vslab_ref, misc_ref, kin_ref):
    h = _rms(x_ref[...], g_ref[...]).astype(BF16)
    kdb_ref[...] = jnp.dot(h, w_ref[:, 0:PROJ_TILE], preferred_element_type=F32).astype(BF16)
    kt_ref[...] = lax.dot_general(wkt_ref[...], h, _NT, preferred_element_type=F32)
    vt = lax.dot_general(wvt_ref[...], h, _NT, preferred_element_type=F32)
    vt_ref[...] = vt
    pad_rows = V_ROWS - DSA_HEAD_DIM
    ones_row = jnp.where(lax.broadcasted_iota(I32, (pad_rows, DSA_TK), 0) == 0, 1.0, 0.0).astype(BF16)
    for s in range(vslab_ref.shape[0]):
        for hd in range(DSA_HEADS):
            r0 = hd * V_ROWS
            vslab_ref[s, r0:r0 + DSA_HEAD_DIM, :] = vt[hd * DSA_HEAD_DIM:(hd + 1) * DSA_HEAD_DIM,
                                                       s * DSA_TK:(s + 1) * DSA_TK].astype(BF16)
            vslab_ref[s, r0 + DSA_HEAD_DIM:r0 + V_ROWS, :] = ones_row
    res = jnp.dot(h, w_ref[:, PROJ_TILE:2 * PROJ_TILE], preferred_element_type=F32)
    misc_ref[...] = res
    kin_ref[...] = _idx_key_norm(res, ikg_ref, ikb_ref)


def _split_w_in(w_in):
    w = w_in.astype(BF16)
    pts = np.cumsum((0,) + SPLIT_SIZES)
    seg = [w[:, int(pts[i]):int(pts[i + 1])] for i in range(len(SPLIT_SIZES))]
    q_g, k_g, v_g, r_g, a_low, q_d, k_d, v_d, q_i, k_i, w_i, gate_g, gate_d = seg
    z = lambda n: jnp.zeros((w.shape[0], n), w.dtype)
    misc = jnp.concatenate([k_i, a_low, w_i, z(LANES - MISC_WI - IDX_HEADS), q_i,
                            z(PROJ_TILE - MISC_QI - IDX_HEADS * IDX_DIM)], axis=1)
    wide = jnp.concatenate([q_g, k_g, v_g, r_g, gate_g, gate_d, q_d], axis=1)
    return wide, q_d, k_d, v_d, misc


def _in_proj_specs():
    row = lambda i: (i, 0)
    const = lambda i: (0, 0)
    resident = lambda shape: pl.BlockSpec(shape, const, pipeline_mode=pl.Buffered(1))
    params = pltpu.CompilerParams(dimension_semantics=("arbitrary",), vmem_limit_bytes=VMEM_LIMIT)
    return row, const, resident, params


def _in_proj(x2d, g_mix, w_parts, idx_k_g, idx_k_b):
    wide, q_d, k_d, v_d, misc_w = w_parts
    n = x2d.shape[0]
    row, const, resident, params = _in_proj_specs()
    g2 = g_mix.reshape(1, -1)
    tm = min(n, 512)
    zA = pl.pallas_call(
        _inproj_wide_body,
        grid=(n // tm,),
        in_specs=[pl.BlockSpec((tm, D_MODEL), row), pl.BlockSpec((1, D_MODEL), const),
                  resident((D_MODEL, ZA_WIDTH))],
        out_specs=pl.BlockSpec((tm, ZA_WIDTH), row),
        out_shape=jax.ShapeDtypeStruct((n, ZA_WIDTH), F32),
        compiler_params=params,
        name="in_proj_wide",
    )(x2d, g2, wide)
    tm = min(n, 1024)
    w_tail = jnp.concatenate([k_d, v_d, misc_w], axis=1)
    kd, vd, misc, kin = pl.pallas_call(
        _inproj_tail_body,
        grid=(n // tm,),
        in_specs=[pl.BlockSpec((tm, D_MODEL), row), pl.BlockSpec((1, D_MODEL), const),
                  resident((D_MODEL, w_tail.shape[1])),
                  pl.BlockSpec((1, IDX_DIM), const), pl.BlockSpec((1, IDX_DIM), const)],
        out_specs=[pl.BlockSpec((tm, PROJ_TILE), row)] * 3 + [pl.BlockSpec((tm, IDX_DIM), row)],
        out_shape=[jax.ShapeDtypeStruct((n, PROJ_TILE), F32)] * 3 + [jax.ShapeDtypeStruct((n, IDX_DIM), F32)],
        compiler_params=params,
        name="in_proj_tail",
    )(x2d, g2, w_tail, idx_k_g.reshape(1, -1), idx_k_b.reshape(1, -1))
    return zA, kd, vd, misc, kin


def _in_proj_prompt(x2d, g_mix, w_parts, idx_k_g, idx_k_b, nb, l):
    wide, q_d, k_d, v_d, misc_w = w_parts
    n = x2d.shape[0]
    row, const, resident, params = _in_proj_specs()
    g2 = g_mix.reshape(1, -1)
    za_w = ZA_WIDTH - PROJ_TILE
    tm = 512
    per_b = l // tm
    zA, qdt = pl.pallas_call(
        _inproj_wide_prompt_body,
        grid=(n // tm,),
        in_specs=[pl.BlockSpec((tm, D_MODEL), row), pl.BlockSpec((1, D_MODEL), const),
                  resident((D_MODEL, za_w)), resident((DSA_WIDTH, D_MODEL))],
        out_specs=[pl.BlockSpec((tm, za_w), row),
                   pl.BlockSpec((None, DSA_WIDTH, tm), lambda i: (i // per_b, 0, i % per_b))],
        out_shape=[jax.ShapeDtypeStruct((n, za_w), F32), jax.ShapeDtypeStruct((nb, DSA_WIDTH, l), BF16)],
        compiler_params=params,
        name="in_proj_wide",
    )(x2d, g2, wide[:, :za_w], q_d.T)
    tm = 1024
    per_t = l // tm
    slabs = tm // DSA_TK
    tok = lambda i: (i // per_t, 0, i % per_t)
    w_tail = jnp.concatenate([k_d, misc_w], axis=1)
    kdb, kt, vt, vslab, misc, kin = pl.pallas_call(
        _inproj_tail_prompt_body,
        grid=(n // tm,),
        in_specs=[pl.BlockSpec((tm, D_MODEL), row), pl.BlockSpec((1, D_MODEL), const),
                  resident((D_MODEL, w_tail.shape[1])), resident((DSA_WIDTH, D_MODEL)),
                  resident((DSA_WIDTH, D_MODEL)),
                  pl.BlockSpec((1, IDX_DIM), const), pl.BlockSpec((1, IDX_DIM), const)],
        out_specs=[pl.BlockSpec((tm, DSA_WIDTH), row),
                   pl.BlockSpec((None, DSA_WIDTH, tm), tok),
                   pl.BlockSpec((None, DSA_WIDTH, tm), tok),
                   pl.BlockSpec((None, slabs, DSA_HEADS * V_ROWS, DSA_TK), lambda i: (i // per_t, i % per_t, 0, 0)),
                   pl.BlockSpec((tm, PROJ_TILE), row),
                   pl.BlockSpec((tm, IDX_DIM), row)],
        out_shape=[jax.ShapeDtypeStruct((n, DSA_WIDTH), BF16),
                   jax.ShapeDtypeStruct((nb, DSA_WIDTH, l), F32),
                   jax.ShapeDtypeStruct((nb, DSA_WIDTH, l), F32),
                   jax.ShapeDtypeStruct((nb, l // DSA_TK, DSA_HEADS * V_ROWS, DSA_TK), BF16),
                   jax.ShapeDtypeStruct((n, PROJ_TILE), F32),
                   jax.ShapeDtypeStruct((n, IDX_DIM), F32)],
        compiler_params=params,
        name="in_proj_tail",
    )(x2d, g2, w_tail, k_d.T, v_d.T, idx_k_g.reshape(1, -1), idx_k_b.reshape(1, -1))
    return zA, qdt, kdb, kt, vt, vslab, misc, kin


def _gla_consts(c):
    nlev = int(math.log2(c))
    t = np.arange(c)[:, None]
    s = np.arange(c)[None, :]
    mats = [(s <= t), np.ones((c, c), bool)]
    masks = []
    for l in range(nlev):
        mid = ((t >> (l + 1)) << (l + 1)) + (1 << l) - 1
        mats.append(s <= mid)
        masks.append(((t >> (l + 1)) == (s >> (l + 1))) & (((t >> l) & 1) == 1) & (((s >> l) & 1) == 0))
    masks.append(t == s)
    return (jnp.asarray(np.concatenate(mats, 0).astype(np.float32), BF16),
            jnp.asarray(np.stack(masks).astype(np.float32), F32), nlev)


_NT = (((1,), (1,)), ((), ()))
_TN = (((0,), (0,)), ((), ()))


def _gla_body(q_ref, k_ref, v_ref, misc_ref, wgu_ref, bg_ref, gn_ref, mst_ref, lmask_ref, s0_ref,
              o_ref, sout_ref, s_scr, la_scr, *, c, nc, nlev, valid_rows, bb):
    step = pl.program_id(1)
    tile = c * nc

    @pl.when(step == 0)
    def _():
        for bi in range(bb):
            for h in range(GLA_HEADS):
                s_scr[bi, h] = s0_ref[bi, h].T

    for bi in range(bb):
        x = jnp.dot(misc_ref[bi].astype(BF16), wgu_ref[...], preferred_element_type=F32) + bg_ref[...]
        log_a = (jnp.minimum(x, 0.0) - jnp.log1p(jnp.exp(-jnp.abs(x)))) * (1.0 / GLA_GATE_TAU)
        if valid_rows < tile:
            log_a = jnp.where(lax.broadcasted_iota(I32, log_a.shape, 0) < valid_rows, log_a, 0.0)
        la_scr[bi] = log_a
    scale = GLA_HEAD_DK ** -0.5

    def chunk(ci, carry):
        r0 = pl.multiple_of(ci * c, c)
        rows = pl.ds(r0, c)
        chains = [(bi, h) for bi in range(bb) for h in range(GLA_HEADS)]
        cs = []
        for bi in range(bb):
            g_all = la_scr[bi, rows, :]
            g_hi = g_all.astype(BF16)
            g_lo = (g_all - g_hi.astype(F32)).astype(BF16)
            t = jnp.dot(mst_ref[...], jnp.concatenate([g_hi, g_lo], axis=1), preferred_element_type=F32)
            cs.append(t[:, :GLA_DK] + t[:, GLA_DK:])
        ksl = lambda h: slice(h * GLA_HEAD_DK, (h + 1) * GLA_HEAD_DK)
        vsl = lambda h: slice(h * GLA_HEAD_DV, (h + 1) * GLA_HEAD_DV)
        qs, ks, bs, els, o_inter, att = [], [], [], [], [], []
        for bi, h in chains:
            q = q_ref[bi, rows, ksl(h)] * scale
            k = k_ref[bi, rows, ksl(h)]
            b = cs[bi][0:c, ksl(h)]
            qs.append(q), ks.append(k), bs.append(b), els.append(cs[bi][c:2 * c, ksl(h)])
            o_inter.append(lax.dot_general((q * jnp.exp(b)).astype(BF16), s_scr[bi, h].astype(BF16), _NT,
                                           preferred_element_type=F32))
            a = lmask_ref[nlev] * lax.dot_general(q.astype(BF16), k.astype(BF16), _NT,
                                                  preferred_element_type=F32)
            for l in range(nlev):
                e = cs[bi][(2 + l) * c:(3 + l) * c, ksl(h)]
                ql = (q * jnp.exp(jnp.minimum(b - e, 0.0))).astype(BF16)
                kl = (k * jnp.exp(jnp.minimum(e - b, 0.0))).astype(BF16)
                a = a + lmask_ref[l] * lax.dot_general(ql, kl, _NT, preferred_element_type=F32)
            att.append(a)
        vs = [v_ref[bi, rows, vsl(h)].astype(BF16) for bi, h in chains]
        outs = [o_inter[n] + jnp.dot(att[n].astype(BF16), vs[n], preferred_element_type=F32)
                for n in range(len(chains))]
        for n, (bi, h) in enumerate(chains):
            k_st = (ks[n] * jnp.exp(els[n] - bs[n])).astype(BF16)
            s_scr[bi, h] = s_scr[bi, h] * jnp.exp(els[n][0:1, :]) + lax.dot_general(
                vs[n], k_st, _TN, preferred_element_type=F32)
            o_ref[bi, rows, vsl(h)] = _rms(outs[n], gn_ref[...])
        return carry

    lax.fori_loop(0, nc, chunk, 0)

    @pl.when(step == pl.num_programs(1) - 1)
    def _():
        for bi in range(bb):
            for h in range(GLA_HEADS):
                sout_ref[bi, h] = s_scr[bi, h].T


def _gla(q_arr, k_arr, v_arr, misc_arr, cols, wgu_pad, b_gate, gla_norm_g, s0, c, tile, valid_rows, bb):
    nb, l = q_arr.shape[0], q_arr.shape[1]
    assert nb % bb == 0 and l % tile == 0
    mst, lmask, nlev = _gla_consts(c)
    const2 = lambda b, s: (0, 0)
    state_spec = pl.BlockSpec((bb, GLA_HEADS, GLA_HEAD_DK, GLA_HEAD_DV), lambda b, s: (b, 0, 0, 0))
    body = functools.partial(_gla_body, c=c, nc=tile // c, nlev=nlev, valid_rows=valid_rows, bb=bb)
    return pl.pallas_call(
        body,
        grid=(nb // bb, l // tile),
        in_specs=[pl.BlockSpec((bb, tile, GLA_DK), lambda b, s: (b, s, cols[0])),
                  pl.BlockSpec((bb, tile, GLA_DK), lambda b, s: (b, s, cols[1])),
                  pl.BlockSpec((bb, tile, GLA_DV), lambda b, s: (b, s, cols[2])),
                  pl.BlockSpec((bb, tile, LANES), lambda b, s: (b, s, 0)),
                  pl.BlockSpec((LANES, GLA_DK), const2),
                  pl.BlockSpec((1, GLA_DK), const2),
                  pl.BlockSpec((1, GLA_HEAD_DV), const2),
                  pl.BlockSpec(mst.shape, const2),
                  pl.BlockSpec(lmask.shape, lambda b, s: (0, 0, 0)),
                  state_spec],
        out_specs=[pl.BlockSpec((bb, tile, GLA_DV), lambda b, s: (b, s, 0)), state_spec],
        out_shape=[jax.ShapeDtypeStruct((nb, l, GLA_DV), F32),
                   jax.ShapeDtypeStruct((nb, GLA_HEADS, GLA_HEAD_DK, GLA_HEAD_DV), F32)],
        scratch_shapes=[pltpu.VMEM((bb, GLA_HEADS, GLA_HEAD_DV, GLA_HEAD_DK), F32),
                        pltpu.VMEM((bb, tile, GLA_DK), F32)],
        compiler_params=pltpu.CompilerParams(dimension_semantics=("arbitrary", "arbitrary"),
                                             vmem_limit_bytes=VMEM_LIMIT),
        name="gla",
    )(q_arr, k_arr, v_arr, misc_arr, wgu_pad, b_gate.reshape(1, -1), gla_norm_g.reshape(1, -1),
      mst, lmask, s0)


def _score_stats_init(w):
    inf = jnp.full((SUBLANES, w), jnp.inf, F32)
    zero = jnp.zeros((SUBLANES, w), I32)
    return -inf, inf, zero, zero


def _score_stats_update(carry, blk, finite=False):
    mx, mn, c_pos, c_nn = carry
    b3 = blk.reshape(blk.shape[0] // SUBLANES, SUBLANES, blk.shape[1])
    lows = b3 if finite else jnp.where(b3 == -jnp.inf, jnp.inf, b3)
    return (jnp.maximum(mx, jnp.max(b3, axis=0)),
            jnp.minimum(mn, jnp.min(lows, axis=0)),
            c_pos + jnp.sum((b3 > 0.0).astype(I32), axis=0),
            c_nn + jnp.sum((b3 >= 0.0).astype(I32), axis=0))


def _select_threshold(sc_ref, tri_ref, nt, tr, topk, small, lane_ok, stats=None):
    w = sc_ref.shape[1]
    inf = jnp.float32(jnp.inf)

    def over_tiles(fn, init):
        def single(i, carry):
            r0 = pl.multiple_of(i * tr, tr)
            return fn(carry, sc_ref[pl.ds(r0, tr), :], r0)

        def pair(i, carry):
            return single(2 * i + 1, single(2 * i, carry))

        return lax.fori_loop(2 * (nt // 2), nt, single, lax.fori_loop(0, nt // 2, pair, init))

    fold = lambda x: x.reshape(tr // SUBLANES, SUBLANES, w)
    zeros8 = jnp.zeros((SUBLANES, w), I32)
    pinf8 = jnp.full((SUBLANES, w), inf, F32)

    def count(pred):
        acc = over_tiles(lambda a, blk, r0: a + jnp.sum(fold(pred(blk, r0).astype(I32)), axis=0), zeros8)
        return jnp.sum(acc, axis=0, keepdims=True)

    def min_where(pred):
        acc = over_tiles(lambda a, blk, r0: jnp.minimum(
            a, jnp.min(fold(jnp.where(pred(blk, r0), blk, inf)), axis=0)), pinf8)
        return jnp.min(acc, axis=0, keepdims=True)

    if stats is None:
        stats = over_tiles(lambda carry, blk, r0: _score_stats_update(carry, blk), _score_stats_init(w))
    mx8, mn8, cp8, cn8 = stats
    c_pos = jnp.sum(cp8, axis=0, keepdims=True)
    c_nn = jnp.sum(cn8, axis=0, keepdims=True)
    hi = jnp.where(c_pos >= topk, jnp.max(mx8, axis=0, keepdims=True), 0.0)
    lo = jnp.where(c_nn >= topk, 0.0, jnp.min(mn8, axis=0, keepdims=True))

    live = lane_ok & jnp.logical_not(small)

    def bisect(_, carry):
        lo, hi = carry
        mid = 0.5 * lo + 0.5 * hi
        ge = count(lambda blk, r0: blk >= mid) >= topk
        return jnp.where(ge, mid, lo), jnp.where(ge, hi, mid)

    lo, hi = lax.fori_loop(0, BISECT_STEPS, bisect, (lo, hi))
    v0 = min_where(lambda blk, r0: blk >= lo)

    def gt_next(v):
        def f(carry, blk, r0):
            cg, nx = carry
            gt = blk > v
            return (cg + jnp.sum(fold(gt.astype(I32)), axis=0),
                    jnp.minimum(nx, jnp.min(fold(jnp.where(gt, blk, inf)), axis=0)))
        cg8, nx8 = over_tiles(f, (zeros8, pinf8))
        return jnp.sum(cg8, axis=0, keepdims=True), jnp.min(nx8, axis=0, keepdims=True)

    def peel(state):
        v, _, _ = state
        cg, nx = gt_next(v)
        move = (cg >= topk) & live
        return jnp.where(move, nx, v), cg, jnp.max(move.astype(I32))

    v, cnt_gt, _ = lax.while_loop(lambda s: s[2] > 0, peel,
                                  (v0, jnp.zeros((1, w), I32), jnp.int32(1)))

    need = jnp.where(live, (topk - cnt_gt).astype(F32), jnp.float32(2 ** 30))
    tb = tri_ref.shape[0]

    def drop_surplus(rows_per_step):
        def step(i, seen):
            r0 = pl.multiple_of(i * rows_per_step, rows_per_step)
            blk = sc_ref[pl.ds(r0, rows_per_step), :]
            eq = blk == v
            ones = jnp.where(eq, 1.0, 0.0).astype(BF16)
            ranks = [jnp.dot(tri_ref[...], ones[s * tb:(s + 1) * tb, :], preferred_element_type=F32)
                     for s in range(rows_per_step // tb)]
            for s in range(rows_per_step // tb):
                rows = slice(s * tb, (s + 1) * tb)
                rank = ranks[s] + seen
                sc_ref[pl.ds(r0 + s * tb, tb), :] = jnp.where(
                    eq[rows], jnp.where(rank > need, -inf, blk[rows]), blk[rows])
                seen = rank[tb - 1:tb, :]
            return seen
        return step

    seen = lax.fori_loop(0, nt // 2, drop_surplus(2 * tr), jnp.zeros((1, w), F32))
    lax.fori_loop(2 * (nt // 2), nt, drop_surplus(tr), seen)

    return jnp.where(small, -inf, v)


def _dsa_prompt_body(relb_ref, btab_ref, tri_ref, ki_ref, qi_ref, wi_ref, kd_ref, vt_ref, qd_ref, o_ref,
                     sc_scr, tbl_scr, q2_scr, acc_scr, m_scr, o_scr, lg_scr, pr_scr, *,
                     topk, far_bucket):
    b = pl.program_id(0)
    j = pl.program_id(1)
    tk = DSA_TK
    hd = DSA_HEAD_DIM

    @pl.when((b == 0) & (j == 0))
    def _():
        q2_scr[...] = jnp.zeros(q2_scr.shape, BF16)

        def build(ci, carry):
            r0 = pl.multiple_of(ci * LANES, LANES)
            bt = btab_ref[pl.ds(r0, LANES), :]
            for h in range(DSA_HEADS):
                t = jnp.zeros(bt.shape, F32)
                for bk in range(REL_BUCKETS):
                    t = jnp.where(bt == bk, (relb_ref[bk, h] - relb_ref[far_bucket, h]) * LOG2E, t)
                tbl_scr[h, pl.ds(r0, LANES), :] = t
            return carry

        lax.fori_loop(0, BIAS_TAB_ROWS // LANES, build, 0)

    nt = j // (tk // QUERY_BLOCK) + 1
    qpos = j * QUERY_BLOCK + lax.broadcasted_iota(I32, (1, LANES), 1)
    for p in range(DSA_HEADS // 2):
        q2_scr[p, 0:hd, 0:LANES] = qd_ref[2 * p * hd:(2 * p + 1) * hd, :]
        q2_scr[p, hd:2 * hd, LANES:2 * LANES] = qd_ref[(2 * p + 1) * hd:(2 * p + 2) * hd, :]

    wi = wi_ref[...]

    def score_tile(i, carry, last):
        r0 = pl.multiple_of(i * tk, tk)
        s4 = jnp.dot(ki_ref[pl.ds(r0, tk), :], qi_ref[...], preferred_element_type=F32)
        sc = jnp.zeros((tk, LANES), F32)
        for h in range(IDX_HEADS):
            sc = sc + jnp.maximum(s4[:, h * LANES:(h + 1) * LANES], 0.0) * wi[h:h + 1, :]
        if last:
            kpos = r0 + lax.broadcasted_iota(I32, (tk, LANES), 0)
            sc = jnp.where(kpos <= qpos, sc, -jnp.inf)
        sc_scr[pl.ds(r0, tk), :] = sc
        return _score_stats_update(carry, sc, finite=not last)

    stats = lax.fori_loop(0, nt - 1, lambda i, c: score_tile(i, c, False), _score_stats_init(LANES))
    stats = score_tile(nt - 1, stats, True)

    small = (qpos + 1) < topk
    kstar = _select_threshold(sc_scr, tri_ref, nt, tk, topk, small, jnp.full((1, LANES), True), stats)

    m_scr[...] = jnp.full(m_scr.shape, NEG, F32)
    acc_scr[...] = jnp.zeros(acc_scr.shape, F32)
    vrows = acc_scr.shape[0] // DSA_HEADS

    def logits_stage(i, buf, near):
        r0 = pl.multiple_of(i * tk, tk)
        blk = sc_scr[pl.ds(r0, tk), :]
        if near:
            kpos = r0 + lax.broadcasted_iota(I32, (tk, LANES), 0)
            addm = jnp.where(blk >= kstar, jnp.where(kpos <= qpos, 0.0, NEG), NEG)
            off = pl.multiple_of(i * tk - j * QUERY_BLOCK + BIAS_TAB_OFF, LANES)
        else:
            addm = jnp.where(blk >= kstar, 0.0, NEG)
        tile_max = []
        for p in range(DSA_HEADS // 2):
            lg2 = jnp.dot(kd_ref[pl.ds(r0, tk), p * LANES:(p + 1) * LANES], q2_scr[p],
                          preferred_element_type=F32)
            for hh in range(2):
                h = 2 * p + hh
                lg = lg2[:, hh * LANES:(hh + 1) * LANES] + addm
                if near:
                    lg = lg + tbl_scr[h, pl.ds(off, tk), :]
                lg_scr[buf, h] = lg
                tile_max.append(jnp.max(lg, axis=0, keepdims=True))
        return tuple(tile_max)

    def softmax_pv_stage(i, buf, tile_max):
        alpha = []
        for h in range(DSA_HEADS):
            m_old = m_scr[h:h + 1, :]
            m_new = jnp.maximum(m_old, tile_max[h])
            m_scr[h:h + 1, :] = m_new
            alpha.append(jnp.exp2(m_old - m_new))
            pr_scr[h] = jnp.exp2((lg_scr[buf, h] - m_new).astype(BF16))
        for h in range(DSA_HEADS):
            rows = slice(h * vrows, (h + 1) * vrows)
            acc_scr[rows, :] = alpha[h] * acc_scr[rows, :] + jnp.dot(
                vt_ref[i, rows, :], pr_scr[h], preferred_element_type=F32)

    near_tiles = jnp.where(j % (tk // QUERY_BLOCK) == 0, 2, 1)
    n_far = jnp.maximum(nt - near_tiles, 0)

    @pl.when(n_far > 0)
    def _():
        n_pairs = (n_far - 1) // 2

        def pair(it, tile_max):
            i0 = 2 * it
            max1 = logits_stage(i0 + 1, 1, False)
            softmax_pv_stage(i0, 0, tile_max)
            max2 = logits_stage(i0 + 2, 0, False)
            softmax_pv_stage(i0 + 1, 1, max1)
            return max2

        tile_max = lax.fori_loop(0, n_pairs, pair, logits_stage(0, 0, False))
        last = 2 * n_pairs

        @pl.when(n_far - last == 1)
        def _():
            softmax_pv_stage(last, 0, tile_max)

        @pl.when(n_far - last == 2)
        def _():
            max1 = logits_stage(last + 1, 1, False)
            softmax_pv_stage(last, 0, tile_max)
            softmax_pv_stage(last + 1, 1, max1)

    def near_tile(i, carry):
        softmax_pv_stage(i, 0, logits_stage(i, 0, True))
        return carry

    lax.fori_loop(n_far, nt, near_tile, 0)

    for h in range(DSA_HEADS):
        o_scr[h * hd:(h + 1) * hd, :] = (acc_scr[h * vrows:h * vrows + hd, :]
                                         / acc_scr[h * vrows + hd:h * vrows + hd + 1, :])
    o_ref[...] = o_scr[...].T.astype(BF16)


def _tri_ones(n):
    return jnp.asarray(np.tril(np.ones((n, n), np.float32)), BF16)


def _bias_bucket_table():
    u = np.arange(BIAS_TAB_ROWS)[:, None]
    r = np.arange(LANES)[None, :]
    return jnp.asarray(_t5_bucket_np(r + BIAS_TAB_OFF - u), I32)


def _dsa_prompt(qd, kd_bf, vt, misc, kin, rel_bias, nb, l):
    nq = l // QUERY_BLOCK
    topk = min(TOPK_MAX, l // 4)
    assert l % DSA_TK == 0
    far = _t5_bucket_np(np.arange(REL_MAX_DIST, max(l, REL_MAX_DIST + 1)))
    assert (far == far[0]).all()
    ki = kin.reshape(nb, l, IDX_DIM).astype(BF16)
    qi = misc[:, MISC_QI:MISC_QI + IDX_HEADS * IDX_DIM].reshape(nb, nq, QUERY_BLOCK, IDX_HEADS, IDX_DIM)
    qi = qi.transpose(0, 4, 1, 3, 2).reshape(nb, IDX_DIM, nq * IDX_HEADS * QUERY_BLOCK).astype(BF16)
    wi = misc[:, MISC_WI:MISC_WI + IDX_HEADS] * ((IDX_DIM ** -0.5) * (IDX_HEADS ** -0.5))
    wi = wi.reshape(nb, nq, QUERY_BLOCK, IDX_HEADS).transpose(0, 1, 3, 2)
    wi = jnp.pad(wi, ((0, 0), (0, 0), (0, SUBLANES - IDX_HEADS), (0, 0))).reshape(nb, nq * SUBLANES, QUERY_BLOCK)
    kd_bf = kd_bf.reshape(nb, l, DSA_WIDTH)
    body = functools.partial(_dsa_prompt_body, topk=topk, far_bucket=int(far[0]))
    whole = lambda b, j: (b, 0, 0)
    return pl.pallas_call(
        body,
        grid=(nb, nq),
        in_specs=[pl.BlockSpec(memory_space=pltpu.SMEM),
                  pl.BlockSpec((BIAS_TAB_ROWS, LANES), lambda b, j: (0, 0), pipeline_mode=pl.Buffered(1)),
                  pl.BlockSpec((LANES, LANES), lambda b, j: (0, 0), pipeline_mode=pl.Buffered(1)),
                  pl.BlockSpec((None, l, IDX_DIM), whole, pipeline_mode=pl.Buffered(1)),
                  pl.BlockSpec((None, IDX_DIM, IDX_HEADS * QUERY_BLOCK), lambda b, j: (b, 0, j)),
                  pl.BlockSpec((None, SUBLANES, QUERY_BLOCK), lambda b, j: (b, j, 0)),
                  pl.BlockSpec((None, l, DSA_WIDTH), whole, pipeline_mode=pl.Buffered(1)),
                  pl.BlockSpec((None, l // DSA_TK, DSA_HEADS * V_ROWS, DSA_TK), lambda b, j: (b, 0, 0, 0),
                               pipeline_mode=pl.Buffered(1)),
                  pl.BlockSpec((None, DSA_WIDTH, QUERY_BLOCK), lambda b, j: (b, 0, j))],
        out_specs=pl.BlockSpec((None, QUERY_BLOCK, DSA_WIDTH), lambda b, j: (b, j, 0)),
        out_shape=jax.ShapeDtypeStruct((nb, l, DSA_WIDTH), BF16),
        scratch_shapes=[pltpu.VMEM((l, LANES), F32),
                        pltpu.VMEM((DSA_HEADS, BIAS_TAB_ROWS, LANES), F32),
                        pltpu.VMEM((DSA_HEADS // 2, LANES, 2 * LANES), BF16),
                        pltpu.VMEM((DSA_HEADS * V_ROWS, LANES), F32),
                        pltpu.VMEM((DSA_HEADS, LANES), F32),
                        pltpu.VMEM((DSA_WIDTH, LANES), F32),
                        pltpu.VMEM((2, DSA_HEADS, DSA_TK, LANES), F32),
                        pltpu.VMEM((DSA_HEADS, DSA_TK, LANES), BF16)],
        compiler_params=pltpu.CompilerParams(dimension_semantics=("arbitrary", "arbitrary"),
                                             vmem_limit_bytes=VMEM_LIMIT),
        name="dsa_prompt",
    )(rel_bias, _bias_bucket_table(), _tri_ones(LANES), ki, qi, wi, kd_bf, vt, qd)


def _select_threshold_lanes(sc_ref, triu_ref, nt, topk, small, row_ok):
    r = sc_ref.shape[0]
    inf = jnp.float32(jnp.inf)
    tiles = [slice(i * LANES, (i + 1) * LANES) for i in range(nt)]
    rowsum = lambda x: jnp.sum(x, axis=1, keepdims=True)
    rowmin = lambda x: jnp.min(x, axis=1, keepdims=True)

    def count(pred):
        acc = jnp.zeros((r, LANES), I32)
        for t in tiles:
            acc = acc + pred(sc_ref[:, t]).astype(I32)
        return rowsum(acc)

    def min_where(pred):
        acc = jnp.full((r, LANES), inf, F32)
        for t in tiles:
            blk = sc_ref[:, t]
            acc = jnp.minimum(acc, jnp.where(pred(blk), blk, inf))
        return rowmin(acc)

    mx = jnp.full((r, LANES), -inf, F32)
    mn = jnp.full((r, LANES), inf, F32)
    c_pos = jnp.zeros((r, LANES), I32)
    c_nn = jnp.zeros((r, LANES), I32)
    for t in tiles:
        blk = sc_ref[:, t]
        mx = jnp.maximum(mx, blk)
        mn = jnp.minimum(mn, jnp.where(blk == -inf, inf, blk))
        c_pos = c_pos + (blk > 0.0).astype(I32)
        c_nn = c_nn + (blk >= 0.0).astype(I32)
    hi = jnp.where(rowsum(c_pos) >= topk, jnp.max(mx, axis=1, keepdims=True), 0.0)
    lo = jnp.where(rowsum(c_nn) >= topk, 0.0, rowmin(mn))
    live = row_ok & jnp.logical_not(small)

    def bisect(_, carry):
        lo, hi = carry
        mid = 0.5 * lo + 0.5 * hi
        ge = count(lambda blk: blk >= mid) >= topk
        return jnp.where(ge, mid, lo), jnp.where(ge, hi, mid)

    lo, hi = lax.fori_loop(0, BISECT_STEPS, bisect, (lo, hi))
    v0 = min_where(lambda blk: blk >= lo)

    def peel(state):
        v, _, _ = state
        cg = jnp.zeros((r, LANES), I32)
        nx = jnp.full((r, LANES), inf, F32)
        for t in tiles:
            blk = sc_ref[:, t]
            gt = blk > v
            cg = cg + gt.astype(I32)
            nx = jnp.minimum(nx, jnp.where(gt, blk, inf))
        cg = rowsum(cg)
        move = (cg >= topk) & live
        return jnp.where(move, rowmin(nx), v), cg, jnp.max(move.astype(I32))

    v, cnt_gt, _ = lax.while_loop(lambda s: s[2] > 0, peel,
                                  (v0, jnp.zeros((r, 1), I32), jnp.int32(1)))
    excess = (count(lambda blk: blk >= v) > topk) & live

    @pl.when(jnp.max(excess.astype(I32)) > 0)
    def _():
        need = (topk - cnt_gt).astype(F32)
        seen = jnp.zeros((r, 1), F32)
        for t in tiles:
            blk = sc_ref[:, t]
            eq = blk == v
            rank = seen + jnp.dot(jnp.where(eq, 1.0, 0.0).astype(BF16), triu_ref[...],
                                  preferred_element_type=F32)
            sc_ref[:, t] = jnp.where(eq & (rank > need) & excess, -inf, blk)
            seen = rank[:, LANES - 1:LANES]

    return jnp.where(small, -inf, v)


SEQ_GROUP = LANES // SUBLANES
NEW_ROWS = 16


def _dsa_sample_select_body(pt_ref, *refs, n_pages, t_new, topk):
    page_refs = refs[:n_pages]
    knew_ref, qi_ref, wi_ref, triu_ref, mask_ref, sc_scr = refs[n_pages:]
    g = pl.program_id(1)
    past = n_pages * PAGE_SIZE
    rows = pl.ds(pl.multiple_of(g * SUBLANES, SUBLANES), SUBLANES)
    for p in range(n_pages + 1):
        keys_t = (page_refs[p][...] if p < n_pages else knew_ref[...]).astype(BF16)
        s = jnp.dot(qi_ref[...], keys_t, preferred_element_type=F32)
        sc = jnp.zeros((SUBLANES, PAGE_SIZE), F32)
        for h in range(IDX_HEADS):
            hs = slice(h * SUBLANES, (h + 1) * SUBLANES)
            sc = sc + jnp.maximum(s[hs, :], 0.0) * wi_ref[hs, :]
        sc_scr[rows, p * PAGE_SIZE:(p + 1) * PAGE_SIZE] = sc

    @pl.when(g == SEQ_GROUP - 1)
    def _():
        q_of_row = lax.broadcasted_iota(I32, (LANES, 1), 0) % SUBLANES
        new = slice(past, past + PAGE_SIZE)
        cpos = lax.broadcasted_iota(I32, (LANES, PAGE_SIZE), 1)
        sc_scr[:, new] = jnp.where(cpos <= q_of_row, sc_scr[:, new], -jnp.inf)
        row_ok = q_of_row < t_new
        small = (past + q_of_row + 1) < topk
        kstar = _select_threshold_lanes(sc_scr, triu_ref, n_pages + 1, topk, small, row_ok)
        for p in range(n_pages + 1):
            t = slice(p * PAGE_SIZE, (p + 1) * PAGE_SIZE)
            mask_ref[:, t] = jnp.where(sc_scr[:, t] >= kstar, 1.0, 0.0)


def _dsa_sample_select(page_table, kidx_t, kin_new_t, qi_rows, wi_rows, t_new, topk):
    db, n_pages = page_table.shape
    ng = db // SEQ_GROUP
    width = (n_pages + 1) * PAGE_SIZE
    seq = lambda gi, g, pt: (gi * SEQ_GROUP + g, 0, 0)
    page_specs = [pl.BlockSpec((None, IDX_DIM, PAGE_SIZE),
                               functools.partial(lambda gi, g, pt, p: (pt[gi * SEQ_GROUP + g, p], 0, 0), p=p))
                  for p in range(n_pages)]
    body = functools.partial(_dsa_sample_select_body, n_pages=n_pages, t_new=t_new, topk=topk)
    grid_spec = pltpu.PrefetchScalarGridSpec(
        num_scalar_prefetch=1,
        grid=(ng, SEQ_GROUP),
        in_specs=page_specs + [pl.BlockSpec((None, IDX_DIM, PAGE_SIZE), seq),
                               pl.BlockSpec((None, IDX_HEADS * SUBLANES, IDX_DIM), seq),
                               pl.BlockSpec((None, IDX_HEADS * SUBLANES, PAGE_SIZE), seq),
                               pl.BlockSpec((LANES, LANES), lambda gi, g, pt: (0, 0))],
        out_specs=pl.BlockSpec((None, LANES, width), lambda gi, g, pt: (gi, 0, 0)),
        scratch_shapes=[pltpu.VMEM((LANES, width), F32)],
    )
    triu = jnp.asarray(np.triu(np.ones((LANES, LANES), np.float32)), BF16)
    return pl.pallas_call(
        body, grid_spec=grid_spec,
        out_shape=jax.ShapeDtypeStruct((ng, LANES, width), F32),
        compiler_params=pltpu.CompilerParams(dimension_semantics=("arbitrary", "arbitrary"),
                                             vmem_limit_bytes=VMEM_LIMIT),
        name="dsa_sample_select",
    )(page_table, *([kidx_t] * n_pages), kin_new_t, qi_rows, wi_rows, triu)


def _dsa_sample_attend_body(pt_ref, *refs, n_pages):
    k_refs = refs[:n_pages]
    v_refs = refs[n_pages:2 * n_pages]
    (knew_ref, vnew_ref, mask_ref, qbd_ref, btab_ref, rb_ref, o_ref, lg_scr, tbl_scr) = refs[2 * n_pages:]
    b = pl.program_id(0)
    hq = DSA_HEADS * SUBLANES

    @pl.when(b == 0)
    def _():
        for half in range(2):
            cols = slice(half * PAGE_SIZE, (half + 1) * PAGE_SIZE)
            bt = btab_ref[:, cols]
            t = jnp.zeros(bt.shape, F32)
            for bk in range(REL_BUCKETS):
                t = jnp.where(bt == bk, rb_ref[bk], t)
            tbl_scr[:, cols] = t

    far_bias = rb_ref[REL_BUCKETS - 1]
    page = lambda ref: ref[...].reshape(DSA_WIDTH, PAGE_SIZE).astype(BF16)

    def masked(lg, sel):
        return lg + jnp.where(jnp.concatenate([sel] * DSA_HEADS, axis=0) > 0.5, 0.0, NEG)

    m = jnp.full((hq, LANES), NEG, F32)
    for p in range(n_pages):
        cols = slice(p * PAGE_SIZE, (p + 1) * PAGE_SIZE)
        lg = jnp.dot(qbd_ref[...], page(k_refs[p]), preferred_element_type=F32)
        lg = lg + (tbl_scr[:, 0:PAGE_SIZE] if p == n_pages - 1 else far_bias)
        lg = masked(lg, mask_ref[:, cols])
        lg_scr[:, cols] = lg
        m = jnp.maximum(m, lg)
    past = n_pages * PAGE_SIZE
    lg_new = lax.dot_general(qbd_ref[...], knew_ref[...].astype(BF16), _NT, preferred_element_type=F32)
    lg_new = masked(lg_new + tbl_scr[:, PAGE_SIZE:PAGE_SIZE + NEW_ROWS], mask_ref[:, past:past + NEW_ROWS])
    m = jnp.maximum(jnp.max(m, axis=1, keepdims=True), jnp.max(lg_new, axis=1, keepdims=True))

    pr_new = jnp.exp(lg_new - m)
    acc = jnp.dot(pr_new.astype(BF16), vnew_ref[...].astype(BF16), preferred_element_type=F32)
    lsum = jnp.zeros((hq, LANES), F32)
    for p in range(n_pages):
        cols = slice(p * PAGE_SIZE, (p + 1) * PAGE_SIZE)
        pr = jnp.exp(lg_scr[:, cols] - m)
        lsum = lsum + pr
        acc = acc + lax.dot_general(pr.astype(BF16), page(v_refs[p]), _NT, preferred_element_type=F32)
    lsum = jnp.sum(lsum, axis=1, keepdims=True) + jnp.sum(pr_new, axis=1, keepdims=True)
    acc = acc / lsum
    head_of_lane = lax.broadcasted_iota(I32, (SUBLANES, DSA_WIDTH), 1) // DSA_HEAD_DIM
    out = jnp.zeros((SUBLANES, DSA_WIDTH), F32)
    for h in range(DSA_HEADS):
        out = jnp.where(head_of_lane == h, acc[h * SUBLANES:(h + 1) * SUBLANES, :], out)
    o_ref[...] = out


def _dsa_sample_attend(page_table, k_t, v_t, k_new, v_new, mask, qbd, rel_bias):
    db, n_pages = page_table.shape
    hq = DSA_HEADS * SUBLANES
    width = (n_pages + 1) * PAGE_SIZE
    seq = lambda b, pt: (b, 0, 0)
    page = lambda p: functools.partial(lambda b, pt, p: (pt[b, p], 0, 0, 0), p=p)
    kv_spec = lambda p: pl.BlockSpec((None, DSA_HEADS, DSA_HEAD_DIM, PAGE_SIZE), page(p))
    u = np.arange(2 * PAGE_SIZE)[None, :]
    q = (np.arange(hq) % SUBLANES)[:, None]
    btab = jnp.asarray(_t5_bucket_np(PAGE_SIZE + q - u), I32)
    rb = jnp.broadcast_to(jnp.repeat(rel_bias, SUBLANES, axis=1)[:, :, None], (REL_BUCKETS, hq, PAGE_SIZE))
    grid_spec = pltpu.PrefetchScalarGridSpec(
        num_scalar_prefetch=1,
        grid=(db,),
        in_specs=[kv_spec(p) for p in range(n_pages)] + [kv_spec(p) for p in range(n_pages)] + [
            pl.BlockSpec((None, NEW_ROWS, DSA_WIDTH), seq),
            pl.BlockSpec((None, NEW_ROWS, DSA_WIDTH), seq),
            pl.BlockSpec((None, SUBLANES, width), lambda b, pt: (b // SEQ_GROUP, b % SEQ_GROUP, 0)),
            pl.BlockSpec((None, hq, DSA_WIDTH), seq),
            pl.BlockSpec(btab.shape, lambda b, pt: (0, 0)),
            pl.BlockSpec(rb.shape, lambda b, pt: (0, 0, 0))],
        out_specs=pl.BlockSpec((None, SUBLANES, DSA_WIDTH), seq),
        scratch_shapes=[pltpu.VMEM((hq, width - PAGE_SIZE), F32), pltpu.VMEM((hq, 2 * PAGE_SIZE), F32)],
    )
    return pl.pallas_call(
        functools.partial(_dsa_sample_attend_body, n_pages=n_pages), grid_spec=grid_spec,
        out_shape=jax.ShapeDtypeStruct((db, SUBLANES, DSA_WIDTH), F32),
        compiler_params=pltpu.CompilerParams(dimension_semantics=("arbitrary",),
                                             vmem_limit_bytes=VMEM_LIMIT),
        name="dsa_sample_attend",
    )(page_table, *([k_t] * n_pages), *([v_t] * n_pages), k_new, v_new, mask, qbd, btab, rb)


def _dsa_sample(zA, kd, vd, misc, kin, rel_bias, cache_k, cache_v, cache_kidx, page_table, db, t_new):
    n_pages = page_table.shape[1]
    past = n_pages * PAGE_SIZE
    topk = min(TOPK_MAX, (past + t_new) // 4)
    assert db % SEQ_GROUP == 0 and t_new <= SUBLANES
    new_t = lambda a, w: jnp.pad(a.reshape(db, t_new, w).transpose(0, 2, 1),
                                 ((0, 0), (0, 0), (0, PAGE_SIZE - t_new)))
    pad_q = lambda a: jnp.pad(a, ((0, 0), (0, 0), (0, SUBLANES - t_new), (0, 0)))
    new_rows = lambda a: jnp.pad(a.reshape(db, t_new, DSA_WIDTH), ((0, 0), (0, NEW_ROWS - t_new), (0, 0)))
    qi = misc[:, MISC_QI:MISC_QI + IDX_HEADS * IDX_DIM].reshape(db, t_new, IDX_HEADS, IDX_DIM)
    qi_rows = pad_q(qi.transpose(0, 2, 1, 3)).reshape(db, IDX_HEADS * SUBLANES, IDX_DIM).astype(BF16)
    wi = misc[:, MISC_WI:MISC_WI + IDX_HEADS] * ((IDX_DIM ** -0.5) * (IDX_HEADS ** -0.5))
    wi = pad_q(wi.reshape(db, t_new, IDX_HEADS, 1).transpose(0, 2, 1, 3)).reshape(db, IDX_HEADS * SUBLANES, 1)
    wi_rows = jnp.broadcast_to(wi, (db, IDX_HEADS * SUBLANES, PAGE_SIZE))
    mask = _dsa_sample_select(page_table, cache_kidx.transpose(0, 2, 1), new_t(kin, IDX_DIM),
                              qi_rows, wi_rows, t_new, topk)
    qd = (zA[:, ZA_QD:ZA_QD + DSA_WIDTH] * (DSA_HEAD_DIM ** -0.5)).reshape(db, t_new, DSA_HEADS, DSA_HEAD_DIM)
    qd = pad_q(qd.transpose(0, 2, 1, 3))
    eye = jnp.eye(DSA_HEADS, dtype=qd.dtype)
    qbd = (qd[:, :, :, None, :] * eye[None, :, None, :, None]).reshape(db, DSA_HEADS * SUBLANES, DSA_WIDTH)
    o = _dsa_sample_attend(page_table, cache_k.transpose(0, 2, 3, 1), cache_v.transpose(0, 2, 3, 1),
                           new_rows(kd), new_rows(vd), mask, qbd.astype(BF16), rel_bias)
    return o[:, :t_new, :].reshape(db * t_new, DSA_WIDTH).astype(BF16)


def _post_body(x_ref, og_ref, rg_ref, gg_ref, gd_ref, od_ref, wg_ref, wd_ref, wo_ref, gf_ref,
               x1_ref, hf_ref):
    half = x_ref.shape[0] // 2
    parts = [slice(0, half), slice(half, 2 * half)]
    branch = []
    for r in parts:
        rg = rg_ref[r, :]
        a = (og_ref[r, :] * (rg * _sigmoid(rg))).astype(BF16)
        branch.append((jnp.dot(a, wg_ref[...], preferred_element_type=F32),
                       jnp.dot(od_ref[r, :], wd_ref[...], preferred_element_type=F32)))
    for r, (y_g, y_d) in zip(parts, branch):
        mix = (_sigmoid(gg_ref[r, :]) * y_g + _sigmoid(gd_ref[r, :]) * y_d).astype(BF16)
        x1 = x_ref[r, :] + jnp.dot(mix, wo_ref[...], preferred_element_type=F32)
        x1_ref[r, :] = x1
        hf_ref[r, :] = _rms(x1, gf_ref[...]).astype(BF16)


def _post(x2d, og, zA, od, w_gla, w_dsa, w_o, g_ffn):
    n = x2d.shape[0]
    tm = min(n, 512)
    row = lambda i: (i, 0)
    const = lambda i: (0, 0)
    return pl.pallas_call(
        _post_body,
        grid=(n // tm,),
        in_specs=[pl.BlockSpec((tm, D_MODEL), row),
                  pl.BlockSpec((tm, GLA_DV), row),
                  pl.BlockSpec((tm, GLA_DV), lambda i: (i, COL_RG)),
                  pl.BlockSpec((tm, D_MODEL), lambda i: (i, COL_GG)),
                  pl.BlockSpec((tm, D_MODEL), lambda i: (i, COL_GD)),
                  pl.BlockSpec((tm, DSA_WIDTH), row),
                  pl.BlockSpec((GLA_DV, D_MODEL), const),
                  pl.BlockSpec((DSA_WIDTH, D_MODEL), const),
                  pl.BlockSpec((D_MODEL, D_MODEL), const),
                  pl.BlockSpec((1, D_MODEL), const)],
        out_specs=[pl.BlockSpec((tm, D_MODEL), row), pl.BlockSpec((tm, D_MODEL), row)],
        out_shape=[jax.ShapeDtypeStruct((n, D_MODEL), F32), jax.ShapeDtypeStruct((n, D_MODEL), BF16)],
        compiler_params=pltpu.CompilerParams(dimension_semantics=("arbitrary",),
                                             vmem_limit_bytes=VMEM_LIMIT),
        name="post_mix",
    )(x2d, og, zA, zA, zA, od, w_gla, w_dsa, w_o, g_ffn.reshape(1, -1))


FFN_TILE = 256
FFN_ROWS = 512


def _ffn_body(hf_ref, x1_ref, wg_ref, wu_ref, wd_ref, gfin_ref, y_ref, acc_scr):
    hf = hf_ref[...]
    n_chunks = wg_ref.shape[1] // FFN_TILE
    cols = lambda k: slice(k * FFN_TILE, (k + 1) * FFN_TILE)

    def gate_up(k):
        return (jnp.dot(hf, wg_ref[:, cols(k)], preferred_element_type=F32),
                jnp.dot(hf, wu_ref[:, cols(k)], preferred_element_type=F32))

    acc_scr[...] = x1_ref[...]
    nxt = gate_up(0)
    for k in range(n_chunks):
        gate, up = nxt
        if k + 1 < n_chunks:
            nxt = gate_up(k + 1)
        act = (gate * _sigmoid(gate) * up).astype(BF16)
        acc_scr[...] = acc_scr[...] + jnp.dot(act, wd_ref[cols(k), :], preferred_element_type=F32)
    y_ref[...] = _rms(acc_scr[...], gfin_ref[...])


def _ffn(hf, x1, w_gate, w_up, w_down, g_final):
    n = hf.shape[0]
    d_ff = w_gate.shape[1]
    assert d_ff % FFN_TILE == 0
    tm = min(n, FFN_ROWS)
    row = lambda i: (i, 0)
    const = lambda i: (0, 0)
    resident = lambda shape: pl.BlockSpec(shape, const, pipeline_mode=pl.Buffered(1))
    return pl.pallas_call(
        _ffn_body,
        grid=(n // tm,),
        in_specs=[pl.BlockSpec((tm, D_MODEL), row),
                  pl.BlockSpec((tm, D_MODEL), row),
                  resident((D_MODEL, d_ff)),
                  resident((D_MODEL, d_ff)),
                  resident((d_ff, D_MODEL)),
                  pl.BlockSpec((1, D_MODEL), const)],
        out_specs=pl.BlockSpec((tm, D_MODEL), row),
        out_shape=jax.ShapeDtypeStruct((n, D_MODEL), F32),
        scratch_shapes=[pltpu.VMEM((tm, D_MODEL), F32)],
        compiler_params=pltpu.CompilerParams(dimension_semantics=("arbitrary",),
                                             vmem_limit_bytes=VMEM_LIMIT),
        name="ffn",
    )(hf, x1, w_gate, w_up, w_down, g_final.reshape(1, -1))


def _gate_weight_pad(w_gate_up):
    pad = jnp.zeros((LANES, GLA_DK), w_gate_up.dtype)
    return pad.at[MISC_ALOW:MISC_ALOW + GLA_GATE_RANK].set(w_gate_up).astype(BF16)


def kernel(x_prompt, x_sample, cache_k, cache_v, cache_kidx, state_gla, page_table, g_mix, w_in,
           w_gate_up, b_gate, gla_norm_g, w_gla_branch, idx_k_g, idx_k_b, w_dsa_branch, w_o, g_ffn,
           w_ffn_gate, w_ffn_up, w_ffn_down, rel_bias, g_final):
    depth = w_in.shape[0]
    assert depth == 1, "the final RMSNorm is fused into the FFN kernel of the single layer"
    nb, l, _ = x_prompt.shape
    db, t_new, _ = x_sample.shape
    layer = 0
    w_parts = _split_w_in(w_in[layer])
    wgu_pad = _gate_weight_pad(w_gate_up[layer])
    w_gla = w_gla_branch[layer].astype(BF16)
    w_dsa = w_dsa_branch[layer].astype(BF16)
    w_out = w_o[layer].astype(BF16)
    w_fg, w_fu, w_fd = (w.astype(BF16) for w in (w_ffn_gate[layer], w_ffn_up[layer], w_ffn_down[layer]))

    xp = x_prompt.reshape(nb * l, D_MODEL)
    zA, qdt, kdb, kt, vt, vslab, misc, kin = _in_proj_prompt(xp, g_mix[layer], w_parts, idx_k_g[layer],
                                                             idx_k_b[layer], nb, l)
    zA3 = zA.reshape(nb, l, zA.shape[1])
    s0 = jnp.zeros((nb, GLA_HEADS, GLA_HEAD_DK, GLA_HEAD_DV), state_gla.dtype)
    og, s_p = _gla(zA3, zA3, zA3, misc.reshape(nb, l, PROJ_TILE), (COL_QG, COL_KG, COL_VG), wgu_pad,
                   b_gate[layer], gla_norm_g[layer], s0, GLA_CHUNK, GLA_TILE, GLA_TILE, nb)
    od = _dsa_prompt(qdt, kdb, vslab, misc, kin, rel_bias, nb, l)
    x1, hf = _post(xp, og.reshape(nb * l, GLA_DV), zA, od.reshape(nb * l, DSA_WIDTH), w_gla, w_dsa,
                   w_out, g_ffn[layer])
    y_p = _ffn(hf, x1, w_fg, w_fu, w_fd, g_final).reshape(nb, l, D_MODEL)

    xs = x_sample.reshape(db * t_new, D_MODEL)
    zA_s, kd_s, vd_s, misc_s, kin_s = _in_proj(xs, g_mix[layer], w_parts, idx_k_g[layer], idx_k_b[layer])
    pad_t = lambda a: jnp.pad(a.reshape(db, t_new, -1), ((0, 0), (0, SAMPLE_CHUNK - t_new), (0, 0)))
    og_s, s_s = _gla(pad_t(zA_s[:, 0:GLA_DK]), pad_t(zA_s[:, GLA_DK:2 * GLA_DK]),
                     pad_t(zA_s[:, 2 * GLA_DK:2 * GLA_DK + GLA_DV]), pad_t(misc_s[:, 0:LANES]), (0, 0, 0),
                     wgu_pad, b_gate[layer], gla_norm_g[layer], state_gla[layer], SAMPLE_CHUNK, SAMPLE_CHUNK, t_new, SAMPLE_GLA_SEQS)
    og_s = og_s[:, :t_new, :].reshape(db * t_new, GLA_DV)
    od_s = _dsa_sample(zA_s, kd_s, vd_s, misc_s, kin_s, rel_bias, cache_k[layer], cache_v[layer],
                       cache_kidx[layer], page_table, db, t_new)
    x1_s, hf_s = _post(xs, og_s, zA_s, od_s, w_gla, w_dsa, w_out, g_ffn[layer])
    y_s = _ffn(hf_s, x1_s, w_fg, w_fu, w_fd, g_final).reshape(db, t_new, D_MODEL)

    heads = lambda a, n, t: a.reshape(1, n, t, DSA_HEADS, DSA_HEAD_DIM)
    heads_t = lambda a: a.reshape(nb, DSA_HEADS, DSA_HEAD_DIM, l).transpose(0, 3, 1, 2)[None]
    return (y_p, y_s,
            heads_t(kt), heads_t(vt), kin.reshape(1, nb, l, IDX_DIM), s_p[None],
            heads(kd_s, db, t_new), heads(vd_s, db, t_new), kin_s.reshape(1, db, t_new, IDX_DIM), s_s[None])
```

```python
import functools
import math

import numpy as np
import jax
import jax.numpy as jnp
from jax import lax
from jax.experimental import pallas as pl
from jax.experimental.pallas import tpu as pltpu

F32, BF16, I32 = jnp.float32, jnp.bfloat16, jnp.int32

D_MODEL = 1024
GLA_HEADS = 4
GLA_HEAD_DK = 128
GLA_HEAD_DV = 256
GLA_DK = GLA_HEADS * GLA_HEAD_DK
GLA_DV = GLA_HEADS * GLA_HEAD_DV
GLA_GATE_RANK = 16
GLA_GATE_TAU = 16.0
DSA_HEADS = 8
DSA_HEAD_DIM = 64
DSA_WIDTH = DSA_HEADS * DSA_HEAD_DIM
IDX_HEADS = 4
IDX_DIM = 64
TOPK_MAX = 256
QUERY_BLOCK = 128
PAGE_SIZE = 128
REL_BUCKETS = 32
REL_MAX_DIST = 128
RMS_EPS = 1e-6
LN_EPS = 1e-6
SPLIT_SIZES = (GLA_DK, GLA_DK, GLA_DV, GLA_DV, GLA_GATE_RANK, DSA_WIDTH, DSA_WIDTH, DSA_WIDTH,
               IDX_HEADS * IDX_DIM, IDX_DIM, IDX_HEADS, D_MODEL, D_MODEL)

LANES = 128
SUBLANES = 8
VMEM_LIMIT = 56 * 1024 * 1024

PROJ_TILE = 512
ZA_TILES = 11
ZA_WIDTH = ZA_TILES * PROJ_TILE
COL_QG, COL_KG = 0, 1
COL_VG, COL_RG, COL_GG, COL_GD = 1, 2, 3, 4
ZA_QD = 10 * PROJ_TILE
MISC_ALOW, MISC_WI, MISC_QI = 64, 80, 128

GLA_CHUNK = 64
GLA_TILE = 256
SAMPLE_CHUNK = 16
SAMPLE_GLA_SEQS = 4
DSA_TK = 512
BIAS_TAB_OFF = 2 * DSA_TK - QUERY_BLOCK
BIAS_TAB_ROWS = BIAS_TAB_OFF + DSA_TK
V_ROWS = 80
NEG = -1e30
LOG2E = 1.4426950408889634
BISECT_STEPS = 20


def _sigmoid(x):
    return 1.0 / (1.0 + jnp.exp(-x))


def _rms(x, g):
    return x * lax.rsqrt(jnp.mean(x * x, axis=-1, keepdims=True) + RMS_EPS) * g


def _t5_bucket_np(n):
    n = np.maximum(np.asarray(n, np.int64), 0)
    max_exact = REL_BUCKETS // 2
    large = max_exact + (np.log(np.maximum(n, 1).astype(np.float32) / np.float32(max_exact))
                         / np.float32(math.log(REL_MAX_DIST / max_exact))
                         * np.float32(REL_BUCKETS - max_exact)).astype(np.int32)
    large = np.minimum(large, REL_BUCKETS - 1)
    return np.where(n < max_exact, n, large).astype(np.int32)


def _inproj_wide_body(x_ref, g_ref, w_ref, za_ref):
    h = _rms(x_ref[...], g_ref[...]).astype(BF16)
    for j in range(ZA_TILES):
        cols = slice(j * PROJ_TILE, (j + 1) * PROJ_TILE)
        za_ref[:, cols] = jnp.dot(h, w_ref[:, cols], preferred_element_type=F32)


def _idx_key_norm(misc, ikg_ref, ikb_ref):
    ki = misc[:, 0:IDX_DIM]
    mu = jnp.mean(ki, axis=-1, keepdims=True)
    var = jnp.mean(jnp.square(ki - mu), axis=-1, keepdims=True)
    return (ki - mu) * lax.rsqrt(var + LN_EPS) * ikg_ref[...] + ikb_ref[...]


def _inproj_tail_body(x_ref, g_ref, w_ref, ikg_ref, ikb_ref, kd_ref, vd_ref, misc_ref, kin_ref):
    h = _rms(x_ref[...], g_ref[...]).astype(BF16)
    tile = lambda j: jnp.dot(h, w_ref[:, j * PROJ_TILE:(j + 1) * PROJ_TILE], preferred_element_type=F32)
    kd_ref[...] = tile(0)
    vd_ref[...] = tile(1)
    res = tile(2)
    misc_ref[...] = res
    kin_ref[...] = _idx_key_norm(res, ikg_ref, ikb_ref)


def _inproj_wide_prompt_body(x_ref, g_ref, w_ref, wqt_ref, za_ref, qdt_ref):
    h = _rms(x_ref[...], g_ref[...]).astype(BF16)
    for j in range(ZA_TILES - 1):
        cols = slice(j * PROJ_TILE, (j + 1) * PROJ_TILE)
        za_ref[:, cols] = jnp.dot(h, w_ref[:, cols], preferred_element_type=F32)
    qdt = lax.dot_general(wqt_ref[...], h, _NT, preferred_element_type=F32)
    qdt_ref[...] = (qdt * ((DSA_HEAD_DIM ** -0.5) * LOG2E)).astype(BF16)


def _inproj_tail_prompt_body(x_ref, g_ref, w_ref, wkt_ref, wvt_ref, ikg_ref, ikb_ref,
                             kdb_ref, kt_ref, vt_ref, vslab_ref, misc_ref, kin_ref):
    h = _rms(x_ref[...], g_ref[...]).astype(BF16)
    kdb_ref[...] = jnp.dot(h, w_ref[:, 0:PROJ_TILE], preferred_element_type=F32).astype(BF16)
    kt_ref[...] = lax.dot_general(wkt_ref[...], h, _NT, preferred_element_type=F32)
    vt = lax.dot_general(wvt_ref[...], h, _NT, preferred_element_type=F32)
    vt_ref[...] = vt
    pad_rows = V_ROWS - DSA_HEAD_DIM
    ones_row = jnp.where(lax.broadcasted_iota(I32, (pad_rows, DSA_TK), 0) == 0, 1.0, 0.0).astype(BF16)
    for s in range(vslab_ref.shape[0]):
        for hd in range(DSA_HEADS):
            r0 = hd * V_ROWS
            vslab_ref[s, r0:r0 + DSA_HEAD_DIM, :] = vt[hd * DSA_HEAD_DIM:(hd + 1) * DSA_HEAD_DIM,
                                                       s * DSA_TK:(s + 1) * DSA_TK].astype(BF16)
            vslab_ref[s, r0 + DSA_HEAD_DIM:r0 + V_ROWS, :] = ones_row
    res = jnp.dot(h, w_ref[:, PROJ_TILE:2 * PROJ_TILE], preferred_element_type=F32)
    misc_ref[...] = res
    kin_ref[...] = _idx_key_norm(res, ikg_ref, ikb_ref)


def _split_w_in(w_in):
    w = w_in.astype(BF16)
    pts = np.cumsum((0,) + SPLIT_SIZES)
    seg = [w[:, int(pts[i]):int(pts[i + 1])] for i in range(len(SPLIT_SIZES))]
    q_g, k_g, v_g, r_g, a_low, q_d, k_d, v_d, q_i, k_i, w_i, gate_g, gate_d = seg
    z = lambda n: jnp.zeros((w.shape[0], n), w.dtype)
    misc = jnp.concatenate([k_i, a_low, w_i, z(LANES - MISC_WI - IDX_HEADS), q_i,
                            z(PROJ_TILE - MISC_QI - IDX_HEADS * IDX_DIM)], axis=1)
    wide = jnp.concatenate([q_g, k_g, v_g, r_g, gate_g, gate_d, q_d], axis=1)
    return wide, q_d, k_d, v_d, misc


def _in_proj_specs():
    row = lambda i: (i, 0)
    const = lambda i: (0, 0)
    resident = lambda shape: pl.BlockSpec(shape, const, pipeline_mode=pl.Buffered(1))
    params = pltpu.CompilerParams(dimension_semantics=("arbitrary",), vmem_limit_bytes=VMEM_LIMIT)
    return row, const, resident, params


def _in_proj(x2d, g_mix, w_parts, idx_k_g, idx_k_b):
    wide, q_d, k_d, v_d, misc_w = w_parts
    n = x2d.shape[0]
    row, const, resident, params = _in_proj_specs()
    g2 = g_mix.reshape(1, -1)
    tm = min(n, 512)
    zA = pl.pallas_call(
        _inproj_wide_body,
        grid=(n // tm,),
        in_specs=[pl.BlockSpec((tm, D_MODEL), row), pl.BlockSpec((1, D_MODEL), const),
                  resident((D_MODEL, ZA_WIDTH))],
        out_specs=pl.BlockSpec((tm, ZA_WIDTH), row),
        out_shape=jax.ShapeDtypeStruct((n, ZA_WIDTH), F32),
        compiler_params=params,
        name="in_proj_wide",
    )(x2d, g2, wide)
    tm = min(n, 1024)
    w_tail = jnp.concatenate([k_d, v_d, misc_w], axis=1)
    kd, vd, misc, kin = pl.pallas_call(
        _inproj_tail_body,
        grid=(n // tm,),
        in_specs=[pl.BlockSpec((tm, D_MODEL), row), pl.BlockSpec((1, D_MODEL), const),
                  resident((D_MODEL, w_tail.shape[1])),
                  pl.BlockSpec((1, IDX_DIM), const), pl.BlockSpec((1, IDX_DIM), const)],
        out_specs=[pl.BlockSpec((tm, PROJ_TILE), row)] * 3 + [pl.BlockSpec((tm, IDX_DIM), row)],
        out_shape=[jax.ShapeDtypeStruct((n, PROJ_TILE), F32)] * 3 + [jax.ShapeDtypeStruct((n, IDX_DIM), F32)],
        compiler_params=params,
        name="in_proj_tail",
    )(x2d, g2, w_tail, idx_k_g.reshape(1, -1), idx_k_b.reshape(1, -1))
    return zA, kd, vd, misc, kin


def _in_proj_prompt(x2d, g_mix, w_parts, idx_k_g, idx_k_b, nb, l):
    wide, q_d, k_d, v_d, misc_w = w_parts
    n = x2d.shape[0]
    row, const, resident, params = _in_proj_specs()
    g2 = g_mix.reshape(1, -1)
    za_w = ZA_WIDTH - PROJ_TILE
    tm = 512
    per_b = l // tm
    zA, qdt = pl.pallas_call(
        _inproj_wide_prompt_body,
        grid=(n // tm,),
        in_specs=[pl.BlockSpec((tm, D_MODEL), row), pl.BlockSpec((1, D_MODEL), const),
                  resident((D_MODEL, za_w)), resident((DSA_WIDTH, D_MODEL))],
        out_specs=[pl.BlockSpec((tm, za_w), row),
                   pl.BlockSpec((None, DSA_WIDTH, tm), lambda i: (i // per_b, 0, i % per_b))],
        out_shape=[jax.ShapeDtypeStruct((n, za_w), F32), jax.ShapeDtypeStruct((nb, DSA_WIDTH, l), BF16)],
        compiler_params=params,
        name="in_proj_wide",
    )(x2d, g2, wide[:, :za_w], q_d.T)
    tm = 1024
    per_t = l // tm
    slabs = tm // DSA_TK
    tok = lambda i: (i // per_t, 0, i % per_t)
    w_tail = jnp.concatenate([k_d, misc_w], axis=1)
    kdb, kt, vt, vslab, misc, kin = pl.pallas_call(
        _inproj_tail_prompt_body,
        grid=(n // tm,),
        in_specs=[pl.BlockSpec((tm, D_MODEL), row), pl.BlockSpec((1, D_MODEL), const),
                  resident((D_MODEL, w_tail.shape[1])), resident((DSA_WIDTH, D_MODEL)),
                  resident((DSA_WIDTH, D_MODEL)),
                  pl.BlockSpec((1, IDX_DIM), const), pl.BlockSpec((1, IDX_DIM), const)],
        out_specs=[pl.BlockSpec((tm, DSA_WIDTH), row),
                   pl.BlockSpec((None, DSA_WIDTH, tm), tok),
                   pl.BlockSpec((None, DSA_WIDTH, tm), tok),
                   pl.BlockSpec((None, slabs, DSA_HEADS * V_ROWS, DSA_TK), lambda i: (i // per_t, i % per_t, 0, 0)),
                   pl.BlockSpec((tm, PROJ_TILE), row),
                   pl.BlockSpec((tm, IDX_DIM), row)],
        out_shape=[jax.ShapeDtypeStruct((n, DSA_WIDTH), BF16),
                   jax.ShapeDtypeStruct((nb, DSA_WIDTH, l), F32),
                   jax.ShapeDtypeStruct((nb, DSA_WIDTH, l), F32),
                   jax.ShapeDtypeStruct((nb, l // DSA_TK, DSA_HEADS * V_ROWS, DSA_TK), BF16),
                   jax.ShapeDtypeStruct((n, PROJ_TILE), F32),
                   jax.ShapeDtypeStruct((n, IDX_DIM), F32)],
        compiler_params=params,
        name="in_proj_tail",
    )(x2d, g2, w_tail, k_d.T, v_d.T, idx_k_g.reshape(1, -1), idx_k_b.reshape(1, -1))
    return zA, qdt, kdb, kt, vt, vslab, misc, kin


def _gla_consts(c):
    nlev = int(math.log2(c))
    t = np.arange(c)[:, None]
    s = np.arange(c)[None, :]
    mats = [(s <= t), np.ones((c, c), bool)]
    masks = []
    for l in range(nlev):
        mid = ((t >> (l + 1)) << (l + 1)) + (1 << l) - 1
        mats.append(s <= mid)
        masks.append(((t >> (l + 1)) == (s >> (l + 1))) & (((t >> l) & 1) == 1) & (((s >> l) & 1) == 0))
    masks.append(t == s)
    return (jnp.asarray(np.concatenate(mats, 0).astype(np.float32), BF16),
            jnp.asarray(np.stack(masks).astype(np.float32), F32), nlev)


_NT = (((1,), (1,)), ((), ()))
_TN = (((0,), (0,)), ((), ()))


def _gla_body(q_ref, k_ref, v_ref, misc_ref, wgu_ref, bg_ref, gn_ref, mst_ref, lmask_ref, s0_ref,
              o_ref, sout_ref, s_scr, la_scr, *, c, nc, nlev, valid_rows, bb):
    step = pl.program_id(1)
    tile = c * nc

    @pl.when(step == 0)
    def _():
        for bi in range(bb):
            for h in range(GLA_HEADS):
                s_scr[bi, h] = s0_ref[bi, h].T

    for bi in range(bb):
        x = jnp.dot(misc_ref[bi].astype(BF16), wgu_ref[...], preferred_element_type=F32) + bg_ref[...]
        log_a = (jnp.minimum(x, 0.0) - jnp.log1p(jnp.exp(-jnp.abs(x)))) * (1.0 / GLA_GATE_TAU)
        if valid_rows < tile:
            log_a = jnp.where(lax.broadcasted_iota(I32, log_a.shape, 0) < valid_rows, log_a, 0.0)
        la_scr[bi] = log_a
    scale = GLA_HEAD_DK ** -0.5

    def chunk(ci, carry):
        r0 = pl.multiple_of(ci * c, c)
        rows = pl.ds(r0, c)
        chains = [(bi, h) for bi in range(bb) for h in range(GLA_HEADS)]
        cs = []
        for bi in range(bb):
            g_all = la_scr[bi, rows, :]
            g_hi = g_all.astype(BF16)
            g_lo = (g_all - g_hi.astype(F32)).astype(BF16)
            t = jnp.dot(mst_ref[...], jnp.concatenate([g_hi, g_lo], axis=1), preferred_element_type=F32)
            cs.append(t[:, :GLA_DK] + t[:, GLA_DK:])
        ksl = lambda h: slice(h * GLA_HEAD_DK, (h + 1) * GLA_HEAD_DK)
        vsl = lambda h: slice(h * GLA_HEAD_DV, (h + 1) * GLA_HEAD_DV)
        qs, ks, bs, els, o_inter, att = [], [], [], [], [], []
        for bi, h in chains:
            q = q_ref[bi, rows, ksl(h)] * scale
            k = k_ref[bi, rows, ksl(h)]
            b = cs[bi][0:c, ksl(h)]
            qs.append(q), ks.append(k), bs.append(b), els.append(cs[bi][c:2 * c, ksl(h)])
            o_inter.append(lax.dot_general((q * jnp.exp(b)).astype(BF16), s_scr[bi, h].astype(BF16), _NT,
                                           preferred_element_type=F32))
            a = lmask_ref[nlev] * lax.dot_general(q.astype(BF16), k.astype(BF16), _NT,
                                                  preferred_element_type=F32)
            for l in range(nlev):
                e = cs[bi][(2 + l) * c:(3 + l) * c, ksl(h)]
                ql = (q * jnp.exp(jnp.minimum(b - e, 0.0))).astype(BF16)
                kl = (k * jnp.exp(jnp.minimum(e - b, 0.0))).astype(BF16)
                a = a + lmask_ref[l] * lax.dot_general(ql, kl, _NT, preferred_element_type=F32)
            att.append(a)
        vs = [v_ref[bi, rows, vsl(h)].astype(BF16) for bi, h in chains]
        outs = [o_inter[n] + jnp.dot(att[n].astype(BF16), vs[n], preferred_element_type=F32)
                for n in range(len(chains))]
        for n, (bi, h) in enumerate(chains):
            k_st = (ks[n] * jnp.exp(els[n] - bs[n])).astype(BF16)
            s_scr[bi, h] = s_scr[bi, h] * jnp.exp(els[n][0:1, :]) + lax.dot_general(
                vs[n], k_st, _TN, preferred_element_type=F32)
            o_ref[bi, rows, vsl(h)] = _rms(outs[n], gn_ref[...])
        return carry

    lax.fori_loop(0, nc, chunk, 0)

    @pl.when(step == pl.num_programs(1) - 1)
    def _():
        for bi in range(bb):
            for h in range(GLA_HEADS):
                sout_ref[bi, h] = s_scr[bi, h].T


def _gla(q_arr, k_arr, v_arr, misc_arr, cols, wgu_pad, b_gate, gla_norm_g, s0, c, tile, valid_rows, bb):
    nb, l = q_arr.shape[0], q_arr.shape[1]
    assert nb % bb == 0 and l % tile == 0
    mst, lmask, nlev = _gla_consts(c)
    const2 = lambda b, s: (0, 0)
    state_spec = pl.BlockSpec((bb, GLA_HEADS, GLA_HEAD_DK, GLA_HEAD_DV), lambda b, s: (b, 0, 0, 0))
    body = functools.partial(_gla_body, c=c, nc=tile // c, nlev=nlev, valid_rows=valid_rows, bb=bb)
    return pl.pallas_call(
        body,
        grid=(nb // bb, l // tile),
        in_specs=[pl.BlockSpec((bb, tile, GLA_DK), lambda b, s: (b, s, cols[0])),
                  pl.BlockSpec((bb, tile, GLA_DK), lambda b, s: (b, s, cols[1])),
                  pl.BlockSpec((bb, tile, GLA_DV), lambda b, s: (b, s, cols[2])),
                  pl.BlockSpec((bb, tile, LANES), lambda b, s: (b, s, 0)),
                  pl.BlockSpec((LANES, GLA_DK), const2),
                  pl.BlockSpec((1, GLA_DK), const2),
                  pl.BlockSpec((1, GLA_HEAD_DV), const2),
                  pl.BlockSpec(mst.shape, const2),
                  pl.BlockSpec(lmask.shape, lambda b, s: (0, 0, 0)),
                  state_spec],
        out_specs=[pl.BlockSpec((bb, tile, GLA_DV), lambda b, s: (b, s, 0)), state_spec],
        out_shape=[jax.ShapeDtypeStruct((nb, l, GLA_DV), F32),
                   jax.ShapeDtypeStruct((nb, GLA_HEADS, GLA_HEAD_DK, GLA_HEAD_DV), F32)],
        scratch_shapes=[pltpu.VMEM((bb, GLA_HEADS, GLA_HEAD_DV, GLA_HEAD_DK), F32),
                        pltpu.VMEM((bb, tile, GLA_DK), F32)],
        compiler_params=pltpu.CompilerParams(dimension_semantics=("arbitrary", "arbitrary"),
                                             vmem_limit_bytes=VMEM_LIMIT),
        name="gla",
    )(q_arr, k_arr, v_arr, misc_arr, wgu_pad, b_gate.reshape(1, -1), gla_norm_g.reshape(1, -1),
      mst, lmask, s0)


def _score_stats_init(w):
    inf = jnp.full((SUBLANES, w), jnp.inf, F32)
    zero = jnp.zeros((SUBLANES, w), I32)
    return -inf, inf, zero, zero


def _score_stats_update(carry, blk, finite=False):
    mx, mn, c_pos, c_nn = carry
    b3 = blk.reshape(blk.shape[0] // SUBLANES, SUBLANES, blk.shape[1])
    lows = b3 if finite else jnp.where(b3 == -jnp.inf, jnp.inf, b3)
    return (jnp.maximum(mx, jnp.max(b3, axis=0)),
            jnp.minimum(mn, jnp.min(lows, axis=0)),
            c_pos + jnp.sum((b3 > 0.0).astype(I32), axis=0),
            c_nn + jnp.sum((b3 >= 0.0).astype(I32), axis=0))


def _select_threshold(sc_ref, tri_ref, nt, tr, topk, small, lane_ok, stats=None):
    w = sc_ref.shape[1]
    inf = jnp.float32(jnp.inf)

    def over_tiles(fn, init):
        def single(i, carry):
            r0 = pl.multiple_of(i * tr, tr)
            return fn(carry, sc_ref[pl.ds(r0, tr), :], r0)

        def pair(i, carry):
            return single(2 * i + 1, single(2 * i, carry))

        return lax.fori_loop(2 * (nt // 2), nt, single, lax.fori_loop(0, nt // 2, pair, init))

    fold = lambda x: x.reshape(tr // SUBLANES, SUBLANES, w)
    zeros8 = jnp.zeros((SUBLANES, w), I32)
    pinf8 = jnp.full((SUBLANES, w), inf, F32)

    def count(pred):
        acc = over_tiles(lambda a, blk, r0: a + jnp.sum(fold(pred(blk, r0).astype(I32)), axis=0), zeros8)
        return jnp.sum(acc, axis=0, keepdims=True)

    def min_where(pred):
        acc = over_tiles(lambda a, blk, r0: jnp.minimum(
            a, jnp.min(fold(jnp.where(pred(blk, r0), blk, inf)), axis=0)), pinf8)
        return jnp.min(acc, axis=0, keepdims=True)

    if stats is None:
        stats = over_tiles(lambda carry, blk, r0: _score_stats_update(carry, blk), _score_stats_init(w))
    mx8, mn8, cp8, cn8 = stats
    c_pos = jnp.sum(cp8, axis=0, keepdims=True)
    c_nn = jnp.sum(cn8, axis=0, keepdims=True)
    hi = jnp.where(c_pos >= topk, jnp.max(mx8, axis=0, keepdims=True), 0.0)
    lo = jnp.where(c_nn >= topk, 0.0, jnp.min(mn8, axis=0, keepdims=True))

    live = lane_ok & jnp.logical_not(small)

    def bisect(_, carry):
        lo, hi = carry
        mid = 0.5 * lo + 0.5 * hi
        ge = count(lambda blk, r0: blk >= mid) >= topk
        return jnp.where(ge, mid, lo), jnp.where(ge, hi, mid)

    lo, hi = lax.fori_loop(0, BISECT_STEPS, bisect, (lo, hi))
    v0 = min_where(lambda blk, r0: blk >= lo)

    def gt_next(v):
        def f(carry, blk, r0):
            cg, nx = carry
            gt = blk > v
            return (cg + jnp.sum(fold(gt.astype(I32)), axis=0),
                    jnp.minimum(nx, jnp.min(fold(jnp.where(gt, blk, inf)), axis=0)))
        cg8, nx8 = over_tiles(f, (zeros8, pinf8))
        return jnp.sum(cg8, axis=0, keepdims=True), jnp.min(nx8, axis=0, keepdims=True)

    def peel(state):
        v, _, _ = state
        cg, nx = gt_next(v)
        move = (cg >= topk) & live
        return jnp.where(move, nx, v), cg, jnp.max(move.astype(I32))

    v, cnt_gt, _ = lax.while_loop(lambda s: s[2] > 0, peel,
                                  (v0, jnp.zeros((1, w), I32), jnp.int32(1)))

    need = jnp.where(live, (topk - cnt_gt).astype(F32), jnp.float32(2 ** 30))
    tb = tri_ref.shape[0]

    def drop_surplus(rows_per_step):
        def step(i, seen):
            r0 = pl.multiple_of(i * rows_per_step, rows_per_step)
            blk = sc_ref[pl.ds(r0, rows_per_step), :]
            eq = blk == v
            ones = jnp.where(eq, 1.0, 0.0).astype(BF16)
            ranks = [jnp.dot(tri_ref[...], ones[s * tb:(s + 1) * tb, :], preferred_element_type=F32)
                     for s in range(rows_per_step // tb)]
            for s in range(rows_per_step // tb):
                rows = slice(s * tb, (s + 1) * tb)
                rank = ranks[s] + seen
                sc_ref[pl.ds(r0 + s * tb, tb), :] = jnp.where(
                    eq[rows], jnp.where(rank > need, -inf, blk[rows]), blk[rows])
                seen = rank[tb - 1:tb, :]
            return seen
        return step

    seen = lax.fori_loop(0, nt // 2, drop_surplus(2 * tr), jnp.zeros((1, w), F32))
    lax.fori_loop(2 * (nt // 2), nt, drop_surplus(tr), seen)

    return jnp.where(small, -inf, v)


def _dsa_prompt_body(relb_ref, btab_ref, tri_ref, ki_ref, qi_ref, wi_ref, kd_ref, vt_ref, qd_ref, o_ref,
                     sc_scr, tbl_scr, q2_scr, acc_scr, m_scr, o_scr, lg_scr, pr_scr, *,
                     topk, far_bucket):
    b = pl.program_id(0)
    j = pl.program_id(1)
    tk = DSA_TK
    hd = DSA_HEAD_DIM

    @pl.when((b == 0) & (j == 0))
    def _():
        q2_scr[...] = jnp.zeros(q2_scr.shape, BF16)

        def build(ci, carry):
            r0 = pl.multiple_of(ci * LANES, LANES)
            bt = btab_ref[pl.ds(r0, LANES), :]
            for h in range(DSA_HEADS):
                t = jnp.zeros(bt.shape, F32)
                for bk in range(REL_BUCKETS):
                    t = jnp.where(bt == bk, (relb_ref[bk, h] - relb_ref[far_bucket, h]) * LOG2E, t)
                tbl_scr[h, pl.ds(r0, LANES), :] = t
            return carry

        lax.fori_loop(0, BIAS_TAB_ROWS // LANES, build, 0)

    nt = j // (tk // QUERY_BLOCK) + 1
    qpos = j * QUERY_BLOCK + lax.broadcasted_iota(I32, (1, LANES), 1)
    for p in range(DSA_HEADS // 2):
        q2_scr[p, 0:hd, 0:LANES] = qd_ref[2 * p * hd:(2 * p + 1) * hd, :]
        q2_scr[p, hd:2 * hd, LANES:2 * LANES] = qd_ref[(2 * p + 1) * hd:(2 * p + 2) * hd, :]

    wi = wi_ref[...]

    def score_tile(i, carry, last):
        r0 = pl.multiple_of(i * tk, tk)
        s4 = jnp.dot(ki_ref[pl.ds(r0, tk), :], qi_ref[...], preferred_element_type=F32)
        sc = jnp.zeros((tk, LANES), F32)
        for h in range(IDX_HEADS):
            sc = sc + jnp.maximum(s4[:, h * LANES:(h + 1) * LANES], 0.0) * wi[h:h + 1, :]
        if last:
            kpos = r0 + lax.broadcasted_iota(I32, (tk, LANES), 0)
            sc = jnp.where(kpos <= qpos, sc, -jnp.inf)
        sc_scr[pl.ds(r0, tk), :] = sc
        return _score_stats_update(carry, sc, finite=not last)

    stats = lax.fori_loop(0, nt - 1, lambda i, c: score_tile(i, c, False), _score_stats_init(LANES))
    stats = score_tile(nt - 1, stats, True)

    small = (qpos + 1) < topk
    kstar = _select_threshold(sc_scr, tri_ref, nt, tk, topk, small, jnp.full((1, LANES), True), stats)

    m_scr[...] = jnp.full(m_scr.shape, NEG, F32)
    acc_scr[...] = jnp.zeros(acc_scr.shape, F32)
    vrows = acc_scr.shape[0] // DSA_HEADS

    def logits_stage(i, buf, near):
        r0 = pl.multiple_of(i * tk, tk)
        blk = sc_scr[pl.ds(r0, tk), :]
        if near:
            kpos = r0 + lax.broadcasted_iota(I32, (tk, LANES), 0)
            addm = jnp.where(blk >= kstar, jnp.where(kpos <= qpos, 0.0, NEG), NEG)
            off = pl.multiple_of(i * tk - j * QUERY_BLOCK + BIAS_TAB_OFF, LANES)
        else:
            addm = jnp.where(blk >= kstar, 0.0, NEG)
        tile_max = []
        for p in range(DSA_HEADS // 2):
            lg2 = jnp.dot(kd_ref[pl.ds(r0, tk), p * LANES:(p + 1) * LANES], q2_scr[p],
                          preferred_element_type=F32)
            for hh in range(2):
                h = 2 * p + hh
                lg = lg2[:, hh * LANES:(hh + 1) * LANES] + addm
                if near:
                    lg = lg + tbl_scr[h, pl.ds(off, tk), :]
                lg_scr[buf, h] = lg
                tile_max.append(jnp.max(lg, axis=0, keepdims=True))
        return tuple(tile_max)

    def softmax_pv_stage(i, buf, tile_max):
        alpha = []
        for h in range(DSA_HEADS):
            m_old = m_scr[h:h + 1, :]
            m_new = jnp.maximum(m_old, tile_max[h])
            m_scr[h:h + 1, :] = m_new
            alpha.append(jnp.exp2(m_old - m_new))
            pr_scr[h] = jnp.exp2((lg_scr[buf, h] - m_new).astype(BF16))
        for h in range(DSA_HEADS):
            rows = slice(h * vrows, (h + 1) * vrows)
            acc_scr[rows, :] = alpha[h] * acc_scr[rows, :] + jnp.dot(
                vt_ref[i, rows, :], pr_scr[h], preferred_element_type=F32)

    near_tiles = jnp.where(j % (tk // QUERY_BLOCK) == 0, 2, 1)
    n_far = jnp.maximum(nt - near_tiles, 0)

    @pl.when(n_far > 0)
    def _():
        n_pairs = (n_far - 1) // 2

        def pair(it, tile_max):
            i0 = 2 * it
            max1 = logits_stage(i0 + 1, 1, False)
            softmax_pv_stage(i0, 0, tile_max)
            max2 = logits_stage(i0 + 2, 0, False)
            softmax_pv_stage(i0 + 1, 1, max1)
            return max2

        tile_max = lax.fori_loop(0, n_pairs, pair, logits_stage(0, 0, False))
        last = 2 * n_pairs

        @pl.when(n_far - last == 1)
        def _():
            softmax_pv_stage(last, 0, tile_max)

        @pl.when(n_far - last == 2)
        def _():
            max1 = logits_stage(last + 1, 1, False)
            softmax_pv_stage(last, 0, tile_max)
            softmax_pv_stage(last + 1, 1, max1)

    def near_tile(i, carry):
        softmax_pv_stage(i, 0, logits_stage(i, 0, True))
        return carry

    lax.fori_loop(n_far, nt, near_tile, 0)

    for h in range(DSA_HEADS):
        o_scr[h * hd:(h + 1) * hd, :] = (acc_scr[h * vrows:h * vrows + hd, :]
                                         / acc_scr[h * vrows + hd:h * vrows + hd + 1, :])
    o_ref[...] = o_scr[...].T.astype(BF16)


def _tri_ones(n):
    return jnp.asarray(np.tril(np.ones((n, n), np.float32)), BF16)


def _bias_bucket_table():
    u = np.arange(BIAS_TAB_ROWS)[:, None]
    r = np.arange(LANES)[None, :]
    return jnp.asarray(_t5_bucket_np(r + BIAS_TAB_OFF - u), I32)


def _dsa_prompt(qd, kd_bf, vt, misc, kin, rel_bias, nb, l):
    nq = l // QUERY_BLOCK
    topk = min(TOPK_MAX, l // 4)
    assert l % DSA_TK == 0
    far = _t5_bucket_np(np.arange(REL_MAX_DIST, max(l, REL_MAX_DIST + 1)))
    assert (far == far[0]).all()
    ki = kin.reshape(nb, l, IDX_DIM).astype(BF16)
    qi = misc[:, MISC_QI:MISC_QI + IDX_HEADS * IDX_DIM].reshape(nb, nq, QUERY_BLOCK, IDX_HEADS, IDX_DIM)
    qi = qi.transpose(0, 4, 1, 3, 2).reshape(nb, IDX_DIM, nq * IDX_HEADS * QUERY_BLOCK).astype(BF16)
    wi = misc[:, MISC_WI:MISC_WI + IDX_HEADS] * ((IDX_DIM ** -0.5) * (IDX_HEADS ** -0.5))
    wi = wi.reshape(nb, nq, QUERY_BLOCK, IDX_HEADS).transpose(0, 1, 3, 2)
    wi = jnp.pad(wi, ((0, 0), (0, 0), (0, SUBLANES - IDX_HEADS), (0, 0))).reshape(nb, nq * SUBLANES, QUERY_BLOCK)
    kd_bf = kd_bf.reshape(nb, l, DSA_WIDTH)
    body = functools.partial(_dsa_prompt_body, topk=topk, far_bucket=int(far[0]))
    whole = lambda b, j: (b, 0, 0)
    return pl.pallas_call(
        body,
        grid=(nb, nq),
        in_specs=[pl.BlockSpec(memory_space=pltpu.SMEM),
                  pl.BlockSpec((BIAS_TAB_ROWS, LANES), lambda b, j: (0, 0), pipeline_mode=pl.Buffered(1)),
                  pl.BlockSpec((LANES, LANES), lambda b, j: (0, 0), pipeline_mode=pl.Buffered(1)),
                  pl.BlockSpec((None, l, IDX_DIM), whole, pipeline_mode=pl.Buffered(1)),
                  pl.BlockSpec((None, IDX_DIM, IDX_HEADS * QUERY_BLOCK), lambda b, j: (b, 0, j)),
                  pl.BlockSpec((None, SUBLANES, QUERY_BLOCK), lambda b, j: (b, j, 0)),
                  pl.BlockSpec((None, l, DSA_WIDTH), whole, pipeline_mode=pl.Buffered(1)),
                  pl.BlockSpec((None, l // DSA_TK, DSA_HEADS * V_ROWS, DSA_TK), lambda b, j: (b, 0, 0, 0),
                               pipeline_mode=pl.Buffered(1)),
                  pl.BlockSpec((None, DSA_WIDTH, QUERY_BLOCK), lambda b, j: (b, 0, j))],
        out_specs=pl.BlockSpec((None, QUERY_BLOCK, DSA_WIDTH), lambda b, j: (b, j, 0)),
        out_shape=jax.ShapeDtypeStruct((nb, l, DSA_WIDTH), BF16),
        scratch_shapes=[pltpu.VMEM((l, LANES), F32),
                        pltpu.VMEM((DSA_HEADS, BIAS_TAB_ROWS, LANES), F32),
                        pltpu.VMEM((DSA_HEADS // 2, LANES, 2 * LANES), BF16),
                        pltpu.VMEM((DSA_HEADS * V_ROWS, LANES), F32),
                        pltpu.VMEM((DSA_HEADS, LANES), F32),
                        pltpu.VMEM((DSA_WIDTH, LANES), F32),
                        pltpu.VMEM((2, DSA_HEADS, DSA_TK, LANES), F32),
                        pltpu.VMEM((DSA_HEADS, DSA_TK, LANES), BF16)],
        compiler_params=pltpu.CompilerParams(dimension_semantics=("arbitrary", "arbitrary"),
                                             vmem_limit_bytes=VMEM_LIMIT),
        name="dsa_prompt",
    )(rel_bias, _bias_bucket_table(), _tri_ones(LANES), ki, qi, wi, kd_bf, vt, qd)


def _select_threshold_lanes(sc_ref, triu_ref, nt, topk, small, row_ok):
    r = sc_ref.shape[0]
    inf = jnp.float32(jnp.inf)
    tiles = [slice(i * LANES, (i + 1) * LANES) for i in range(nt)]
    rowsum = lambda x: jnp.sum(x, axis=1, keepdims=True)
    rowmin = lambda x: jnp.min(x, axis=1, keepdims=True)

    def count(pred):
        acc = jnp.zeros((r, LANES), I32)
        for t in tiles:
            acc = acc + pred(sc_ref[:, t]).astype(I32)
        return rowsum(acc)

    def min_where(pred):
        acc = jnp.full((r, LANES), inf, F32)
        for t in tiles:
            blk = sc_ref[:, t]
            acc = jnp.minimum(acc, jnp.where(pred(blk), blk, inf))
        return rowmin(acc)

    mx = jnp.full((r, LANES), -inf, F32)
    mn = jnp.full((r, LANES), inf, F32)
    c_pos = jnp.zeros((r, LANES), I32)
    c_nn = jnp.zeros((r, LANES), I32)
    for t in tiles:
        blk = sc_ref[:, t]
        mx = jnp.maximum(mx, blk)
        mn = jnp.minimum(mn, jnp.where(blk == -inf, inf, blk))
        c_pos = c_pos + (blk > 0.0).astype(I32)
        c_nn = c_nn + (blk >= 0.0).astype(I32)
    hi = jnp.where(rowsum(c_pos) >= topk, jnp.max(mx, axis=1, keepdims=True), 0.0)
    lo = jnp.where(rowsum(c_nn) >= topk, 0.0, rowmin(mn))
    live = row_ok & jnp.logical_not(small)

    def bisect(_, carry):
        lo, hi = carry
        mid = 0.5 * lo + 0.5 * hi
        ge = count(lambda blk: blk >= mid) >= topk
        return jnp.where(ge, mid, lo), jnp.where(ge, hi, mid)

    lo, hi = lax.fori_loop(0, BISECT_STEPS, bisect, (lo, hi))
    v0 = min_where(lambda blk: blk >= lo)

    def peel(state):
        v, _, _ = state
        cg = jnp.zeros((r, LANES), I32)
        nx = jnp.full((r, LANES), inf, F32)
        for t in tiles:
            blk = sc_ref[:, t]
            gt = blk > v
            cg = cg + gt.astype(I32)
            nx = jnp.minimum(nx, jnp.where(gt, blk, inf))
        cg = rowsum(cg)
        move = (cg >= topk) & live
        return jnp.where(move, rowmin(nx), v), cg, jnp.max(move.astype(I32))

    v, cnt_gt, _ = lax.while_loop(lambda s: s[2] > 0, peel,
                                  (v0, jnp.zeros((r, 1), I32), jnp.int32(1)))
    need = jnp.where(live, (topk - cnt_gt).astype(F32), jnp.float32(2 ** 30))
    seen = jnp.zeros((r, 1), F32)
    for t in tiles:
        blk = sc_ref[:, t]
        eq = blk == v
        rank = seen + jnp.dot(jnp.where(eq, 1.0, 0.0).astype(BF16), triu_ref[...],
                              preferred_element_type=F32)
        sc_ref[:, t] = jnp.where(eq, jnp.where(rank > need, -inf, blk), blk)
        seen = rank[:, LANES - 1:LANES]

    return jnp.where(small, -inf, v)


SEQ_GROUP = LANES // SUBLANES
NEW_ROWS = 16


def _dsa_sample_select_body(pt_ref, *refs, n_pages, t_new, topk):
    page_refs = refs[:n_pages]
    knew_ref, qi_ref, wi_ref, triu_ref, mask_ref, sc_scr = refs[n_pages:]
    g = pl.program_id(1)
    past = n_pages * PAGE_SIZE
    rows = pl.ds(pl.multiple_of(g * SUBLANES, SUBLANES), SUBLANES)
    for p in range(n_pages + 1):
        keys_t = (page_refs[p][...] if p < n_pages else knew_ref[...]).astype(BF16)
        s = jnp.dot(qi_ref[...], keys_t, preferred_element_type=F32)
        sc = jnp.zeros((SUBLANES, PAGE_SIZE), F32)
        for h in range(IDX_HEADS):
            hs = slice(h * SUBLANES, (h + 1) * SUBLANES)
            sc = sc + jnp.maximum(s[hs, :], 0.0) * wi_ref[hs, :]
        sc_scr[rows, p * PAGE_SIZE:(p + 1) * PAGE_SIZE] = sc

    @pl.when(g == SEQ_GROUP - 1)
    def _():
        q_of_row = lax.broadcasted_iota(I32, (LANES, 1), 0) % SUBLANES
        new = slice(past, past + PAGE_SIZE)
        cpos = lax.broadcasted_iota(I32, (LANES, PAGE_SIZE), 1)
        sc_scr[:, new] = jnp.where(cpos <= q_of_row, sc_scr[:, new], -jnp.inf)
        row_ok = q_of_row < t_new
        small = (past + q_of_row + 1) < topk
        kstar = _select_threshold_lanes(sc_scr, triu_ref, n_pages + 1, topk, small, row_ok)
        for p in range(n_pages + 1):
            t = slice(p * PAGE_SIZE, (p + 1) * PAGE_SIZE)
            mask_ref[:, t] = jnp.where(sc_scr[:, t] >= kstar, 1.0, 0.0)


def _dsa_sample_select(page_table, kidx_t, kin_new_t, qi_rows, wi_rows, t_new, topk):
    db, n_pages = page_table.shape
    ng = db // SEQ_GROUP
    width = (n_pages + 1) * PAGE_SIZE
    seq = lambda gi, g, pt: (gi * SEQ_GROUP + g, 0, 0)
    page_specs = [pl.BlockSpec((None, IDX_DIM, PAGE_SIZE),
                               functools.partial(lambda gi, g, pt, p: (pt[gi * SEQ_GROUP + g, p], 0, 0), p=p))
                  for p in range(n_pages)]
    body = functools.partial(_dsa_sample_select_body, n_pages=n_pages, t_new=t_new, topk=topk)
    grid_spec = pltpu.PrefetchScalarGridSpec(
        num_scalar_prefetch=1,
        grid=(ng, SEQ_GROUP),
        in_specs=page_specs + [pl.BlockSpec((None, IDX_DIM, PAGE_SIZE), seq),
                               pl.BlockSpec((None, IDX_HEADS * SUBLANES, IDX_DIM), seq),
                               pl.BlockSpec((None, IDX_HEADS * SUBLANES, PAGE_SIZE), seq),
                               pl.BlockSpec((LANES, LANES), lambda gi, g, pt: (0, 0))],
        out_specs=pl.BlockSpec((None, LANES, width), lambda gi, g, pt: (gi, 0, 0)),
        scratch_shapes=[pltpu.VMEM((LANES, width), F32)],
    )
    triu = jnp.asarray(np.triu(np.ones((LANES, LANES), np.float32)), BF16)
    return pl.pallas_call(
        body, grid_spec=grid_spec,
        out_shape=jax.ShapeDtypeStruct((ng, LANES, width), F32),
        compiler_params=pltpu.CompilerParams(dimension_semantics=("arbitrary", "arbitrary"),
                                             vmem_limit_bytes=VMEM_LIMIT),
        name="dsa_sample_select",
    )(page_table, *([kidx_t] * n_pages), kin_new_t, qi_rows, wi_rows, triu)


def _dsa_sample_attend_body(pt_ref, *refs, n_pages):
    k_refs = refs[:n_pages]
    v_refs = refs[n_pages:2 * n_pages]
    (knew_ref, vnew_ref, mask_ref, qbd_ref, btab_ref, rb_ref, o_ref, lg_scr, tbl_scr) = refs[2 * n_pages:]
    b = pl.program_id(0)
    hq = DSA_HEADS * SUBLANES

    @pl.when(b == 0)
    def _():
        for half in range(2):
            cols = slice(half * PAGE_SIZE, (half + 1) * PAGE_SIZE)
            bt = btab_ref[:, cols]
            t = jnp.zeros(bt.shape, F32)
            for bk in range(REL_BUCKETS):
                t = jnp.where(bt == bk, rb_ref[bk], t)
            tbl_scr[:, cols] = t

    far_bias = rb_ref[REL_BUCKETS - 1]
    page = lambda ref: ref[...].reshape(DSA_WIDTH, PAGE_SIZE).astype(BF16)

    def masked(lg, sel):
        return lg + jnp.where(jnp.concatenate([sel] * DSA_HEADS, axis=0) > 0.5, 0.0, NEG)

    m = jnp.full((hq, LANES), NEG, F32)
    for p in range(n_pages):
        cols = slice(p * PAGE_SIZE, (p + 1) * PAGE_SIZE)
        lg = jnp.dot(qbd_ref[...], page(k_refs[p]), preferred_element_type=F32)
        lg = lg + (tbl_scr[:, 0:PAGE_SIZE] if p == n_pages - 1 else far_bias)
        lg = masked(lg, mask_ref[:, cols])
        lg_scr[:, cols] = lg
        m = jnp.maximum(m, lg)
    past = n_pages * PAGE_SIZE
    lg_new = lax.dot_general(qbd_ref[...], knew_ref[...].astype(BF16), _NT, preferred_element_type=F32)
    lg_new = masked(lg_new + tbl_scr[:, PAGE_SIZE:PAGE_SIZE + NEW_ROWS], mask_ref[:, past:past + NEW_ROWS])
    m = jnp.maximum(jnp.max(m, axis=1, keepdims=True), jnp.max(lg_new, axis=1, keepdims=True))

    pr_new = jnp.exp(lg_new - m)
    acc = jnp.dot(pr_new.astype(BF16), vnew_ref[...].astype(BF16), preferred_element_type=F32)
    lsum = jnp.zeros((hq, LANES), F32)
    for p in range(n_pages):
        cols = slice(p * PAGE_SIZE, (p + 1) * PAGE_SIZE)
        pr = jnp.exp(lg_scr[:, cols] - m)
        lsum = lsum + pr
        acc = acc + lax.dot_general(pr.astype(BF16), page(v_refs[p]), _NT, preferred_element_type=F32)
    lsum = jnp.sum(lsum, axis=1, keepdims=True) + jnp.sum(pr_new, axis=1, keepdims=True)
    acc = acc / lsum
    head_of_lane = lax.broadcasted_iota(I32, (SUBLANES, DSA_WIDTH), 1) // DSA_HEAD_DIM
    out = jnp.zeros((SUBLANES, DSA_WIDTH), F32)
    for h in range(DSA_HEADS):
        out = jnp.where(head_of_lane == h, acc[h * SUBLANES:(h + 1) * SUBLANES, :], out)
    o_ref[...] = out


def _dsa_sample_attend(page_table, k_t, v_t, k_new, v_new, mask, qbd, rel_bias):
    db, n_pages = page_table.shape
    hq = DSA_HEADS * SUBLANES
    width = (n_pages + 1) * PAGE_SIZE
    seq = lambda b, pt: (b, 0, 0)
    page = lambda p: functools.partial(lambda b, pt, p: (pt[b, p], 0, 0, 0), p=p)
    kv_spec = lambda p: pl.BlockSpec((None, DSA_HEADS, DSA_HEAD_DIM, PAGE_SIZE), page(p))
    u = np.arange(2 * PAGE_SIZE)[None, :]
    q = (np.arange(hq) % SUBLANES)[:, None]
    btab = jnp.asarray(_t5_bucket_np(PAGE_SIZE + q - u), I32)
    rb = jnp.broadcast_to(jnp.repeat(rel_bias, SUBLANES, axis=1)[:, :, None], (REL_BUCKETS, hq, PAGE_SIZE))
    grid_spec = pltpu.PrefetchScalarGridSpec(
        num_scalar_prefetch=1,
        grid=(db,),
        in_specs=[kv_spec(p) for p in range(n_pages)] + [kv_spec(p) for p in range(n_pages)] + [
            pl.BlockSpec((None, NEW_ROWS, DSA_WIDTH), seq),
            pl.BlockSpec((None, NEW_ROWS, DSA_WIDTH), seq),
            pl.BlockSpec((None, SUBLANES, width), lambda b, pt: (b // SEQ_GROUP, b % SEQ_GROUP, 0)),
            pl.BlockSpec((None, hq, DSA_WIDTH), seq),
            pl.BlockSpec(btab.shape, lambda b, pt: (0, 0)),
            pl.BlockSpec(rb.shape, lambda b, pt: (0, 0, 0))],
        out_specs=pl.BlockSpec((None, SUBLANES, DSA_WIDTH), seq),
        scratch_shapes=[pltpu.VMEM((hq, width - PAGE_SIZE), F32), pltpu.VMEM((hq, 2 * PAGE_SIZE), F32)],
    )
    return pl.pallas_call(
        functools.partial(_dsa_sample_attend_body, n_pages=n_pages), grid_spec=grid_spec,
        out_shape=jax.ShapeDtypeStruct((db, SUBLANES, DSA_WIDTH), F32),
        compiler_params=pltpu.CompilerParams(dimension_semantics=("arbitrary",),
                                             vmem_limit_bytes=VMEM_LIMIT),
        name="dsa_sample_attend",
    )(page_table, *([k_t] * n_pages), *([v_t] * n_pages), k_new, v_new, mask, qbd, btab, rb)


def _dsa_sample(zA, kd, vd, misc, kin, rel_bias, cache_k, cache_v, cache_kidx, page_table, db, t_new):
    n_pages = page_table.shape[1]
    past = n_pages * PAGE_SIZE
    topk = min(TOPK_MAX, (past + t_new) // 4)
    assert db % SEQ_GROUP == 0 and t_new <= SUBLANES
    new_t = lambda a, w: jnp.pad(a.reshape(db, t_new, w).transpose(0, 2, 1),
                                 ((0, 0), (0, 0), (0, PAGE_SIZE - t_new)))
    pad_q = lambda a: jnp.pad(a, ((0, 0), (0, 0), (0, SUBLANES - t_new), (0, 0)))
    new_rows = lambda a: jnp.pad(a.reshape(db, t_new, DSA_WIDTH), ((0, 0), (0, NEW_ROWS - t_new), (0, 0)))
    qi = misc[:, MISC_QI:MISC_QI + IDX_HEADS * IDX_DIM].reshape(db, t_new, IDX_HEADS, IDX_DIM)
    qi_rows = pad_q(qi.transpose(0, 2, 1, 3)).reshape(db, IDX_HEADS * SUBLANES, IDX_DIM).astype(BF16)
    wi = misc[:, MISC_WI:MISC_WI + IDX_HEADS] * ((IDX_DIM ** -0.5) * (IDX_HEADS ** -0.5))
    wi = pad_q(wi.reshape(db, t_new, IDX_HEADS, 1).transpose(0, 2, 1, 3)).reshape(db, IDX_HEADS * SUBLANES, 1)
    wi_rows = jnp.broadcast_to(wi, (db, IDX_HEADS * SUBLANES, PAGE_SIZE))
    mask = _dsa_sample_select(page_table, cache_kidx.transpose(0, 2, 1), new_t(kin, IDX_DIM),
                              qi_rows, wi_rows, t_new, topk)
    qd = (zA[:, ZA_QD:ZA_QD + DSA_WIDTH] * (DSA_HEAD_DIM ** -0.5)).reshape(db, t_new, DSA_HEADS, DSA_HEAD_DIM)
    qd = pad_q(qd.transpose(0, 2, 1, 3))
    eye = jnp.eye(DSA_HEADS, dtype=qd.dtype)
    qbd = (qd[:, :, :, None, :] * eye[None, :, None, :, None]).reshape(db, DSA_HEADS * SUBLANES, DSA_WIDTH)
    o = _dsa_sample_attend(page_table, cache_k.transpose(0, 2, 3, 1), cache_v.transpose(0, 2, 3, 1),
                           new_rows(kd), new_rows(vd), mask, qbd.astype(BF16), rel_bias)
    return o[:, :t_new, :].reshape(db * t_new, DSA_WIDTH).astype(BF16)


def _post_body(x_ref, og_ref, rg_ref, gg_ref, gd_ref, od_ref, wg_ref, wd_ref, wo_ref, gf_ref,
               x1_ref, hf_ref):
    half = x_ref.shape[0] // 2
    parts = [slice(0, half), slice(half, 2 * half)]
    branch = []
    for r in parts:
        rg = rg_ref[r, :]
        a = (og_ref[r, :] * (rg * _sigmoid(rg))).astype(BF16)
        branch.append((jnp.dot(a, wg_ref[...], preferred_element_type=F32),
                       jnp.dot(od_ref[r, :], wd_ref[...], preferred_element_type=F32)))
    for r, (y_g, y_d) in zip(parts, branch):
        mix = (_sigmoid(gg_ref[r, :]) * y_g + _sigmoid(gd_ref[r, :]) * y_d).astype(BF16)
        x1 = x_ref[r, :] + jnp.dot(mix, wo_ref[...], preferred_element_type=F32)
        x1_ref[r, :] = x1
        hf_ref[r, :] = _rms(x1, gf_ref[...]).astype(BF16)


def _post(x2d, og, zA, od, w_gla, w_dsa, w_o, g_ffn):
    n = x2d.shape[0]
    tm = min(n, 512)
    row = lambda i: (i, 0)
    const = lambda i: (0, 0)
    return pl.pallas_call(
        _post_body,
        grid=(n // tm,),
        in_specs=[pl.BlockSpec((tm, D_MODEL), row),
                  pl.BlockSpec((tm, GLA_DV), row),
                  pl.BlockSpec((tm, GLA_DV), lambda i: (i, COL_RG)),
                  pl.BlockSpec((tm, D_MODEL), lambda i: (i, COL_GG)),
                  pl.BlockSpec((tm, D_MODEL), lambda i: (i, COL_GD)),
                  pl.BlockSpec((tm, DSA_WIDTH), row),
                  pl.BlockSpec((GLA_DV, D_MODEL), const),
                  pl.BlockSpec((DSA_WIDTH, D_MODEL), const),
                  pl.BlockSpec((D_MODEL, D_MODEL), const),
                  pl.BlockSpec((1, D_MODEL), const)],
        out_specs=[pl.BlockSpec((tm, D_MODEL), row), pl.BlockSpec((tm, D_MODEL), row)],
        out_shape=[jax.ShapeDtypeStruct((n, D_MODEL), F32), jax.ShapeDtypeStruct((n, D_MODEL), BF16)],
        compiler_params=pltpu.CompilerParams(dimension_semantics=("arbitrary",),
                                             vmem_limit_bytes=VMEM_LIMIT),
        name="post_mix",
    )(x2d, og, zA, zA, zA, od, w_gla, w_dsa, w_o, g_ffn.reshape(1, -1))


FFN_TILE = 256
FFN_ROWS = 512


def _ffn_body(hf_ref, x1_ref, wg_ref, wu_ref, wd_ref, gfin_ref, y_ref, acc_scr):
    hf = hf_ref[...]
    n_chunks = wg_ref.shape[1] // FFN_TILE
    cols = lambda k: slice(k * FFN_TILE, (k + 1) * FFN_TILE)

    def gate_up(k):
        return (jnp.dot(hf, wg_ref[:, cols(k)], preferred_element_type=F32),
                jnp.dot(hf, wu_ref[:, cols(k)], preferred_element_type=F32))

    acc_scr[...] = x1_ref[...]
    nxt = gate_up(0)
    for k in range(n_chunks):
        gate, up = nxt
        if k + 1 < n_chunks:
            nxt = gate_up(k + 1)
        act = (gate * _sigmoid(gate) * up).astype(BF16)
        acc_scr[...] = acc_scr[...] + jnp.dot(act, wd_ref[cols(k), :], preferred_element_type=F32)
    y_ref[...] = _rms(acc_scr[...], gfin_ref[...])


def _ffn(hf, x1, w_gate, w_up, w_down, g_final):
    n = hf.shape[0]
    d_ff = w_gate.shape[1]
    assert d_ff % FFN_TILE == 0
    tm = min(n, FFN_ROWS)
    row = lambda i: (i, 0)
    const = lambda i: (0, 0)
    resident = lambda shape: pl.BlockSpec(shape, const, pipeline_mode=pl.Buffered(1))
    return pl.pallas_call(
        _ffn_body,
        grid=(n // tm,),
        in_specs=[pl.BlockSpec((tm, D_MODEL), row),
                  pl.BlockSpec((tm, D_MODEL), row),
                  resident((D_MODEL, d_ff)),
                  resident((D_MODEL, d_ff)),
                  resident((d_ff, D_MODEL)),
                  pl.BlockSpec((1, D_MODEL), const)],
        out_specs=pl.BlockSpec((tm, D_MODEL), row),
        out_shape=jax.ShapeDtypeStruct((n, D_MODEL), F32),
        scratch_shapes=[pltpu.VMEM((tm, D_MODEL), F32)],
        compiler_params=pltpu.CompilerParams(dimension_semantics=("arbitrary",),
                                             vmem_limit_bytes=VMEM_LIMIT),
        name="ffn",
    )(hf, x1, w_gate, w_up, w_down, g_final.reshape(1, -1))


def _gate_weight_pad(w_gate_up):
    pad = jnp.zeros((LANES, GLA_DK), w_gate_up.dtype)
    return pad.at[MISC_ALOW:MISC_ALOW + GLA_GATE_RANK].set(w_gate_up).astype(BF16)


def kernel(x_prompt, x_sample, cache_k, cache_v, cache_kidx, state_gla, page_table, g_mix, w_in,
           w_gate_up, b_gate, gla_norm_g, w_gla_branch, idx_k_g, idx_k_b, w_dsa_branch, w_o, g_ffn,
           w_ffn_gate, w_ffn_up, w_ffn_down, rel_bias, g_final):
    depth = w_in.shape[0]
    assert depth == 1, "the final RMSNorm is fused into the FFN kernel of the single layer"
    nb, l, _ = x_prompt.shape
    db, t_new, _ = x_sample.shape
    layer = 0
    w_parts = _split_w_in(w_in[layer])
    wgu_pad = _gate_weight_pad(w_gate_up[layer])
    w_gla = w_gla_branch[layer].astype(BF16)
    w_dsa = w_dsa_branch[layer].astype(BF16)
    w_out = w_o[layer].astype(BF16)
    w_fg, w_fu, w_fd = (w.astype(BF16) for w in (w_ffn_gate[layer], w_ffn_up[layer], w_ffn_down[layer]))

    xp = x_prompt.reshape(nb * l, D_MODEL)
    zA, qdt, kdb, kt, vt, vslab, misc, kin = _in_proj_prompt(xp, g_mix[layer], w_parts, idx_k_g[layer],
                                                             idx_k_b[layer], nb, l)
    zA3 = zA.reshape(nb, l, zA.shape[1])
    s0 = jnp.zeros((nb, GLA_HEADS, GLA_HEAD_DK, GLA_HEAD_DV), state_gla.dtype)
    og, s_p = _gla(zA3, zA3, zA3, misc.reshape(nb, l, PROJ_TILE), (COL_QG, COL_KG, COL_VG), wgu_pad,
                   b_gate[layer], gla_norm_g[layer], s0, GLA_CHUNK, GLA_TILE, GLA_TILE, nb)
    od = _dsa_prompt(qdt, kdb, vslab, misc, kin, rel_bias, nb, l)
    x1, hf = _post(xp, og.reshape(nb * l, GLA_DV), zA, od.reshape(nb * l, DSA_WIDTH), w_gla, w_dsa,
                   w_out, g_ffn[layer])
    y_p = _ffn(hf, x1, w_fg, w_fu, w_fd, g_final).reshape(nb, l, D_MODEL)

    xs = x_sample.reshape(db * t_new, D_MODEL)
    zA_s, kd_s, vd_s, misc_s, kin_s = _in_proj(xs, g_mix[layer], w_parts, idx_k_g[layer], idx_k_b[layer])
    pad_t = lambda a: jnp.pad(a.reshape(db, t_new, -1), ((0, 0), (0, SAMPLE_CHUNK - t_new), (0, 0)))
    og_s, s_s = _gla(pad_t(zA_s[:, 0:GLA_DK]), pad_t(zA_s[:, GLA_DK:2 * GLA_DK]),
                     pad_t(zA_s[:, 2 * GLA_DK:2 * GLA_DK + GLA_DV]), pad_t(misc_s[:, 0:LANES]), (0, 0, 0),
                     wgu_pad, b_gate[layer], gla_norm_g[layer], state_gla[layer], SAMPLE_CHUNK, SAMPLE_CHUNK, t_new, SAMPLE_GLA_SEQS)
    og_s = og_s[:, :t_new, :].reshape(db * t_new, GLA_DV)
    od_s = _dsa_sample(zA_s, kd_s, vd_s, misc_s, kin_s, rel_bias, cache_k[layer], cache_v[layer],
                       cache_kidx[layer], page_table, db, t_new)
    x1_s, hf_s = _post(xs, og_s, zA_s, od_s, w_gla, w_dsa, w_out, g_ffn[layer])
    y_s = _ffn(hf_s, x1_s, w_fg, w_fu, w_fd, g_final).reshape(db, t_new, D_MODEL)

    heads = lambda a, n, t: a.reshape(1, n, t, DSA_HEADS, DSA_HEAD_DIM)
    heads_t = lambda a: a.reshape(nb, DSA_HEADS, DSA_HEAD_DIM, l).transpose(0, 3, 1, 2)[None]
    return (y_p, y_s,
            heads_t(kt), heads_t(vt), kin.reshape(1, nb, l, IDX_DIM), s_p[None],
            heads(kd_s, db, t_new), heads(vd_s, db, t_new), kin_s.reshape(1, db, t_new, IDX_DIM), s_s[None])
```
